```python
import math
import jax, jax.numpy as jnp
from jax import lax
import numpy as np

D_MODEL = 1024
BATCH = 8
SEQ = 4096
DEPTH = 1

D_MIX = D_MODEL
HEAD_DIM = 64
ATTN_HEADS = (D_MIX // 2) // HEAD_DIM
ATTN_DIM = ATTN_HEADS * HEAD_DIM
CONV_DIM = D_MIX - ATTN_DIM
CONV_K = 3
IDX_HEADS = 8
IDX_DIM = 64
IDX_SCALE = (IDX_DIM ** -0.5) * (IDX_HEADS ** -0.5)
TOPK_KEYS_MAX = 256
QUERY_BLOCK = 128
ROPE_THETA = 500000.0
ROPE_DIM = HEAD_DIM // 4
N_EXPERTS = 64
TOP_K_EXPERTS = 8
EXPERT_DIM = 256
SHARED_DIM = 256
ROUTED_SCALE = 2.5
DISPATCH_BLOCK = 128
EPS = 1e-6
IN_SPLITS = (CONV_DIM, CONV_DIM, CONV_DIM, ATTN_DIM, ATTN_DIM, ATTN_DIM,
             IDX_HEADS * IDX_DIM, IDX_DIM, IDX_HEADS)
IN_COLS = sum(IN_SPLITS)

kernel_name = "hymba_conv_dsa_moe_adaln_block"


def rms_norm(x, g):
    xf = x.astype(jnp.float32)
    y = xf * lax.rsqrt(jnp.mean(xf * xf, axis=-1, keepdims=True) + EPS)
    return y.astype(x.dtype) * g


def apply_partial_rope(x, positions):
    half = ROPE_DIM // 2
    inv_freq = ROPE_THETA ** (-jnp.arange(half, dtype=jnp.float32) / half)
    ang = positions.astype(jnp.float32)[..., None] * inv_freq
    ang = ang.reshape(ang.shape[:2] + (1,) * (x.ndim - 3) + (half,))
    cos, sin = jnp.cos(ang), jnp.sin(ang)
    x1 = x[..., :half].astype(jnp.float32)
    x2 = x[..., half:ROPE_DIM].astype(jnp.float32)
    rot = jnp.concatenate([x1 * cos - x2 * sin, x2 * cos + x1 * sin], axis=-1)
    return jnp.concatenate([rot.astype(x.dtype), x[..., ROPE_DIM:]], axis=-1)


def short_conv(u, w):
    up = jnp.pad(u, ((0, 0), (CONV_K - 1, 0), (0, 0)))
    return up[:, :-2] * w[0] + up[:, 1:-1] * w[1] + up[:, 2:] * w[2]


def dsa_attention(q, k, v, qi, ki, wi):
    B, S, H, Dh = q.shape
    n_sel = min(TOPK_KEYS_MAX, S // 4)
    n_blocks = S // QUERY_BLOCK
    key_pos = jnp.arange(S)

    def block(j):
        q0 = j * QUERY_BLOCK
        qb = lax.dynamic_slice_in_dim(q, q0, QUERY_BLOCK, axis=1)
        qib = lax.dynamic_slice_in_dim(qi, q0, QUERY_BLOCK, axis=1)
        wib = lax.dynamic_slice_in_dim(wi, q0, QUERY_BLOCK, axis=1)
        t_pos = q0 + jnp.arange(QUERY_BLOCK)
        rel = jax.nn.relu(jnp.einsum('bqhd,bsd->bqhs', qib, ki).astype(jnp.float32))
        score = jnp.einsum('bqhs,bqh->bqs', rel, wib.astype(jnp.float32)) * IDX_SCALE
        causal = key_pos[None, :] <= t_pos[:, None]
        score = jnp.where(causal[None], score, -jnp.inf)
        _, idx = lax.top_k(score, n_sel)
        valid = idx <= t_pos[None, :, None]
        flat = idx.reshape(B, QUERY_BLOCK * n_sel)
        ks = jax.vmap(lambda kb, ib: kb[ib])(k, flat).reshape(B, QUERY_BLOCK, n_sel, H, Dh)
        vs = jax.vmap(lambda vb, ib: vb[ib])(v, flat).reshape(B, QUERY_BLOCK, n_sel, H, Dh)
        s = jnp.einsum('bqhd,bqkhd->bhqk', qb, ks).astype(jnp.float32) * (Dh ** -0.5)
        s = jnp.where(valid[:, None], s, -jnp.inf)
        p = jax.nn.softmax(s, axis=-1).astype(vs.dtype)
        o = jnp.einsum('bhqk,bqkhd->bqhd', p, vs)
        return o.reshape(B, QUERY_BLOCK, H * Dh)

    out = lax.map(block, jnp.arange(n_blocks))
    return out.transpose(1, 0, 2, 3).reshape(B, S, H * Dh)


def swiglu(x, w1, w3, w2):
    return (jax.nn.silu(x @ w1) * (x @ w3)) @ w2


def moe(h, w_router, router_bias, w1, w3, w2, ws1, ws3, ws2):
    B, S, D = h.shape
    t = h.reshape(-1, D)
    n_tok = t.shape[0]
    scores = jax.nn.sigmoid((t @ w_router).astype(jnp.float32))
    _, eidx = lax.top_k(scores + router_bias.astype(jnp.float32), TOP_K_EXPERTS)
    g = jnp.take_along_axis(scores, eidx, axis=1)
    g = g / jnp.sum(g, axis=-1, keepdims=True) * ROUTED_SCALE
    n_pair = n_tok * TOP_K_EXPERTS
    flat_e = eidx.reshape(-1)
    flat_tok = jnp.arange(n_pair, dtype=jnp.int32) // TOP_K_EXPERTS
    flat_g = g.reshape(-1).astype(t.dtype)
    order = jnp.argsort(flat_e)
    se, stok, sg = flat_e[order], flat_tok[order], flat_g[order]
    counts = jnp.bincount(flat_e, length=N_EXPERTS)
    padded = (counts + DISPATCH_BLOCK - 1) // DISPATCH_BLOCK * DISPATCH_BLOCK
    start = jnp.cumsum(counts) - counts
    pend = jnp.cumsum(padded)
    pstart = pend - padded
    dest = pstart[se] + jnp.arange(n_pair) - start[se]
    rows = n_pair + N_EXPERTS * DISPATCH_BLOCK
    n_blk = rows // DISPATCH_BLOCK
    row_tok = jnp.zeros((rows,), jnp.int32).at[dest].set(stok)
    row_g = jnp.zeros((rows,), t.dtype).at[dest].set(sg)
    block_e = jnp.clip(jnp.searchsorted(pend, jnp.arange(n_blk) * DISPATCH_BLOCK, side='right'),
                       0, N_EXPERTS - 1)

    def expert_block(args):
        e, toks, gs = args
        xb = t[toks]
        return swiglu(xb, w1[e], w3[e], w2[e]) * gs[:, None]

    out = lax.map(expert_block, (block_e, row_tok.reshape(n_blk, DISPATCH_BLOCK),
                                 row_g.reshape(n_blk, DISPATCH_BLOCK)))
    routed = jax.ops.segment_sum(out.reshape(rows, D), row_tok, num_segments=n_tok)
    shared = swiglu(t, ws1, ws3, ws2)
    return (routed + shared).reshape(B, S, D)


def hybrid_layer(x, c, positions, norm1_g, norm2_g, w_ada, b_ada, w_in, conv_w,
                 q_norm_g, k_norm_g, kidx_norm_g, w_out, w_router, router_bias,
                 w1, w3, w2, ws1, ws3, ws2):
    B, S, D = x.shape
    ada = jax.nn.silu(c) @ w_ada + b_ada
    shift1, scale1, gate1, shift2, scale2, gate2 = [a[:, None, :] for a in jnp.split(ada, 6, axis=-1)]

    h = rms_norm(x, norm1_g) * (1 + scale1) + shift1
    proj = h @ w_in
    cuts = list(np.cumsum(IN_SPLITS)[:-1])
    xc, bg, cg, q, k, v, qi, ki, wi = jnp.split(proj, cuts, axis=-1)
    conv_out = bg * short_conv(cg * xc, conv_w)
    q = apply_partial_rope(rms_norm(q.reshape(B, S, ATTN_HEADS, HEAD_DIM), q_norm_g), positions)
    k = apply_partial_rope(rms_norm(k.reshape(B, S, ATTN_HEADS, HEAD_DIM), k_norm_g), positions)
    v = v.reshape(B, S, ATTN_HEADS, HEAD_DIM)
    qi = apply_partial_rope(qi.reshape(B, S, IDX_HEADS, IDX_DIM), positions)
    ki = apply_partial_rope(rms_norm(ki, kidx_norm_g), positions)
    attn_out = dsa_attention(q, k, v, qi, ki, wi)
    mix = jnp.concatenate([conv_out, attn_out], axis=-1) @ w_out
    x = x + gate1 * mix

    h2 = rms_norm(x, norm2_g) * (1 + scale2) + shift2
    x = x + gate2 * moe(h2, w_router, router_bias, w1, w3, w2, ws1, ws3, ws2)
    return x


def setup_inputs(seed: int = 0) -> dict:
    key = jax.random.key(seed)
    ks = jax.random.split(key, 24)
    f32 = jnp.float32
    L, D, E, F, Fs = DEPTH, D_MODEL, N_EXPERTS, EXPERT_DIM, SHARED_DIM

    def nrm(k, shape, scale):
        return jax.random.normal(k, shape, f32) * scale

    positions = (jax.random.randint(ks[2], (BATCH, 1), 0, 1024, dtype=jnp.int32)
                 + jnp.arange(SEQ, dtype=jnp.int32)[None, :])
    return {
        "x": nrm(ks[0], (BATCH, SEQ, D), 1.0),
        "c": nrm(ks[1], (BATCH, D), 1.0),
        "positions": positions,
        "norm1_g": 1.0 + nrm(ks[3], (L, D), 0.02),
        "norm2_g": 1.0 + nrm(ks[4], (L, D), 0.02),
        "w_ada": nrm(ks[5], (L, D, 6 * D), 0.5 * D ** -0.5),
        "b_ada": nrm(ks[6], (L, 6 * D), 0.02),
        "w_in": nrm(ks[7], (L, D, IN_COLS), D ** -0.5),
        "conv_w": nrm(ks[8], (L, CONV_K, CONV_DIM), CONV_K ** -0.5),
        "q_norm_g": 1.0 + nrm(ks[9], (L, HEAD_DIM), 0.02),
        "k_norm_g": 1.0 + nrm(ks[10], (L, HEAD_DIM), 0.02),
        "kidx_norm_g": 1.0 + nrm(ks[11], (L, IDX_DIM), 0.02),
        "w_out": nrm(ks[12], (L, D_MIX, D), D_MIX ** -0.5),
        "w_router": nrm(ks[13], (L, D, E), D ** -0.5),
        "router_bias": nrm(ks[14], (L, E), 0.01),
        "w1": nrm(ks[15], (L, E, D, F), D ** -0.5),
        "w3": nrm(ks[16], (L, E, D, F), D ** -0.5),
        "w2": nrm(ks[17], (L, E, F, D), F ** -0.5),
        "ws1": nrm(ks[18], (L, D, Fs), D ** -0.5),
        "ws3": nrm(ks[19], (L, D, Fs), D ** -0.5),
        "ws2": nrm(ks[20], (L, Fs, D), Fs ** -0.5),
    }


def reference(x, c, positions, norm1_g, norm2_g, w_ada, b_ada, w_in, conv_w,
              q_norm_g, k_norm_g, kidx_norm_g, w_out, w_router, router_bias,
              w1, w3, w2, ws1, ws3, ws2):
    for l in range(DEPTH):
        x = hybrid_layer(x, c, positions, norm1_g[l], norm2_g[l], w_ada[l], b_ada[l],
                         w_in[l], conv_w[l], q_norm_g[l], k_norm_g[l], kidx_norm_g[l],
                         w_out[l], w_router[l], router_bias[l], w1[l], w3[l], w2[l],
                         ws1[l], ws3[l], ws2[l])
    return x
```

```python
import functools

import jax
import jax.numpy as jnp
from jax import lax
from jax.experimental import pallas as pl
from jax.experimental.pallas import tpu as pltpu

F32 = jnp.float32
BF16 = jnp.bfloat16

HEAD_DIM = 64
ATTN_HEADS = 8
ATTN_DIM = ATTN_HEADS * HEAD_DIM
CONV_DIM = 512
IDX_HEADS = 8
IDX_DIM = 64
IDX_SCALE = (IDX_DIM ** -0.5) * (IDX_HEADS ** -0.5)
TOPK_KEYS_MAX = 256
ROPE_THETA = 500000.0
ROPE_DIM = HEAD_DIM // 4
ROPE_HALF = ROPE_DIM // 2
N_EXPERTS = 64
TOP_K_EXPERTS = 8
EXPERT_DIM = 256
ROUTED_SCALE = 2.5
EPS = 1e-6

LANES = 128
SUBLANES = 8
V7X_VMEM_BYTES = 64 * 1024 * 1024
VMEM_LIMIT = V7X_VMEM_BYTES * 3 // 4

MASKED = -1e30


def _params(*semantics):
    return pltpu.CompilerParams(dimension_semantics=semantics, vmem_limit_bytes=VMEM_LIMIT)


def _dot(a, b):
    return jnp.dot(a, b, preferred_element_type=F32)


def _dot_t(a, b):
    return lax.dot_general(a, b, (((1,), (1,)), ((), ())), preferred_element_type=F32)


def _split(a):
    hi = a.astype(BF16)
    lo = (a - hi.astype(F32)).astype(BF16)
    return hi, lo


def _dot3(a, b):
    a_hi, a_lo = _split(a)
    b_hi, b_lo = _split(b)
    return _dot(a_hi, b_hi) + _dot(a_hi, b_lo) + _dot(a_lo, b_hi)


def _silu(v):
    return v * jax.nn.sigmoid(v)


def _rms_mod(xv, g, scale, shift):
    ms = jnp.mean(xv * xv, axis=-1, keepdims=True)
    y = xv * lax.rsqrt(ms + EPS)
    return (y * g) * (1.0 + scale) + shift


def _ada_kernel(c_ref, w_ref, b_ref, o_ref):
    o_ref[...] = _dot3(_silu(c_ref[...]), w_ref[...]) + b_ref[...]


def _ada(c, w_ada, b_ada):
    bsz, d = c.shape
    n = w_ada.shape[1]
    bn = n // 4
    return pl.pallas_call(
        _ada_kernel,
        out_shape=jax.ShapeDtypeStruct((bsz, n), F32),
        grid=(n // bn,),
        in_specs=[
            pl.BlockSpec((bsz, d), lambda i: (0, 0)),
            pl.BlockSpec((d, bn), lambda i: (0, i)),
            pl.BlockSpec((1, bn), lambda i: (0, i)),
        ],
        out_specs=pl.BlockSpec((bsz, bn), lambda i: (0, i)),
        compiler_params=_params("parallel"),
        name="ada",
    )(c, w_ada, b_ada.reshape(1, n))


def _rope_kernel(pos_ref, invf_ref, c_ref, s1_ref, s2_ref):
    ang = pos_ref[...].astype(F32) * invf_ref[...]
    d = lax.broadcasted_iota(jnp.int32, ang.shape, 1) & (HEAD_DIM - 1)
    cos = jnp.cos(ang)
    sin = jnp.sin(ang)
    c_ref[...] = jnp.where(d < ROPE_DIM, cos, 1.0)
    s1_ref[...] = jnp.where(d < ROPE_HALF, -sin, 0.0)
    s2_ref[...] = jnp.where(d < ROPE_HALF, 0.0, jnp.where(d < ROPE_DIM, sin, 0.0))


def _rope_tables(pos, invf_lane, tm):
    t = pos.shape[0]
    spec = pl.BlockSpec((tm, LANES), lambda i: (i, 0))
    shp = jax.ShapeDtypeStruct((t, LANES), F32)
    return pl.pallas_call(
        _rope_kernel,
        out_shape=(shp, shp, shp),
        grid=(t // tm,),
        in_specs=[pl.BlockSpec((tm, 1), lambda i: (i, 0)), pl.BlockSpec((1, LANES), lambda i: (0, 0))],
        out_specs=(spec, spec, spec),
        compiler_params=_params("parallel"),
        name="rope_tables",
    )(pos, invf_lane)


def _rope(y, c, s1, s2):
    return y * c + pltpu.roll(y, LANES - ROPE_HALF, 1) * s1 + pltpu.roll(y, ROPE_HALF, 1) * s2


def _head_rms(xb, avg):
    hi, lo = _split(xb * xb)
    ms = _dot(hi, avg) + _dot(lo, avg)
    return xb * lax.rsqrt(ms + EPS)


def _inproj_kernel(x_ref, xh_ref, sc_ref, sh_ref, g1_ref, wmix_ref, wq_ref, wk_ref, wv_ref, wqi_ref, wl_ref,
                   cw_ref, qg_ref, kg_ref, kig_ref, ct_ref, s1_ref, s2_ref, avg_ref,
                   conv_o, q_o, k_o, v_o, qi_o, kilo_o, kihi_o, wi_o, *, seq):
    tm = x_ref.shape[0]
    scale = sc_ref[0]
    shift = sh_ref[0]
    g1 = g1_ref[...]
    h = _rms_mod(x_ref[...], g1, scale, shift).astype(BF16)
    hh = _rms_mod(xh_ref[...], g1, scale, shift).astype(BF16)

    mix = _dot(h, wmix_ref[...])
    mixh = _dot(hh, wmix_ref[...])
    u = mix[:, 2 * CONV_DIM:] * mix[:, :CONV_DIM]
    uh = mixh[:, 2 * CONV_DIM:] * mixh[:, :CONV_DIM]
    seq_start = (pl.program_id(0) * tm) % seq == 0
    uh = jnp.where(seq_start, 0.0, uh)
    ext = jnp.concatenate([uh, u], axis=0)
    u1 = pltpu.roll(ext, 1, 0)[SUBLANES:]
    u2 = pltpu.roll(ext, 2, 0)[SUBLANES:]
    conv = u2 * cw_ref[0:1, :] + u1 * cw_ref[1:2, :] + u * cw_ref[2:3, :]
    conv_o[...] = (mix[:, CONV_DIM:2 * CONV_DIM] * conv).astype(BF16)

    ct = ct_ref[...]
    s1 = s1_ref[...]
    s2 = s2_ref[...]
    avg = avg_ref[...]
    qf = _dot(h, wq_ref[...])
    kf = _dot(h, wk_ref[...])
    qif = _dot(h, wqi_ref[...])
    for p in range(ATTN_DIM // LANES):
        sl = slice(p * LANES, (p + 1) * LANES)
        qn = _rope(_head_rms(qf[:, sl], avg) * qg_ref[...], ct, s1, s2)
        q_o[:, sl] = (qn * (HEAD_DIM ** -0.5)).astype(BF16)
        kn = _rope(_head_rms(kf[:, sl], avg) * kg_ref[...], ct, s1, s2)
        k_o[:, sl] = kn.astype(BF16)
        qi_o[:, sl] = _rope(qif[:, sl], ct, s1, s2).astype(BF16)
    v_o[...] = _dot(h, wv_ref[...]).astype(BF16)

    last = _dot(h, wl_ref[...])
    lane = lax.broadcasted_iota(jnp.int32, last.shape, 1)
    is_key = lane < IDX_DIM
    kin = _head_rms(last, avg) * kig_ref[...]
    kin = _rope(kin, jnp.where(is_key, ct, 1.0), jnp.where(is_key, s1, 0.0), jnp.where(is_key, s2, 0.0))
    klo = jnp.where(is_key, kin, 0.0)
    kilo_o[...] = klo.astype(BF16)
    kihi_o[...] = pltpu.roll(klo, IDX_DIM, 1).astype(BF16)
    wi_o[...] = pltpu.roll(last, LANES - IDX_DIM, 1)[:, :IDX_HEADS]


def _inproj(x2, scale1, shift1, norm1_g, w_in, conv_w, q_norm_g, k_norm_g, kidx_norm_g, ct, s1, s2, seq, tm):
    t, d = x2.shape
    cuts = [0, 3 * CONV_DIM, 3 * CONV_DIM + ATTN_DIM, 3 * CONV_DIM + 2 * ATTN_DIM, 3 * CONV_DIM + 3 * ATTN_DIM,
            3 * CONV_DIM + 3 * ATTN_DIM + IDX_HEADS * IDX_DIM]
    wb = w_in.astype(BF16)
    wmix, wq, wk, wv, wqi = [wb[:, a:b] for a, b in zip(cuts[:-1], cuts[1:])]
    wl = wb[:, cuts[-1]:]
    wl = jnp.pad(wl, ((0, 0), (0, LANES - wl.shape[1])))
    ones = jnp.ones((1, LANES - IDX_DIM), F32)
    qg = jnp.tile(q_norm_g.reshape(1, HEAD_DIM), (1, 2))
    kg = jnp.tile(k_norm_g.reshape(1, HEAD_DIM), (1, 2))
    kig = jnp.concatenate([kidx_norm_g.reshape(1, IDX_DIM), ones], axis=1)
    blk = jnp.arange(LANES) // HEAD_DIM
    avg = jnp.where(blk[:, None] == blk[None, :], 1.0 / HEAD_DIM, 0.0).astype(BF16)

    bsz = t // seq
    per_b = seq // tm
    row = lambda w: pl.BlockSpec((tm, w), lambda i: (i, 0))
    full = lambda a: pl.BlockSpec(a.shape, lambda i: (0,) * a.ndim)
    mod = pl.BlockSpec((1, 1, d), lambda i: (i // per_b, 0, 0))
    halo = pl.BlockSpec((SUBLANES, d), lambda i: (jnp.maximum(i * (tm // SUBLANES) - 1, 0), 0))
    g1 = norm1_g.reshape(1, d)
    out_shape = (
        jax.ShapeDtypeStruct((t, CONV_DIM), BF16),
        jax.ShapeDtypeStruct((t, ATTN_DIM), BF16),
        jax.ShapeDtypeStruct((t, ATTN_DIM), BF16),
        jax.ShapeDtypeStruct((t, ATTN_DIM), BF16),
        jax.ShapeDtypeStruct((t, IDX_HEADS * IDX_DIM), BF16),
        jax.ShapeDtypeStruct((t, LANES), BF16),
        jax.ShapeDtypeStruct((t, LANES), BF16),
        jax.ShapeDtypeStruct((t, IDX_HEADS), F32),
    )
    del bsz
    return pl.pallas_call(
        functools.partial(_inproj_kernel, seq=seq),
        out_shape=out_shape,
        grid=(t // tm,),
        in_specs=[row(d), halo, mod, mod, full(g1), full(wmix), full(wq), full(wk), full(wv), full(wqi), full(wl),
                  full(conv_w), full(qg), full(kg), full(kig), row(LANES), row(LANES), row(LANES), full(avg)],
        out_specs=(row(CONV_DIM), row(ATTN_DIM), row(ATTN_DIM), row(ATTN_DIM), row(IDX_HEADS * IDX_DIM),
                   row(LANES), row(LANES), row(IDX_HEADS)),
        compiler_params=_params("parallel"),
        name="inproj",
    )(x2, x2, scale1, shift1, g1, wmix, wq, wk, wv, wqi, wl, conv_w, qg, kg, kig, ct, s1, s2, avg)


def _ukey_to_f32(u):
    s = u ^ jnp.int32(-2 ** 31)
    bits = s ^ ((s >> 31) & jnp.int32(0x7FFFFFFF))
    return lax.bitcast_convert_type(bits, F32)


def _attn_kernel(q_ref, k_ref, v_ref, qi_ref, kilo_ref, kihi_ref, wi_ref, o_ref,
                 sc_ref, wb_ref, qm_ref, acc_ref, m_ref, l_ref, oacc_ref, *, n_sel, rb):
    tq = q_ref.shape[0]
    seq = k_ref.shape[0]
    kc = sc_ref.shape[2]
    lb_n = kc // LANES
    j = pl.program_id(1)
    nch = (j + 1) * (tq // kc)
    lane = lax.broadcasted_iota(jnp.int32, (tq, LANES), 1)

    for h in range(IDX_HEADS):
        wb_ref[h] = jnp.broadcast_to(wi_ref[:, h:h + 1], (tq, LANES))
    for h in range(ATTN_HEADS):
        qp = q_ref[:, (h // 2) * LANES:(h // 2 + 1) * LANES].astype(F32)
        keep = (lane < HEAD_DIM) if h % 2 == 0 else (lane >= HEAD_DIM)
        qm_ref[h] = jnp.where(keep, qp, 0.0).astype(BF16)

    t_pos = j * tq + lax.broadcasted_iota(jnp.int32, (tq, kc), 0)
    k_off = lax.broadcasted_iota(jnp.int32, (tq, kc), 1)

    def index_chunk(c, carry):
        r0 = pl.multiple_of(c * kc, kc)
        klo = kilo_ref[pl.ds(r0, kc), :]
        khi = kihi_ref[pl.ds(r0, kc), :]
        acc = jnp.zeros((tq, kc), F32)
        for p in range(IDX_HEADS // 2):
            qip = qi_ref[:, p * LANES:(p + 1) * LANES]
            w_even = jnp.concatenate([wb_ref[2 * p]] * lb_n, axis=1)
            w_odd = jnp.concatenate([wb_ref[2 * p + 1]] * lb_n, axis=1)
            acc = acc + jnp.maximum(_dot_t(qip, klo), 0.0) * w_even
            acc = acc + jnp.maximum(_dot_t(qip, khi), 0.0) * w_odd
        score = acc * IDX_SCALE
        score = jnp.where(r0 + k_off <= t_pos, score, -jnp.inf)
        sc_ref[c] = jnp.where(score == 0.0, 0.0, score)
        return carry

    lax.fori_loop(0, nch, index_chunk, 0)

    def select_rows(r, carry):
        r0 = pl.multiple_of(r * rb, rb)
        rows = pl.ds(r0, rb)
        tpos = j * tq + r0 + lax.broadcasted_iota(jnp.int32, (rb, 1), 0)
        k_row = jnp.minimum(tpos + 1, n_sel).astype(F32)
        kidx0 = lax.broadcasted_iota(jnp.int32, (rb, LANES), 1)

        def count(pred):
            def chunk(c, acc):
                blk = sc_ref[c, rows, :]
                for b in range(lb_n):
                    kidx = kidx0 + (c * kc + b * LANES)
                    acc = acc + jnp.where(pred(blk[:, b * LANES:(b + 1) * LANES], kidx), 1.0, 0.0)
                return acc
            acc = lax.fori_loop(0, nch, chunk, jnp.zeros((rb, LANES), F32))
            return jnp.sum(acc, axis=1, keepdims=True)

        def value_bit(i, p):
            cand = p | jnp.left_shift(jnp.int32(1), 31 - i)
            cb = jnp.broadcast_to(_ukey_to_f32(cand), (rb, LANES))
            cnt = count(lambda v, kidx: v >= cb)
            return jnp.where(cnt >= k_row, cand, p)

        thr = _ukey_to_f32(lax.fori_loop(0, 32, value_bit, jnp.zeros((rb, 1), jnp.int32)))
        thr_b = jnp.broadcast_to(thr, (rb, LANES))
        n_ge = count(lambda v, kidx: v >= thr_b)
        tied = jnp.max(jnp.where(n_ge > k_row, 1.0, 0.0)) > 0.0

        def tie_cut():
            need = k_row - count(lambda v, kidx: v > thr_b)

            def index_bit(i, p):
                cand = p | jnp.left_shift(jnp.int32(1), (seq - 1).bit_length() - 1 - i)
                cb = jnp.broadcast_to(cand, (rb, LANES))
                cnt = count(lambda v, kidx: jnp.where(v == thr_b, kidx, seq) < cb)
                return jnp.where(cnt < need, cand, p)

            return lax.fori_loop(0, (seq - 1).bit_length(), index_bit, jnp.zeros((rb, 1), jnp.int32))

        cut = lax.cond(tied, tie_cut, lambda: jnp.full((rb, 1), seq, jnp.int32))
        cut_b = jnp.broadcast_to(cut, (rb, LANES))

        def write_bias(c, carry2):
            blk = sc_ref[c, rows, :]
            parts = []
            for b in range(lb_n):
                v = blk[:, b * LANES:(b + 1) * LANES]
                kidx = kidx0 + (c * kc + b * LANES)
                tie_bias = jnp.where(kidx <= cut_b, 0.0, MASKED)
                parts.append(jnp.where(v > thr_b, 0.0, jnp.where(v == thr_b, tie_bias, MASKED)))
            sc_ref[c, rows, :] = jnp.concatenate(parts, axis=1)
            return carry2

        lax.fori_loop(0, nch, write_bias, 0)
        return carry

    lax.fori_loop(0, tq // rb, select_rows, 0)

    for h in range(ATTN_HEADS):
        cols = slice((h // 2) * LANES, (h // 2 + 1) * LANES)
        m_ref[...] = jnp.full((tq, 1), MASKED, F32)
        l_ref[...] = jnp.zeros((tq, 1), F32)
        acc_ref[...] = jnp.zeros((tq, LANES), F32)

        def attend(c, carry, h=h, cols=cols):
            r0 = pl.multiple_of(c * kc, kc)
            s = _dot_t(qm_ref[h], k_ref[pl.ds(r0, kc), cols]) + sc_ref[c]
            m_old = m_ref[...]
            m_new = jnp.maximum(m_old, jnp.max(s, axis=1, keepdims=True))
            p = jnp.exp(s - m_new)
            alpha = jnp.exp(m_old - m_new)
            l_ref[...] = alpha * l_ref[...] + jnp.sum(p, axis=1, keepdims=True)
            acc_ref[...] = alpha * acc_ref[...] + _dot(p.astype(BF16), v_ref[pl.ds(r0, kc), cols])
            m_ref[...] = m_new
            return carry

        lax.fori_loop(0, nch, attend, 0)
        o_h = acc_ref[...] / l_ref[...]
        if h % 2 == 0:
            oacc_ref[:, cols] = o_h
        else:
            oacc_ref[:, cols] = jnp.where(lane < HEAD_DIM, oacc_ref[:, cols], o_h)
    o_ref[...] = oacc_ref[...].astype(BF16)


def _attention(q, k, v, qi, kilo, kihi, wi, bsz, seq, tq, kc, rb):
    n_sel = min(TOPK_KEYS_MAX, seq // 4)
    shape3 = lambda a: a.reshape(bsz, seq, a.shape[-1])
    q, k, v, qi, kilo, kihi, wi = map(shape3, (q, k, v, qi, kilo, kihi, wi))
    qblk = lambda w: pl.BlockSpec((None, tq, w), lambda b, j: (b, j, 0))
    kblk = lambda w: pl.BlockSpec((None, seq, w), lambda b, j: (b, 0, 0))
    out = pl.pallas_call(
        functools.partial(_attn_kernel, n_sel=n_sel, rb=rb),
        out_shape=jax.ShapeDtypeStruct((bsz, seq, ATTN_DIM), BF16),
        grid=(bsz, seq // tq),
        in_specs=[qblk(ATTN_DIM), kblk(ATTN_DIM), kblk(ATTN_DIM), qblk(IDX_HEADS * IDX_DIM), kblk(LANES), kblk(LANES),
                  qblk(IDX_HEADS)],
        out_specs=qblk(ATTN_DIM),
        scratch_shapes=[
            pltpu.VMEM((seq // kc, tq, kc), F32),
            pltpu.VMEM((IDX_HEADS, tq, LANES), F32),
            pltpu.VMEM((ATTN_HEADS, tq, LANES), BF16),
            pltpu.VMEM((tq, LANES), F32),
            pltpu.VMEM((tq, 1), F32),
            pltpu.VMEM((tq, 1), F32),
            pltpu.VMEM((tq, ATTN_DIM), F32),
        ],
        compiler_params=_params("parallel", "arbitrary"),
        name="dsa_attention",
    )(q, k, v, qi, kilo, kihi, wi)
    return out.reshape(bsz * seq, ATTN_DIM)


def _outproj_kernel(conv_ref, attn_ref, x_ref, gate1_ref, sc_ref, sh_ref, g2_ref, wout_ref, wr_ref, rbias_ref,
                    x1_o, h2_o, gates_o):
    mix = _dot(conv_ref[...], wout_ref[:CONV_DIM, :]) + _dot(attn_ref[...], wout_ref[CONV_DIM:, :])
    x1 = x_ref[...] + gate1_ref[0] * mix
    x1_o[...] = x1
    h2 = _rms_mod(x1, g2_ref[...], sc_ref[0], sh_ref[0])
    h2_o[...] = h2.astype(BF16)

    scores = jax.nn.sigmoid(_dot3(h2, wr_ref[...]))
    work = scores + rbias_ref[...]
    lane = lax.broadcasted_iota(jnp.int32, work.shape, 1).astype(F32)
    chosen = jnp.zeros(work.shape, F32)
    for _ in range(TOP_K_EXPERTS):
        mx = jnp.max(work, axis=1, keepdims=True)
        first = jnp.min(jnp.where(work == mx, lane, float(N_EXPERTS)), axis=1, keepdims=True)
        onehot = lane == first
        chosen = jnp.where(onehot, 1.0, chosen)
        work = jnp.where(onehot, -jnp.inf, work)
    g = jnp.where(chosen > 0.0, scores, 0.0)
    gates_o[...] = g / jnp.sum(g, axis=1, keepdims=True) * ROUTED_SCALE


def _outproj(conv, attn, x2, gate1, scale2, shift2, norm2_g, w_out, w_router, router_bias, seq, tm):
    t, d = x2.shape
    e = w_router.shape[1]
    per_b = seq // tm
    row = lambda w: pl.BlockSpec((tm, w), lambda i: (i, 0))
    full = lambda a: pl.BlockSpec(a.shape, lambda i: (0,) * a.ndim)
    mod = pl.BlockSpec((1, 1, d), lambda i: (i // per_b, 0, 0))
    g2 = norm2_g.reshape(1, d)
    wo = w_out.astype(BF16)
    rbias = router_bias.reshape(1, e)
    return pl.pallas_call(
        _outproj_kernel,
        out_shape=(jax.ShapeDtypeStruct((t, d), F32), jax.ShapeDtypeStruct((t, d), BF16),
                   jax.ShapeDtypeStruct((t, e), F32)),
        grid=(t // tm,),
        in_specs=[row(CONV_DIM), row(ATTN_DIM), row(d), mod, mod, mod, full(g2), full(wo), full(w_router), full(rbias)],
        out_specs=(row(d), row(d), row(e)),
        compiler_params=_params("parallel"),
        name="outproj_router",
    )(conv, attn, x2, gate1, scale2, shift2, g2, wo, w_router, rbias)


def _moe_kernel(h2_ref, x1_ref, gates_ref, gate2_ref, w13_ref, w2_ref, ws13_ref, ws2_ref, o_ref, acc_ref):
    e = pl.program_id(1)
    f = w2_ref.shape[1]
    h2 = h2_ref[...]

    @pl.when(e == 0)
    def _():
        hs = _dot(h2, ws13_ref[...])
        act = _silu(hs[:, :f]) * hs[:, f:]
        acc_ref[...] = _dot(act.astype(BF16), ws2_ref[...])

    hh = _dot(h2, w13_ref[0])
    act = _silu(hh[:, :f]) * hh[:, f:]
    y = _dot(act.astype(BF16), w2_ref[0])
    gates = gates_ref[...]
    lane = lax.broadcasted_iota(jnp.int32, gates.shape, 1)
    g_col = jnp.sum(jnp.where(lane == e, gates, 0.0), axis=1, keepdims=True)
    acc_ref[...] += y * g_col

    @pl.when(e == pl.num_programs(1) - 1)
    def _():
        o_ref[...] = x1_ref[...] + gate2_ref[0] * acc_ref[...]


def _moe(h2, x1, gates, gate2, w1, w3, w2, ws1, ws3, ws2, seq, tm):
    t, d = x1.shape
    n_e, _, f = w1.shape
    w13 = jnp.concatenate([w1, w3], axis=2).astype(BF16)
    w2b = w2.astype(BF16)
    ws13 = jnp.concatenate([ws1, ws3], axis=1).astype(BF16)
    ws2b = ws2.astype(BF16)
    per_b = seq // tm
    row = lambda w: pl.BlockSpec((tm, w), lambda i, e: (i, 0))
    full = lambda a: pl.BlockSpec(a.shape, lambda i, e: (0,) * a.ndim)
    return pl.pallas_call(
        _moe_kernel,
        out_shape=jax.ShapeDtypeStruct((t, d), F32),
        grid=(t // tm, n_e),
        in_specs=[row(d), row(d), row(n_e), pl.BlockSpec((1, 1, d), lambda i, e: (i // per_b, 0, 0)),
                  pl.BlockSpec((1, d, 2 * f), lambda i, e: (e, 0, 0)), pl.BlockSpec((1, f, d), lambda i, e: (e, 0, 0)),
                  full(ws13), full(ws2b)],
        out_specs=row(d),
        scratch_shapes=[pltpu.VMEM((tm, d), F32)],
        compiler_params=_params("parallel", "arbitrary"),
        name="moe_dense",
    )(h2, x1, gates, gate2, w13, w2b, ws13, ws2b)


def _layer(x, c, positions, norm1_g, norm2_g, w_ada, b_ada, w_in, conv_w, q_norm_g, k_norm_g, kidx_norm_g, w_out,
           w_router, router_bias, w1, w3, w2, ws1, ws3, ws2):
    bsz, seq, d = x.shape
    t = bsz * seq
    tm = min(512, seq)
    tq = min(512, seq)
    x2 = x.reshape(t, d)

    ada = _ada(c, w_ada, b_ada)
    shift1, scale1, gate1, shift2, scale2, gate2 = [a.reshape(bsz, 1, d) for a in jnp.split(ada, 6, axis=-1)]

    inv_freq = ROPE_THETA ** (-jnp.arange(ROPE_HALF, dtype=F32) / ROPE_HALF)
    invf_lane = inv_freq[(jnp.arange(LANES) % HEAD_DIM) % ROPE_HALF].reshape(1, LANES)
    ct, s1, s2 = _rope_tables(positions.reshape(t, 1), invf_lane, min(2048, t))

    conv, q, k, v, qi, kilo, kihi, wi = _inproj(x2, scale1, shift1, norm1_g, w_in, conv_w, q_norm_g, k_norm_g,
                                                kidx_norm_g, ct, s1, s2, seq, tm)
    attn = _attention(q, k, v, qi, kilo, kihi, wi, bsz, seq, tq, kc=tq, rb=64)
    x1, h2, gates = _outproj(conv, attn, x2, gate1, scale2, shift2, norm2_g, w_out, w_router, router_bias, seq, tm)
    out = _moe(h2, x1, gates, gate2, w1, w3, w2, ws1, ws3, ws2, seq, min(1024, seq))
    return out.reshape(bsz, seq, d)


def kernel(x, c, positions, norm1_g, norm2_g, w_ada, b_ada, w_in, conv_w, q_norm_g, k_norm_g, kidx_norm_g, w_out,
           w_router, router_bias, w1, w3, w2, ws1, ws3, ws2):
    for l in range(w_in.shape[0]):
        x = _layer(x, c, positions, norm1_g[l], norm2_g[l], w_ada[l], b_ada[l], w_in[l], conv_w[l], q_norm_g[l],
                   k_norm_g[l], kidx_norm_g[l], w_out[l], w_router[l], router_bias[l], w1[l], w3[l], w2[l], ws1[l],
                   ws3[l], ws2[l])
    return x
```

```python
import functools

import jax
import jax.numpy as jnp
from jax import lax
from jax.experimental import pallas as pl
from jax.experimental.pallas import tpu as pltpu

F32 = jnp.float32
BF16 = jnp.bfloat16

HEAD_DIM = 64
ATTN_HEADS = 8
ATTN_DIM = ATTN_HEADS * HEAD_DIM
CONV_DIM = 512
IDX_HEADS = 8
IDX_DIM = 64
IDX_SCALE = (IDX_DIM ** -0.5) * (IDX_HEADS ** -0.5)
TOPK_KEYS_MAX = 256
ROPE_THETA = 500000.0
ROPE_DIM = HEAD_DIM // 4
ROPE_HALF = ROPE_DIM // 2
N_EXPERTS = 64
TOP_K_EXPERTS = 8
EXPERT_DIM = 256
ROUTED_SCALE = 2.5
EPS = 1e-6

LANES = 128
SUBLANES = 8
V7X_VMEM_BYTES = 64 * 1024 * 1024
VMEM_LIMIT = V7X_VMEM_BYTES * 3 // 4

MASKED = -1e30


def _params(*semantics):
    return pltpu.CompilerParams(dimension_semantics=semantics, vmem_limit_bytes=VMEM_LIMIT)


def _dot(a, b):
    return jnp.dot(a, b, preferred_element_type=F32)


def _dot_t(a, b):
    return lax.dot_general(a, b, (((1,), (1,)), ((), ())), preferred_element_type=F32)


def _split(a):
    hi = a.astype(BF16)
    lo = (a - hi.astype(F32)).astype(BF16)
    return hi, lo


def _dot3(a, b):
    a_hi, a_lo = _split(a)
    b_hi, b_lo = _split(b)
    return _dot(a_hi, b_hi) + _dot(a_hi, b_lo) + _dot(a_lo, b_hi)


def _silu(v):
    return v * jax.nn.sigmoid(v)


def _rms_mod(xv, g, scale, shift):
    ms = jnp.mean(xv * xv, axis=-1, keepdims=True)
    y = xv * lax.rsqrt(ms + EPS)
    return (y * g) * (1.0 + scale) + shift


def _ada_kernel(c_ref, w_ref, b_ref, o_ref):
    o_ref[...] = _dot3(_silu(c_ref[...]), w_ref[...]) + b_ref[...]


def _ada(c, w_ada, b_ada):
    bsz, d = c.shape
    n = w_ada.shape[1]
    bn = n // 4
    return pl.pallas_call(
        _ada_kernel,
        out_shape=jax.ShapeDtypeStruct((bsz, n), F32),
        grid=(n // bn,),
        in_specs=[
            pl.BlockSpec((bsz, d), lambda i: (0, 0)),
            pl.BlockSpec((d, bn), lambda i: (0, i)),
            pl.BlockSpec((1, bn), lambda i: (0, i)),
        ],
        out_specs=pl.BlockSpec((bsz, bn), lambda i: (0, i)),
        compiler_params=_params("parallel"),
        name="ada",
    )(c, w_ada, b_ada.reshape(1, n))


def _rope_kernel(pos_ref, invf_ref, c_ref, s1_ref, s2_ref):
    ang = pos_ref[...].astype(F32) * invf_ref[...]
    d = lax.broadcasted_iota(jnp.int32, ang.shape, 1) & (HEAD_DIM - 1)
    cos = jnp.cos(ang)
    sin = jnp.sin(ang)
    c_ref[...] = jnp.where(d < ROPE_DIM, cos, 1.0)
    s1_ref[...] = jnp.where(d < ROPE_HALF, -sin, 0.0)
    s2_ref[...] = jnp.where(d < ROPE_HALF, 0.0, jnp.where(d < ROPE_DIM, sin, 0.0))


def _rope_tables(pos, invf_lane, tm):
    t = pos.shape[0]
    spec = pl.BlockSpec((tm, LANES), lambda i: (i, 0))
    shp = jax.ShapeDtypeStruct((t, LANES), F32)
    return pl.pallas_call(
        _rope_kernel,
        out_shape=(shp, shp, shp),
        grid=(t // tm,),
        in_specs=[pl.BlockSpec((tm, 1), lambda i: (i, 0)), pl.BlockSpec((1, LANES), lambda i: (0, 0))],
        out_specs=(spec, spec, spec),
        compiler_params=_params("parallel"),
        name="rope_tables",
    )(pos, invf_lane)


def _rope(y, c, s1, s2):
    return y * c + pltpu.roll(y, LANES - ROPE_HALF, 1) * s1 + pltpu.roll(y, ROPE_HALF, 1) * s2


def _head_rms(xb, avg):
    hi, lo = _split(xb * xb)
    ms = _dot(hi, avg) + _dot(lo, avg)
    return xb * lax.rsqrt(ms + EPS)


def _inproj_kernel(x_ref, xh_ref, sc_ref, sh_ref, g1_ref, wmix_ref, wq_ref, wk_ref, wv_ref, wqi_ref, wl_ref,
                   cw_ref, qg_ref, kg_ref, kig_ref, ct_ref, s1_ref, s2_ref, avg_ref,
                   conv_o, q_o, k_o, v_o, qi_o, kilo_o, kihi_o, wi_o, *, seq):
    tm = x_ref.shape[0]
    scale = sc_ref[0]
    shift = sh_ref[0]
    g1 = g1_ref[...]
    h = _rms_mod(x_ref[...], g1, scale, shift).astype(BF16)
    hh = _rms_mod(xh_ref[...], g1, scale, shift).astype(BF16)

    mix = _dot(h, wmix_ref[...])
    mixh = _dot(hh, wmix_ref[...])
    u = mix[:, 2 * CONV_DIM:] * mix[:, :CONV_DIM]
    uh = mixh[:, 2 * CONV_DIM:] * mixh[:, :CONV_DIM]
    seq_start = (pl.program_id(0) * tm) % seq == 0
    uh = jnp.where(seq_start, 0.0, uh)
    ext = jnp.concatenate([uh, u], axis=0)
    u1 = pltpu.roll(ext, 1, 0)[SUBLANES:]
    u2 = pltpu.roll(ext, 2, 0)[SUBLANES:]
    conv = u2 * cw_ref[0:1, :] + u1 * cw_ref[1:2, :] + u * cw_ref[2:3, :]
    conv_o[...] = (mix[:, CONV_DIM:2 * CONV_DIM] * conv).astype(BF16)

    ct = ct_ref[...]
    s1 = s1_ref[...]
    s2 = s2_ref[...]
    avg = avg_ref[...]
    qf = _dot(h, wq_ref[...])
    kf = _dot(h, wk_ref[...])
    qif = _dot(h, wqi_ref[...])
    for p in range(ATTN_DIM // LANES):
        sl = slice(p * LANES, (p + 1) * LANES)
        qn = _rope(_head_rms(qf[:, sl], avg) * qg_ref[...], ct, s1, s2)
        q_o[:, sl] = (qn * (HEAD_DIM ** -0.5)).astype(BF16)
        kn = _rope(_head_rms(kf[:, sl], avg) * kg_ref[...], ct, s1, s2)
        k_o[:, sl] = kn.astype(BF16)
        qi_o[:, sl] = _rope(qif[:, sl], ct, s1, s2).astype(BF16)
    v_o[...] = _dot(h, wv_ref[...]).astype(BF16)

    last = _dot(h, wl_ref[...])
    lane = lax.broadcasted_iota(jnp.int32, last.shape, 1)
    is_key = lane < IDX_DIM
    kin = _head_rms(last, avg) * kig_ref[...]
    kin = _rope(kin, jnp.where(is_key, ct, 1.0), jnp.where(is_key, s1, 0.0), jnp.where(is_key, s2, 0.0))
    klo = jnp.where(is_key, kin, 0.0)
    kilo_o[...] = klo.astype(BF16)
    kihi_o[...] = pltpu.roll(klo, IDX_DIM, 1).astype(BF16)
    wi_o[...] = pltpu.roll(last, LANES - IDX_DIM, 1)[:, :IDX_HEADS]


def _inproj(x2, scale1, shift1, norm1_g, w_in, conv_w, q_norm_g, k_norm_g, kidx_norm_g, ct, s1, s2, seq, tm):
    t, d = x2.shape
    cuts = [0, 3 * CONV_DIM, 3 * CONV_DIM + ATTN_DIM, 3 * CONV_DIM + 2 * ATTN_DIM, 3 * CONV_DIM + 3 * ATTN_DIM,
            3 * CONV_DIM + 3 * ATTN_DIM + IDX_HEADS * IDX_DIM]
    wb = w_in.astype(BF16)
    wmix, wq, wk, wv, wqi = [wb[:, a:b] for a, b in zip(cuts[:-1], cuts[1:])]
    wl = wb[:, cuts[-1]:]
    wl = jnp.pad(wl, ((0, 0), (0, LANES - wl.shape[1])))
    ones = jnp.ones((1, LANES - IDX_DIM), F32)
    qg = jnp.tile(q_norm_g.reshape(1, HEAD_DIM), (1, 2))
    kg = jnp.tile(k_norm_g.reshape(1, HEAD_DIM), (1, 2))
    kig = jnp.concatenate([kidx_norm_g.reshape(1, IDX_DIM), ones], axis=1)
    blk = jnp.arange(LANES) // HEAD_DIM
    avg = jnp.where(blk[:, None] == blk[None, :], 1.0 / HEAD_DIM, 0.0).astype(BF16)

    bsz = t // seq
    per_b = seq // tm
    row = lambda w: pl.BlockSpec((tm, w), lambda i: (i, 0))
    full = lambda a: pl.BlockSpec(a.shape, lambda i: (0,) * a.ndim)
    mod = pl.BlockSpec((1, 1, d), lambda i: (i // per_b, 0, 0))
    halo = pl.BlockSpec((SUBLANES, d), lambda i: (jnp.maximum(i * (tm // SUBLANES) - 1, 0), 0))
    g1 = norm1_g.reshape(1, d)
    out_shape = (
        jax.ShapeDtypeStruct((t, CONV_DIM), BF16),
        jax.ShapeDtypeStruct((t, ATTN_DIM), BF16),
        jax.ShapeDtypeStruct((t, ATTN_DIM), BF16),
        jax.ShapeDtypeStruct((t, ATTN_DIM), BF16),
        jax.ShapeDtypeStruct((t, IDX_HEADS * IDX_DIM), BF16),
        jax.ShapeDtypeStruct((t, LANES), BF16),
        jax.ShapeDtypeStruct((t, LANES), BF16),
        jax.ShapeDtypeStruct((t, IDX_HEADS), F32),
    )
    del bsz
    return pl.pallas_call(
        functools.partial(_inproj_kernel, seq=seq),
        out_shape=out_shape,
        grid=(t // tm,),
        in_specs=[row(d), halo, mod, mod, full(g1), full(wmix), full(wq), full(wk), full(wv), full(wqi), full(wl),
                  full(conv_w), full(qg), full(kg), full(kig), row(LANES), row(LANES), row(LANES), full(avg)],
        out_specs=(row(CONV_DIM), row(ATTN_DIM), row(ATTN_DIM), row(ATTN_DIM), row(IDX_HEADS * IDX_DIM),
                   row(LANES), row(LANES), row(IDX_HEADS)),
        compiler_params=_params("parallel"),
        name="inproj",
    )(x2, x2, scale1, shift1, g1, wmix, wq, wk, wv, wqi, wl, conv_w, qg, kg, kig, ct, s1, s2, avg)


def _ukey_to_f32(u):
    s = u ^ jnp.int32(-2 ** 31)
    bits = s ^ ((s >> 31) & jnp.int32(0x7FFFFFFF))
    return lax.bitcast_convert_type(bits, F32)


def _attn_kernel(q_ref, k_ref, v_ref, qi_ref, kilo_ref, kihi_ref, wi_ref, o_ref,
                 sc_ref, wb_ref, qm_ref, cnt_ref, p_ref, acc_ref, m_ref, l_ref, *, n_sel):
    tq = q_ref.shape[0]
    seq = k_ref.shape[0]
    kc = sc_ref.shape[2]
    lb_n = kc // LANES
    j = pl.program_id(1)
    nch = (j + 1) * (tq // kc)
    lane = lax.broadcasted_iota(jnp.int32, (tq, LANES), 1)

    for h in range(IDX_HEADS):
        wb_ref[h] = jnp.broadcast_to(wi_ref[:, h:h + 1], (tq, LANES))
    for h in range(ATTN_HEADS):
        qp = q_ref[:, (h // 2) * LANES:(h // 2 + 1) * LANES].astype(F32)
        keep = (lane < HEAD_DIM) if h % 2 == 0 else (lane >= HEAD_DIM)
        qm_ref[h] = jnp.where(keep, qp, 0.0).astype(BF16)

    t_pos = j * tq + lax.broadcasted_iota(jnp.int32, (tq, kc), 0)
    k_off = lax.broadcasted_iota(jnp.int32, (tq, kc), 1)

    def index_chunk(c, carry):
        r0 = pl.multiple_of(c * kc, kc)
        klo = kilo_ref[pl.ds(r0, kc), :]
        khi = kihi_ref[pl.ds(r0, kc), :]
        acc = jnp.zeros((tq, kc), F32)
        for p in range(IDX_HEADS // 2):
            qip = qi_ref[:, p * LANES:(p + 1) * LANES]
            w_even = jnp.concatenate([wb_ref[2 * p]] * lb_n, axis=1)
            w_odd = jnp.concatenate([wb_ref[2 * p + 1]] * lb_n, axis=1)
            acc = acc + jnp.maximum(_dot_t(qip, klo), 0.0) * w_even
            acc = acc + jnp.maximum(_dot_t(qip, khi), 0.0) * w_odd
        score = acc * IDX_SCALE
        score = jnp.where(r0 + k_off <= t_pos, score, -jnp.inf)
        sc_ref[c] = jnp.where(score == 0.0, 0.0, score)
        return carry

    lax.fori_loop(0, nch, index_chunk, 0)

    t_row = j * tq + lax.broadcasted_iota(jnp.int32, (tq, LANES), 0)
    k_row = jnp.minimum(t_row + 1, n_sel).astype(F32)
    kidx0 = lax.broadcasted_iota(jnp.int32, (tq, LANES), 1)
    ones = jnp.ones((LANES, LANES), BF16)
    idx_bits = (seq - 1).bit_length()

    def count(pred):
        cnt_ref[...] = jnp.zeros((tq, LANES), F32)

        def chunk(c, carry):
            blk = sc_ref[c]
            add = jnp.zeros((tq, LANES), F32)
            for b in range(lb_n):
                kidx = kidx0 + (c * kc + b * LANES)
                add = add + jnp.where(pred(blk[:, b * LANES:(b + 1) * LANES], kidx), 1.0, 0.0)
            cnt_ref[...] += add
            return carry

        lax.fori_loop(0, nch, chunk, 0)
        return _dot(cnt_ref[...].astype(BF16), ones)

    p_ref[...] = jnp.zeros((tq, LANES), jnp.int32)

    def value_bit(i, carry):
        cand = p_ref[...] | jnp.left_shift(jnp.int32(1), 31 - i)
        cand_f = _ukey_to_f32(cand)
        cnt = count(lambda v, kidx: v >= cand_f)
        p_ref[...] = jnp.where(cnt >= k_row, cand, p_ref[...])
        return carry

    lax.fori_loop(0, 32, value_bit, 0)
    thr = _ukey_to_f32(p_ref[...])
    n_ge = count(lambda v, kidx: v >= thr)
    tied = jnp.max(jnp.where(n_ge > k_row, 1.0, 0.0)) > 0.0
    p_ref[...] = jnp.full((tq, LANES), seq, jnp.int32)

    @pl.when(tied)
    def _():
        need = k_row - count(lambda v, kidx: v > thr)
        p_ref[...] = jnp.zeros((tq, LANES), jnp.int32)

        def index_bit(i, carry):
            cand = p_ref[...] | jnp.left_shift(jnp.int32(1), idx_bits - 1 - i)
            cnt = count(lambda v, kidx: jnp.where(v == thr, kidx, seq) < cand)
            p_ref[...] = jnp.where(cnt < need, cand, p_ref[...])
            return carry

        lax.fori_loop(0, idx_bits, index_bit, 0)

    cut = p_ref[...]

    def write_bias(c, carry):
        blk = sc_ref[c]
        parts = []
        for b in range(lb_n):
            v = blk[:, b * LANES:(b + 1) * LANES]
            kidx = kidx0 + (c * kc + b * LANES)
            tie_bias = jnp.where(kidx <= cut, 0.0, MASKED)
            parts.append(jnp.where(v > thr, 0.0, jnp.where(v == thr, tie_bias, MASKED)))
        sc_ref[c] = jnp.concatenate(parts, axis=1)
        return carry

    lax.fori_loop(0, nch, write_bias, 0)

    for h in range(ATTN_HEADS):
        m_ref[h] = jnp.full((tq, LANES), MASKED, F32)
        l_ref[h] = jnp.zeros((tq, LANES), F32)
        acc_ref[h] = jnp.zeros((tq, LANES), F32)

    def attend(c, carry):
        r0 = pl.multiple_of(c * kc, kc)
        keys = pl.ds(r0, kc)
        bias = sc_ref[c]

        def qk(h):
            return _dot_t(qm_ref[h], k_ref[keys, (h // 2) * LANES:(h // 2 + 1) * LANES])

        s_next = qk(0)
        for h in range(ATTN_HEADS):
            s = s_next + bias
            if h + 1 < ATTN_HEADS:
                s_next = qk(h + 1)
            parts = [s[:, b * LANES:(b + 1) * LANES] for b in range(lb_n)]
            m_old = m_ref[h]
            row_max = jnp.max(functools.reduce(jnp.maximum, parts), axis=1, keepdims=True)
            m_new = jnp.maximum(m_old, row_max)
            alpha = jnp.exp(m_old - m_new)
            p_parts = [jnp.exp(part - m_new) for part in parts]
            l_ref[h] = alpha * l_ref[h] + functools.reduce(jnp.add, p_parts)
            p = jnp.concatenate(p_parts, axis=1).astype(BF16)
            acc_ref[h] = alpha * acc_ref[h] + _dot(p, v_ref[keys, (h // 2) * LANES:(h // 2 + 1) * LANES])
            m_ref[h] = m_new
        return carry

    lax.fori_loop(0, nch, attend, 0)
    for pair in range(ATTN_HEADS // 2):
        l_even = jnp.sum(l_ref[2 * pair], axis=1, keepdims=True)
        l_odd = jnp.sum(l_ref[2 * pair + 1], axis=1, keepdims=True)
        o_pair = jnp.where(lane < HEAD_DIM, acc_ref[2 * pair] / l_even, acc_ref[2 * pair + 1] / l_odd)
        o_ref[:, pair * LANES:(pair + 1) * LANES] = o_pair.astype(BF16)


def _attention(q, k, v, qi, kilo, kihi, wi, bsz, seq, tq, kc):
    n_sel = min(TOPK_KEYS_MAX, seq // 4)
    shape3 = lambda a: a.reshape(bsz, seq, a.shape[-1])
    q, k, v, qi, kilo, kihi, wi = map(shape3, (q, k, v, qi, kilo, kihi, wi))
    qblk = lambda w: pl.BlockSpec((None, tq, w), lambda b, j: (b, j, 0))
    kblk = lambda w: pl.BlockSpec((None, seq, w), lambda b, j: (b, 0, 0), pipeline_mode=pl.Buffered(1))
    out = pl.pallas_call(
        functools.partial(_attn_kernel, n_sel=n_sel),
        out_shape=jax.ShapeDtypeStruct((bsz, seq, ATTN_DIM), BF16),
        grid=(bsz, seq // tq),
        in_specs=[qblk(ATTN_DIM), kblk(ATTN_DIM), kblk(ATTN_DIM), qblk(IDX_HEADS * IDX_DIM), kblk(LANES), kblk(LANES),
                  qblk(IDX_HEADS)],
        out_specs=qblk(ATTN_DIM),
        scratch_shapes=[
            pltpu.VMEM((seq // kc, tq, kc), F32),
            pltpu.VMEM((IDX_HEADS, tq, LANES), F32),
            pltpu.VMEM((ATTN_HEADS, tq, LANES), BF16),
            pltpu.VMEM((tq, LANES), F32),
            pltpu.VMEM((tq, LANES), jnp.int32),
            pltpu.VMEM((ATTN_HEADS, tq, LANES), F32),
            pltpu.VMEM((ATTN_HEADS, tq, LANES), F32),
            pltpu.VMEM((ATTN_HEADS, tq, LANES), F32),
        ],
        compiler_params=_params("parallel", "arbitrary"),
        name="dsa_attention",
    )(q, k, v, qi, kilo, kihi, wi)
    return out.reshape(bsz * seq, ATTN_DIM)


def _outproj_kernel(conv_ref, attn_ref, x_ref, gate1_ref, sc_ref, sh_ref, g2_ref, wout_ref, wr_ref, rbias_ref,
                    x1_o, h2_o, gates_o):
    mix = _dot(conv_ref[...], wout_ref[:CONV_DIM, :]) + _dot(attn_ref[...], wout_ref[CONV_DIM:, :])
    x1 = x_ref[...] + gate1_ref[0] * mix
    x1_o[...] = x1
    h2 = _rms_mod(x1, g2_ref[...], sc_ref[0], sh_ref[0])
    h2_o[...] = h2.astype(BF16)

    scores = jax.nn.sigmoid(_dot3(h2, wr_ref[...]))
    work = scores + rbias_ref[...]
    lane = lax.broadcasted_iota(jnp.int32, work.shape, 1).astype(F32)
    chosen = jnp.zeros(work.shape, F32)
    for _ in range(TOP_K_EXPERTS):
        mx = jnp.max(work, axis=1, keepdims=True)
        first = jnp.min(jnp.where(work == mx, lane, float(N_EXPERTS)), axis=1, keepdims=True)
        onehot = lane == first
        chosen = jnp.where(onehot, 1.0, chosen)
        work = jnp.where(onehot, -jnp.inf, work)
    g = jnp.where(chosen > 0.0, scores, 0.0)
    gates_o[...] = g / jnp.sum(g, axis=1, keepdims=True) * ROUTED_SCALE


def _outproj(conv, attn, x2, gate1, scale2, shift2, norm2_g, w_out, w_router, router_bias, seq, tm):
    t, d = x2.shape
    e = w_router.shape[1]
    per_b = seq // tm
    row = lambda w: pl.BlockSpec((tm, w), lambda i: (i, 0))
    full = lambda a: pl.BlockSpec(a.shape, lambda i: (0,) * a.ndim)
    mod = pl.BlockSpec((1, 1, d), lambda i: (i // per_b, 0, 0))
    g2 = norm2_g.reshape(1, d)
    wo = w_out.astype(BF16)
    rbias = router_bias.reshape(1, e)
    return pl.pallas_call(
        _outproj_kernel,
        out_shape=(jax.ShapeDtypeStruct((t, d), F32), jax.ShapeDtypeStruct((t, d), BF16),
                   jax.ShapeDtypeStruct((t, e), F32)),
        grid=(t // tm,),
        in_specs=[row(CONV_DIM), row(ATTN_DIM), row(d), mod, mod, mod, full(g2), full(wo), full(w_router), full(rbias)],
        out_specs=(row(d), row(d), row(e)),
        compiler_params=_params("parallel"),
        name="outproj_router",
    )(conv, attn, x2, gate1, scale2, shift2, g2, wo, w_router, rbias)


def _moe_kernel(h2_ref, x1_ref, gates_ref, gate2_ref, w13_ref, w2_ref, ws13_ref, ws2_ref, o_ref, acc_ref):
    e = pl.program_id(1)
    f = w2_ref.shape[1]
    h2 = h2_ref[...]

    @pl.when(e == 0)
    def _():
        hs = _dot(h2, ws13_ref[...])
        act = _silu(hs[:, :f]) * hs[:, f:]
        acc_ref[...] = _dot(act.astype(BF16), ws2_ref[...])

    hh = _dot(h2, w13_ref[0])
    act = _silu(hh[:, :f]) * hh[:, f:]
    y = _dot(act.astype(BF16), w2_ref[0])
    gates = gates_ref[...]
    lane = lax.broadcasted_iota(jnp.int32, gates.shape, 1)
    g_col = jnp.sum(jnp.where(lane == e, gates, 0.0), axis=1, keepdims=True)
    acc_ref[...] += y * g_col

    @pl.when(e == pl.num_programs(1) - 1)
    def _():
        o_ref[...] = x1_ref[...] + gate2_ref[0] * acc_ref[...]


def _moe(h2, x1, gates, gate2, w1, w3, w2, ws1, ws3, ws2, seq, tm):
    t, d = x1.shape
    n_e, _, f = w1.shape
    w13 = jnp.concatenate([w1, w3], axis=2).astype(BF16)
    w2b = w2.astype(BF16)
    ws13 = jnp.concatenate([ws1, ws3], axis=1).astype(BF16)
    ws2b = ws2.astype(BF16)
    per_b = seq // tm
    row = lambda w: pl.BlockSpec((tm, w), lambda i, e: (i, 0))
    full = lambda a: pl.BlockSpec(a.shape, lambda i, e: (0,) * a.ndim)
    return pl.pallas_call(
        _moe_kernel,
        out_shape=jax.ShapeDtypeStruct((t, d), F32),
        grid=(t // tm, n_e),
        in_specs=[row(d), row(d), row(n_e), pl.BlockSpec((1, 1, d), lambda i, e: (i // per_b, 0, 0)),
                  pl.BlockSpec((1, d, 2 * f), lambda i, e: (e, 0, 0)), pl.BlockSpec((1, f, d), lambda i, e: (e, 0, 0)),
                  full(ws13), full(ws2b)],
        out_specs=row(d),
        scratch_shapes=[pltpu.VMEM((tm, d), F32)],
        compiler_params=_params("parallel", "arbitrary"),
        name="moe_dense",
    )(h2, x1, gates, gate2, w13, w2b, ws13, ws2b)


def _layer(x, c, positions, norm1_g, norm2_g, w_ada, b_ada, w_in, conv_w, q_norm_g, k_norm_g, kidx_norm_g, w_out,
           w_router, router_bias, w1, w3, w2, ws1, ws3, ws2):
    bsz, seq, d = x.shape
    t = bsz * seq
    tm = min(512, seq)
    tq = min(512, seq)
    x2 = x.reshape(t, d)

    ada = _ada(c, w_ada, b_ada)
    shift1, scale1, gate1, shift2, scale2, gate2 = [a.reshape(bsz, 1, d) for a in jnp.split(ada, 6, axis=-1)]

    inv_freq = ROPE_THETA ** (-jnp.arange(ROPE_HALF, dtype=F32) / ROPE_HALF)
    invf_lane = inv_freq[(jnp.arange(LANES) % HEAD_DIM) % ROPE_HALF].reshape(1, LANES)
    ct, s1, s2 = _rope_tables(positions.reshape(t, 1), invf_lane, min(2048, t))

    conv, q, k, v, qi, kilo, kihi, wi = _inproj(x2, scale1, shift1, norm1_g, w_in, conv_w, q_norm_g, k_norm_g,
                                                kidx_norm_g, ct, s1, s2, seq, tm)
    attn = _attention(q, k, v, qi, kilo, kihi, wi, bsz, seq, tq, kc=tq)
    x1, h2, gates = _outproj(conv, attn, x2, gate1, scale2, shift2, norm2_g, w_out, w_router, router_bias, seq, tm)
    out = _moe(h2, x1, gates, gate2, w1, w3, w2, ws1, ws3, ws2, seq, min(1024, seq))
    return out.reshape(bsz, seq, d)


def kernel(x, c, positions, norm1_g, norm2_g, w_ada, b_ada, w_in, conv_w, q_norm_g, k_norm_g, kidx_norm_g, w_out,
           w_router, router_bias, w1, w3, w2, ws1, ws3, ws2):
    for l in range(w_in.shape[0]):
        x = _layer(x, c, positions, norm1_g[l], norm2_g[l], w_ada[l], b_ada[l], w_in[l], conv_w[l], q_norm_g[l],
                   k_norm_g[l], kidx_norm_g[l], w_out[l], w_router[l], router_bias[l], w1[l], w3[l], w2[l], ws1[l],
                   ws3[l], ws2[l])
    return x
```

```python
import functools

import jax
import jax.numpy as jnp
from jax import lax
from jax.experimental import pallas as pl
from jax.experimental.pallas import tpu as pltpu
from jax.experimental.pallas import tpu_sc as plsc

F32 = jnp.float32
BF16 = jnp.bfloat16

HEAD_DIM = 64
ATTN_HEADS = 8
ATTN_DIM = ATTN_HEADS * HEAD_DIM
CONV_DIM = 512
IDX_HEADS = 8
IDX_DIM = 64
IDX_SCALE = (IDX_DIM ** -0.5) * (IDX_HEADS ** -0.5)
TOPK_KEYS_MAX = 256
ROPE_THETA = 500000.0
ROPE_DIM = HEAD_DIM // 4
ROPE_HALF = ROPE_DIM // 2
N_EXPERTS = 64
TOP_K_EXPERTS = 8
EXPERT_DIM = 256
ROUTED_SCALE = 2.5
EPS = 1e-6

LANES = 128
SUBLANES = 8
V7X_VMEM_BYTES = 64 * 1024 * 1024
VMEM_LIMIT = V7X_VMEM_BYTES * 3 // 4
V7X_SC_CORES = 2
V7X_SC_SUBCORES = 16
SC_WINDOW = 128
SC_ROW_WORDS = 256

MASKED = -1e30
MOE_BLOCK_ROWS = 512


def _params(*semantics):
    return pltpu.CompilerParams(dimension_semantics=semantics, vmem_limit_bytes=VMEM_LIMIT)


def _dot(a, b):
    return jnp.dot(a, b, preferred_element_type=F32)


def _dot_t(a, b):
    return lax.dot_general(a, b, (((1,), (1,)), ((), ())), preferred_element_type=F32)


def _split(a):
    hi = a.astype(BF16)
    lo = (a - hi.astype(F32)).astype(BF16)
    return hi, lo


def _dot3(a, b):
    a_hi, a_lo = _split(a)
    b_hi, b_lo = _split(b)
    return _dot(a_hi, b_hi) + _dot(a_hi, b_lo) + _dot(a_lo, b_hi)


def _silu(v):
    return v * jax.nn.sigmoid(v)


def _rms_mod(xv, g, scale, shift):
    ms = jnp.mean(xv * xv, axis=-1, keepdims=True)
    y = xv * lax.rsqrt(ms + EPS)
    return (y * g) * (1.0 + scale) + shift


def _ada_kernel(c_ref, w_ref, b_ref, o_ref):
    o_ref[...] = _dot3(_silu(c_ref[...]), w_ref[...]) + b_ref[...]


def _ada(c, w_ada, b_ada):
    bsz, d = c.shape
    n = w_ada.shape[1]
    bn = n // 4
    return pl.pallas_call(
        _ada_kernel,
        out_shape=jax.ShapeDtypeStruct((bsz, n), F32),
        grid=(n // bn,),
        in_specs=[
            pl.BlockSpec((bsz, d), lambda i: (0, 0)),
            pl.BlockSpec((d, bn), lambda i: (0, i)),
            pl.BlockSpec((1, bn), lambda i: (0, i)),
        ],
        out_specs=pl.BlockSpec((bsz, bn), lambda i: (0, i)),
        compiler_params=_params("parallel"),
        name="ada",
    )(c, w_ada, b_ada.reshape(1, n))


def _rope_kernel(pos_ref, invf_ref, c_ref, s1_ref, s2_ref):
    ang = pos_ref[...].astype(F32) * invf_ref[...]
    d = lax.broadcasted_iota(jnp.int32, ang.shape, 1) & (HEAD_DIM - 1)
    cos = jnp.cos(ang)
    sin = jnp.sin(ang)
    c_ref[...] = jnp.where(d < ROPE_DIM, cos, 1.0)
    s1_ref[...] = jnp.where(d < ROPE_HALF, -sin, 0.0)
    s2_ref[...] = jnp.where(d < ROPE_HALF, 0.0, jnp.where(d < ROPE_DIM, sin, 0.0))


def _rope_tables(pos, invf_lane, tm):
    t = pos.shape[0]
    spec = pl.BlockSpec((tm, LANES), lambda i: (i, 0))
    shp = jax.ShapeDtypeStruct((t, LANES), F32)
    return pl.pallas_call(
        _rope_kernel,
        out_shape=(shp, shp, shp),
        grid=(t // tm,),
        in_specs=[pl.BlockSpec((tm, 1), lambda i: (i, 0)), pl.BlockSpec((1, LANES), lambda i: (0, 0))],
        out_specs=(spec, spec, spec),
        compiler_params=_params("parallel"),
        name="rope_tables",
    )(pos, invf_lane)


def _rope(y, c, s1, s2):
    return y * c + pltpu.roll(y, LANES - ROPE_HALF, 1) * s1 + pltpu.roll(y, ROPE_HALF, 1) * s2


def _head_rms(xb, avg):
    hi, lo = _split(xb * xb)
    ms = _dot(hi, avg) + _dot(lo, avg)
    return xb * lax.rsqrt(ms + EPS)


def _inproj_kernel(x_ref, xh_ref, sc_ref, sh_ref, g1_ref, wmix_ref, wq_ref, wk_ref, wv_ref, wqi_ref, wl_ref,
                   cw_ref, qg_ref, kg_ref, kig_ref, ct_ref, s1_ref, s2_ref, avg_ref,
                   conv_o, q_o, k_o, v_o, qi_o, kilo_o, kihi_o, wi_o, *, seq):
    tm = x_ref.shape[0]
    scale = sc_ref[0]
    shift = sh_ref[0]
    g1 = g1_ref[...]
    h = _rms_mod(x_ref[...], g1, scale, shift).astype(BF16)
    hh = _rms_mod(xh_ref[...], g1, scale, shift).astype(BF16)

    mix = _dot(h, wmix_ref[...])
    mixh = _dot(hh, wmix_ref[...])
    u = mix[:, 2 * CONV_DIM:] * mix[:, :CONV_DIM]
    uh = mixh[:, 2 * CONV_DIM:] * mixh[:, :CONV_DIM]
    seq_start = (pl.program_id(0) * tm) % seq == 0
    uh = jnp.where(seq_start, 0.0, uh)
    ext = jnp.concatenate([uh, u], axis=0)
    u1 = pltpu.roll(ext, 1, 0)[SUBLANES:]
    u2 = pltpu.roll(ext, 2, 0)[SUBLANES:]
    conv = u2 * cw_ref[0:1, :] + u1 * cw_ref[1:2, :] + u * cw_ref[2:3, :]
    conv_o[...] = (mix[:, CONV_DIM:2 * CONV_DIM] * conv).astype(BF16)

    ct = ct_ref[...]
    s1 = s1_ref[...]
    s2 = s2_ref[...]
    avg = avg_ref[...]
    qf = _dot(h, wq_ref[...])
    kf = _dot(h, wk_ref[...])
    qif = _dot(h, wqi_ref[...])
    for p in range(ATTN_DIM // LANES):
        sl = slice(p * LANES, (p + 1) * LANES)
        qn = _rope(_head_rms(qf[:, sl], avg) * qg_ref[...], ct, s1, s2)
        q_o[:, sl] = (qn * (HEAD_DIM ** -0.5)).astype(BF16)
        kn = _rope(_head_rms(kf[:, sl], avg) * kg_ref[...], ct, s1, s2)
        k_o[:, sl] = kn.astype(BF16)
        qi_o[:, sl] = _rope(qif[:, sl], ct, s1, s2).astype(BF16)
    v_o[...] = _dot(h, wv_ref[...]).astype(BF16)

    last = _dot(h, wl_ref[...])
    lane = lax.broadcasted_iota(jnp.int32, last.shape, 1)
    is_key = lane < IDX_DIM
    kin = _head_rms(last, avg) * kig_ref[...]
    kin = _rope(kin, jnp.where(is_key, ct, 1.0), jnp.where(is_key, s1, 0.0), jnp.where(is_key, s2, 0.0))
    klo = jnp.where(is_key, kin, 0.0)
    kilo_o[...] = klo.astype(BF16)
    kihi_o[...] = pltpu.roll(klo, IDX_DIM, 1).astype(BF16)
    wi_o[...] = pltpu.roll(last, LANES - IDX_DIM, 1)[:, :IDX_HEADS]


def _inproj(x2, scale1, shift1, norm1_g, w_in, conv_w, q_norm_g, k_norm_g, kidx_norm_g, ct, s1, s2, seq, tm):
    t, d = x2.shape
    cuts = [0, 3 * CONV_DIM, 3 * CONV_DIM + ATTN_DIM, 3 * CONV_DIM + 2 * ATTN_DIM, 3 * CONV_DIM + 3 * ATTN_DIM,
            3 * CONV_DIM + 3 * ATTN_DIM + IDX_HEADS * IDX_DIM]
    wb = w_in.astype(BF16)
    wmix, wq, wk, wv, wqi = [wb[:, a:b] for a, b in zip(cuts[:-1], cuts[1:])]
    wl = wb[:, cuts[-1]:]
    wl = jnp.pad(wl, ((0, 0), (0, LANES - wl.shape[1])))
    ones = jnp.ones((1, LANES - IDX_DIM), F32)
    qg = jnp.tile(q_norm_g.reshape(1, HEAD_DIM), (1, 2))
    kg = jnp.tile(k_norm_g.reshape(1, HEAD_DIM), (1, 2))
    kig = jnp.concatenate([kidx_norm_g.reshape(1, IDX_DIM), ones], axis=1)
    blk = jnp.arange(LANES) // HEAD_DIM
    avg = jnp.where(blk[:, None] == blk[None, :], 1.0 / HEAD_DIM, 0.0).astype(BF16)

    bsz = t // seq
    per_b = seq // tm
    row = lambda w: pl.BlockSpec((tm, w), lambda i: (i, 0))
    full = lambda a: pl.BlockSpec(a.shape, lambda i: (0,) * a.ndim)
    mod = pl.BlockSpec((1, 1, d), lambda i: (i // per_b, 0, 0))
    halo = pl.BlockSpec((SUBLANES, d), lambda i: (jnp.maximum(i * (tm // SUBLANES) - 1, 0), 0))
    g1 = norm1_g.reshape(1, d)
    out_shape = (
        jax.ShapeDtypeStruct((t, CONV_DIM), BF16),
        jax.ShapeDtypeStruct((t, ATTN_DIM), BF16),
        jax.ShapeDtypeStruct((t, ATTN_DIM), BF16),
        jax.ShapeDtypeStruct((t, ATTN_DIM), BF16),
        jax.ShapeDtypeStruct((t, IDX_HEADS * IDX_DIM), BF16),
        jax.ShapeDtypeStruct((t, LANES), BF16),
        jax.ShapeDtypeStruct((t, LANES), BF16),
        jax.ShapeDtypeStruct((t, IDX_HEADS), F32),
    )
    del bsz
    return pl.pallas_call(
        functools.partial(_inproj_kernel, seq=seq),
        out_shape=out_shape,
        grid=(t // tm,),
        in_specs=[row(d), halo, mod, mod, full(g1), full(wmix), full(wq), full(wk), full(wv), full(wqi), full(wl),
                  full(conv_w), full(qg), full(kg), full(kig), row(LANES), row(LANES), row(LANES), full(avg)],
        out_specs=(row(CONV_DIM), row(ATTN_DIM), row(ATTN_DIM), row(ATTN_DIM), row(IDX_HEADS * IDX_DIM),
                   row(LANES), row(LANES), row(IDX_HEADS)),
        compiler_params=_params("parallel"),
        name="inproj",
    )(x2, x2, scale1, shift1, g1, wmix, wq, wk, wv, wqi, wl, conv_w, qg, kg, kig, ct, s1, s2, avg)


def _ukey_to_f32(u):
    s = u ^ jnp.int32(-2 ** 31)
    bits = s ^ ((s >> 31) & jnp.int32(0x7FFFFFFF))
    return lax.bitcast_convert_type(bits, F32)


def _attn_kernel(q_ref, k_ref, v_ref, qi_ref, kilo_ref, kihi_ref, wi_ref, o_ref,
                 sc_ref, wb_ref, qm_ref, cnt_ref, p_ref, acc_ref, m_ref, l_ref, *, n_sel):
    tq = q_ref.shape[0]
    seq = k_ref.shape[0]
    kc = sc_ref.shape[2]
    lb_n = kc // LANES
    j = pl.program_id(1)
    nch = (j + 1) * (tq // kc)
    lane = lax.broadcasted_iota(jnp.int32, (tq, LANES), 1)

    for h in range(IDX_HEADS):
        wb_ref[h] = jnp.broadcast_to(wi_ref[:, h:h + 1], (tq, LANES))
    for h in range(ATTN_HEADS):
        qp = q_ref[:, (h // 2) * LANES:(h // 2 + 1) * LANES].astype(F32)
        keep = (lane < HEAD_DIM) if h % 2 == 0 else (lane >= HEAD_DIM)
        qm_ref[h] = jnp.where(keep, qp, 0.0).astype(BF16)

    t_pos = j * tq + lax.broadcasted_iota(jnp.int32, (tq, kc), 0)
    k_off = lax.broadcasted_iota(jnp.int32, (tq, kc), 1)

    def index_chunk(c, carry):
        r0 = pl.multiple_of(c * kc, kc)
        klo = kilo_ref[pl.ds(r0, kc), :]
        khi = kihi_ref[pl.ds(r0, kc), :]
        acc = jnp.zeros((tq, kc), F32)
        for p in range(IDX_HEADS // 2):
            qip = qi_ref[:, p * LANES:(p + 1) * LANES]
            w_even = jnp.concatenate([wb_ref[2 * p]] * lb_n, axis=1)
            w_odd = jnp.concatenate([wb_ref[2 * p + 1]] * lb_n, axis=1)
            acc = acc + jnp.maximum(_dot_t(qip, klo), 0.0) * w_even
            acc = acc + jnp.maximum(_dot_t(qip, khi), 0.0) * w_odd
        score = acc * IDX_SCALE
        score = jnp.where(r0 + k_off <= t_pos, score, -jnp.inf)
        sc_ref[c] = jnp.where(score == 0.0, 0.0, score)
        return carry

    lax.fori_loop(0, nch, index_chunk, 0)

    t_row = j * tq + lax.broadcasted_iota(jnp.int32, (tq, LANES), 0)
    k_row = jnp.minimum(t_row + 1, n_sel).astype(F32)
    kidx0 = lax.broadcasted_iota(jnp.int32, (tq, LANES), 1)
    ones = jnp.ones((LANES, LANES), BF16)
    idx_bits = (seq - 1).bit_length()

    def count(pred):
        cnt_ref[...] = jnp.zeros((tq, LANES), F32)

        def chunk(c, carry):
            blk = sc_ref[c]
            add = jnp.zeros((tq, LANES), F32)
            for b in range(lb_n):
                kidx = kidx0 + (c * kc + b * LANES)
                add = add + jnp.where(pred(blk[:, b * LANES:(b + 1) * LANES], kidx), 1.0, 0.0)
            cnt_ref[...] += add
            return carry

        lax.fori_loop(0, nch, chunk, 0)
        return _dot(cnt_ref[...].astype(BF16), ones)

    p_ref[...] = jnp.zeros((tq, LANES), jnp.int32)

    def value_bit(i, carry):
        cand = p_ref[...] | jnp.left_shift(jnp.int32(1), 31 - i)
        cand_f = _ukey_to_f32(cand)
        cnt = count(lambda v, kidx: v >= cand_f)
        p_ref[...] = jnp.where(cnt >= k_row, cand, p_ref[...])
        return carry

    lax.fori_loop(0, 32, value_bit, 0)
    thr = _ukey_to_f32(p_ref[...])
    n_ge = count(lambda v, kidx: v >= thr)
    tied = jnp.max(jnp.where(n_ge > k_row, 1.0, 0.0)) > 0.0
    p_ref[...] = jnp.full((tq, LANES), seq, jnp.int32)

    @pl.when(tied)
    def _():
        need = k_row - count(lambda v, kidx: v > thr)
        p_ref[...] = jnp.zeros((tq, LANES), jnp.int32)

        def index_bit(i, carry):
            cand = p_ref[...] | jnp.left_shift(jnp.int32(1), idx_bits - 1 - i)
            cnt = count(lambda v, kidx: jnp.where(v == thr, kidx, seq) < cand)
            p_ref[...] = jnp.where(cnt < need, cand, p_ref[...])
            return carry

        lax.fori_loop(0, idx_bits, index_bit, 0)

    cut = p_ref[...]

    def write_bias(c, carry):
        blk = sc_ref[c]
        parts = []
        for b in range(lb_n):
            v = blk[:, b * LANES:(b + 1) * LANES]
            kidx = kidx0 + (c * kc + b * LANES)
            tie_bias = jnp.where(kidx <= cut, 0.0, MASKED)
            parts.append(jnp.where(v > thr, 0.0, jnp.where(v == thr, tie_bias, MASKED)))
        sc_ref[c] = jnp.concatenate(parts, axis=1)
        return carry

    lax.fori_loop(0, nch, write_bias, 0)

    for h in range(ATTN_HEADS):
        m_ref[h] = jnp.full((tq, LANES), MASKED, F32)
        l_ref[h] = jnp.zeros((tq, LANES), F32)
        acc_ref[h] = jnp.zeros((tq, LANES), F32)

    def attend(c, carry):
        r0 = pl.multiple_of(c * kc, kc)
        keys = pl.ds(r0, kc)
        bias = sc_ref[c]

        def qk(h):
            return _dot_t(qm_ref[h], k_ref[keys, (h // 2) * LANES:(h // 2 + 1) * LANES])

        s_next = qk(0)
        for h in range(ATTN_HEADS):
            s = s_next + bias
            if h + 1 < ATTN_HEADS:
                s_next = qk(h + 1)
            parts = [s[:, b * LANES:(b + 1) * LANES] for b in range(lb_n)]
            m_old = m_ref[h]
            row_max = jnp.max(functools.reduce(jnp.maximum, parts), axis=1, keepdims=True)
            m_new = jnp.maximum(m_old, row_max)
            alpha = jnp.exp(m_old - m_new)
            p_parts = [jnp.exp(part - m_new) for part in parts]
            l_ref[h] = alpha * l_ref[h] + functools.reduce(jnp.add, p_parts)
            p = jnp.concatenate(p_parts, axis=1).astype(BF16)
            acc_ref[h] = alpha * acc_ref[h] + _dot(p, v_ref[keys, (h // 2) * LANES:(h // 2 + 1) * LANES])
            m_ref[h] = m_new
        return carry

    lax.fori_loop(0, nch, attend, 0)
    for pair in range(ATTN_HEADS // 2):
        l_even = jnp.sum(l_ref[2 * pair], axis=1, keepdims=True)
        l_odd = jnp.sum(l_ref[2 * pair + 1], axis=1, keepdims=True)
        o_pair = jnp.where(lane < HEAD_DIM, acc_ref[2 * pair] / l_even, acc_ref[2 * pair + 1] / l_odd)
        o_ref[:, pair * LANES:(pair + 1) * LANES] = o_pair.astype(BF16)


def _attention(q, k, v, qi, kilo, kihi, wi, bsz, seq, tq, kc):
    n_sel = min(TOPK_KEYS_MAX, seq // 4)
    shape3 = lambda a: a.reshape(bsz, seq, a.shape[-1])
    q, k, v, qi, kilo, kihi, wi = map(shape3, (q, k, v, qi, kilo, kihi, wi))
    qblk = lambda w: pl.BlockSpec((None, tq, w), lambda b, j: (b, j, 0))
    kblk = lambda w: pl.BlockSpec((None, seq, w), lambda b, j: (b, 0, 0), pipeline_mode=pl.Buffered(1))
    out = pl.pallas_call(
        functools.partial(_attn_kernel, n_sel=n_sel),
        out_shape=jax.ShapeDtypeStruct((bsz, seq, ATTN_DIM), BF16),
        grid=(bsz, seq // tq),
        in_specs=[qblk(ATTN_DIM), kblk(ATTN_DIM), kblk(ATTN_DIM), qblk(IDX_HEADS * IDX_DIM), kblk(LANES), kblk(LANES),
                  qblk(IDX_HEADS)],
        out_specs=qblk(ATTN_DIM),
        scratch_shapes=[
            pltpu.VMEM((seq // kc, tq, kc), F32),
            pltpu.VMEM((IDX_HEADS, tq, LANES), F32),
            pltpu.VMEM((ATTN_HEADS, tq, LANES), BF16),
            pltpu.VMEM((tq, LANES), F32),
            pltpu.VMEM((tq, LANES), jnp.int32),
            pltpu.VMEM((ATTN_HEADS, tq, LANES), F32),
            pltpu.VMEM((ATTN_HEADS, tq, LANES), F32),
            pltpu.VMEM((ATTN_HEADS, tq, LANES), F32),
        ],
        compiler_params=_params("parallel", "arbitrary"),
        name="dsa_attention",
    )(q, k, v, qi, kilo, kihi, wi)
    return out.reshape(bsz * seq, ATTN_DIM)


def _outproj_kernel(conv_ref, attn_ref, x_ref, gate1_ref, sc_ref, sh_ref, g2_ref, wout_ref, wr_ref, rbias_ref,
                    x1_o, h2_o, h2pa_o, h2pb_o, eidx_o, g8_o):
    mix = _dot(conv_ref[...], wout_ref[:CONV_DIM, :]) + _dot(attn_ref[...], wout_ref[CONV_DIM:, :])
    x1 = x_ref[...] + gate1_ref[0] * mix
    x1_o[...] = x1
    h2 = _rms_mod(x1, g2_ref[...], sc_ref[0], sh_ref[0])
    h2b = h2.astype(BF16)
    h2_o[...] = h2b
    quarter = h2.shape[1] // 4
    bits = lax.bitcast_convert_type(h2b.astype(F32), jnp.int32)
    for half_o, c0 in ((h2pa_o, 0), (h2pb_o, 2 * quarter)):
        low = lax.shift_right_logical(bits[:, c0:c0 + quarter], 16)
        half_o[...] = low | (bits[:, c0 + quarter:c0 + 2 * quarter] & jnp.int32(-65536))

    scores = jax.nn.sigmoid(_dot3(h2, wr_ref[...]))
    work = scores + rbias_ref[...]
    lane = lax.broadcasted_iota(jnp.int32, work.shape, 1).astype(F32)
    slot = lax.broadcasted_iota(jnp.int32, (work.shape[0], TOP_K_EXPERTS), 1)
    eidx = jnp.zeros(slot.shape, F32)
    picked = jnp.zeros(slot.shape, F32)
    for r in range(TOP_K_EXPERTS):
        mx = jnp.max(work, axis=1, keepdims=True)
        first = jnp.min(jnp.where(work == mx, lane, float(N_EXPERTS)), axis=1, keepdims=True)
        onehot = lane == first
        score_r = jnp.sum(jnp.where(onehot, scores, 0.0), axis=1, keepdims=True)
        eidx = jnp.where(slot == r, first, eidx)
        picked = jnp.where(slot == r, score_r, picked)
        work = jnp.where(onehot, -jnp.inf, work)
    eidx_o[...] = eidx.astype(jnp.int32)
    g8_o[...] = picked / jnp.sum(picked, axis=1, keepdims=True) * ROUTED_SCALE


def _outproj(conv, attn, x2, gate1, scale2, shift2, norm2_g, w_out, w_router, router_bias, seq, tm):
    t, d = x2.shape
    e = w_router.shape[1]
    per_b = seq // tm
    row = lambda w: pl.BlockSpec((tm, w), lambda i: (i, 0))
    full = lambda a: pl.BlockSpec(a.shape, lambda i: (0,) * a.ndim)
    mod = pl.BlockSpec((1, 1, d), lambda i: (i // per_b, 0, 0))
    g2 = norm2_g.reshape(1, d)
    wo = w_out.astype(BF16)
    rbias = router_bias.reshape(1, e)
    return pl.pallas_call(
        _outproj_kernel,
        out_shape=(jax.ShapeDtypeStruct((t, d), F32), jax.ShapeDtypeStruct((t, d), BF16),
                   jax.ShapeDtypeStruct((t, d // 4), jnp.int32), jax.ShapeDtypeStruct((t, d // 4), jnp.int32),
                   jax.ShapeDtypeStruct((t, TOP_K_EXPERTS), jnp.int32), jax.ShapeDtypeStruct((t, TOP_K_EXPERTS), F32)),
        grid=(t // tm,),
        in_specs=[row(CONV_DIM), row(ATTN_DIM), row(d), mod, mod, mod, full(g2), full(wo), full(w_router), full(rbias)],
        out_specs=(row(d), row(d), row(d // 4), row(d // 4), row(TOP_K_EXPERTS), row(TOP_K_EXPERTS)),
        compiler_params=_params("parallel"),
        name="outproj_router",
    )(conv, attn, x2, gate1, scale2, shift2, g2, wo, w_router, rbias)


def _route_kernel(eidx_ref, dest_o, cnt_o, cnt_ref, run_ref, *, bm):
    phase = pl.program_id(0)
    i = pl.program_id(1)
    tm = eidx_ref.shape[0]
    e8 = eidx_ref[...]
    lane = lax.broadcasted_iota(jnp.int32, (tm, N_EXPERTS), 1)
    hits = [lane == e8[:, k:k + 1] for k in range(TOP_K_EXPERTS)]
    member = functools.reduce(jnp.add, [jnp.where(hit, 1.0, 0.0) for hit in hits])
    tile_cnt = jnp.sum(member, axis=0, keepdims=True)

    @pl.when(phase == 0)
    def _():
        @pl.when(i == 0)
        def _():
            cnt_ref[...] = jnp.zeros(cnt_ref.shape, F32)

        cnt_ref[...] += tile_cnt

    @pl.when(phase == 1)
    def _():
        @pl.when(i == 0)
        def _():
            blocks = jnp.ceil(cnt_ref[...] / bm)
            r = lax.broadcasted_iota(jnp.int32, (N_EXPERTS, N_EXPERTS), 0)
            c = lax.broadcasted_iota(jnp.int32, (N_EXPERTS, N_EXPERTS), 1)
            before = jnp.where(r < c, 1.0, 0.0).astype(BF16)
            b_hi, b_lo = _split(blocks)
            run_ref[...] = (_dot(b_hi, before) + _dot(b_lo, before)) * bm

        r = lax.broadcasted_iota(jnp.int32, (tm, tm), 0)
        c = lax.broadcasted_iota(jnp.int32, (tm, tm), 1)
        earlier = jnp.where(c < r, 1.0, 0.0).astype(BF16)
        base = run_ref[0:1, :] + _dot(earlier, member.astype(BF16))
        slot = lax.broadcasted_iota(jnp.int32, (tm, TOP_K_EXPERTS), 1)
        dest = jnp.zeros((tm, TOP_K_EXPERTS), F32)
        for k in range(TOP_K_EXPERTS):
            dest = jnp.where(slot == k, jnp.sum(jnp.where(hits[k], base, 0.0), axis=1, keepdims=True), dest)
        dest_o[...] = dest.astype(jnp.int32)
        run_ref[...] += tile_cnt

    cnt_o[...] = cnt_ref[...]


def _route(eidx, bm, tm):
    t = eidx.shape[0]
    return pl.pallas_call(
        functools.partial(_route_kernel, bm=bm),
        out_shape=(jax.ShapeDtypeStruct((t, TOP_K_EXPERTS), jnp.int32),
                   jax.ShapeDtypeStruct((SUBLANES, N_EXPERTS), F32)),
        grid=(2, t // tm),
        in_specs=[pl.BlockSpec((tm, TOP_K_EXPERTS), lambda p, i: (i, 0))],
        out_specs=(pl.BlockSpec((tm, TOP_K_EXPERTS), lambda p, i: (p * i, 0)),
                   pl.BlockSpec((SUBLANES, N_EXPERTS), lambda p, i: (0, 0))),
        scratch_shapes=[pltpu.VMEM((SUBLANES, N_EXPERTS), F32), pltpu.VMEM((SUBLANES, N_EXPERTS), F32)],
        compiler_params=_params("arbitrary", "arbitrary"),
        name="moe_route",
    )(eidx)


def _sc_mesh():
    return plsc.VectorSubcoreMesh(core_axis_name="core", subcore_axis_name="subcore", num_cores=V7X_SC_CORES,
                                  num_subcores=V7X_SC_SUBCORES)


def _sc_scatter_rows(rows, dest_kt, n_out):
    t, d = rows.shape
    n_k = dest_kt.shape[0]
    window = SC_WINDOW

    @functools.partial(pl.kernel, out_type=jax.ShapeDtypeStruct((n_out, d), rows.dtype), mesh=_sc_mesh(),
                       name="moe_dispatch_scatter")
    def scatter(x_hbm, i_hbm, o_hbm):
        def body(x_vmem, i_vmem):
            for k in range(n_k):
                pltpu.sync_copy(x_vmem, o_hbm.at[i_vmem.at[k]])

        pltpu.emit_pipeline(
            body,
            grid=(t // window,),
            in_specs=[pl.BlockSpec((window, d), lambda i: (i, 0)), pl.BlockSpec((n_k, window), lambda i: (0, i))],
            out_specs=[],
            core_axis_name=("core", "subcore"),
            dimension_semantics=(pltpu.PARALLEL,),
        )(x_hbm, i_hbm)

    return scatter(rows, dest_kt)


def _sc_gather_rows(table, idx):
    n = idx.shape[0]
    d = table.shape[1]
    window = SC_WINDOW

    @functools.partial(pl.kernel, out_type=jax.ShapeDtypeStruct((n, d), table.dtype), mesh=_sc_mesh(),
                       name="moe_combine_gather")
    def gather(tab_hbm, i_hbm, o_hbm):
        def body(i_vmem, o_vmem):
            pltpu.sync_copy(tab_hbm.at[i_vmem.at[0]], o_vmem)

        pltpu.emit_pipeline(
            body,
            grid=(n // window,),
            in_specs=[pl.BlockSpec((1, window), lambda i: (0, i))],
            out_specs=[pl.BlockSpec((window, d), lambda i: (i, 0))],
            core_axis_name=("core", "subcore"),
            dimension_semantics=(pltpu.PARALLEL,),
        )(i_hbm, o_hbm)

    return gather(table, idx.reshape(1, n))


def _ffn_kernel(be_ref, nused_ref, xa_ref, xb_ref, w13_ref, w2_ref, *ys_refs):
    del be_ref

    @pl.when(pl.program_id(0) < nused_ref[0])
    def _():
        f = w2_ref.shape[1]
        q = xa_ref.shape[1]
        hh = jnp.zeros((xa_ref.shape[0], 2 * f), F32)
        for x_ref, c0 in ((xa_ref, 0), (xb_ref, 2 * q)):
            xu = x_ref[...]
            lo = lax.bitcast_convert_type(lax.shift_left(xu, 16), F32).astype(BF16)
            hi = lax.bitcast_convert_type(xu & jnp.int32(-65536), F32).astype(BF16)
            hh = hh + _dot(lo, w13_ref[0, c0:c0 + q, :]) + _dot(hi, w13_ref[0, c0 + q:c0 + 2 * q, :])
        act = _silu(hh[:, :f]) * hh[:, f:]
        y = _dot(act.astype(BF16), w2_ref[0])
        for j, y_ref in enumerate(ys_refs):
            y_ref[...] = y[:, j * SC_ROW_WORDS:(j + 1) * SC_ROW_WORDS]


def _ffn(block_e, n_used, xs_a, xs_b, w13, w2b, bm):
    rows, q = xs_a.shape
    _, d, f2 = w13.shape
    n_out = d // SC_ROW_WORDS
    grid_spec = pltpu.PrefetchScalarGridSpec(
        num_scalar_prefetch=2,
        grid=(rows // bm,),
        in_specs=[pl.BlockSpec((bm, q), lambda b, be, nu: (b, 0)),
                  pl.BlockSpec((bm, q), lambda b, be, nu: (b, 0)),
                  pl.BlockSpec((1, d, f2), lambda b, be, nu: (be[b], 0, 0)),
                  pl.BlockSpec((1, f2 // 2, d), lambda b, be, nu: (be[b], 0, 0))],
        out_specs=tuple(pl.BlockSpec((bm, SC_ROW_WORDS), lambda b, be, nu: (b, 0)) for _ in range(n_out)),
    )
    return pl.pallas_call(
        _ffn_kernel,
        out_shape=tuple(jax.ShapeDtypeStruct((rows, SC_ROW_WORDS), F32) for _ in range(n_out)),
        grid_spec=grid_spec,
        compiler_params=_params("arbitrary"),
        name="moe_expert_ffn",
    )(block_e, n_used, xs_a, xs_b, w13, w2b)


def _combine_kernel(*refs):
    n_y = len(refs) - 7
    y_refs = refs[:n_y]
    g8_ref, h2_ref, x1_ref, gate2_ref, ws13_ref, ws2_ref, o_ref = refs[n_y:]
    f = ws2_ref.shape[0]
    hs = _dot(h2_ref[...], ws13_ref[...])
    acc = _dot((_silu(hs[:, :f]) * hs[:, f:]).astype(BF16), ws2_ref[...])
    g8 = g8_ref[...]
    for k in range(TOP_K_EXPERTS):
        acc = acc + g8[:, k:k + 1] * jnp.concatenate([y_ref[k] for y_ref in y_refs], axis=1)
    o_ref[...] = x1_ref[...] + gate2_ref[0] * acc


def _combine(y8s, g8, h2, x1, gate2, ws13, ws2b, seq, tm):
    t, d = x1.shape
    per_b = seq // tm
    row = lambda w: pl.BlockSpec((tm, w), lambda i: (i, 0))
    full = lambda a: pl.BlockSpec(a.shape, lambda i: (0,) * a.ndim)
    y_spec = pl.BlockSpec((TOP_K_EXPERTS, tm, SC_ROW_WORDS), lambda i: (0, i, 0))
    return pl.pallas_call(
        _combine_kernel,
        out_shape=jax.ShapeDtypeStruct((t, d), F32),
        grid=(t // tm,),
        in_specs=[y_spec] * len(y8s) + [row(TOP_K_EXPERTS), row(d), row(d),
                                         pl.BlockSpec((1, 1, d), lambda i: (i // per_b, 0, 0)), full(ws13), full(ws2b)],
        out_specs=row(d),
        compiler_params=_params("parallel"),
        name="moe_combine",
    )(*y8s, g8, h2, x1, gate2, ws13, ws2b)


def _moe(h2, h2pa, h2pb, x1, eidx, g8, gate2, w1, w3, w2, ws1, ws3, ws2, seq):
    t, d = x1.shape
    n_e = w1.shape[0]
    bm = MOE_BLOCK_ROWS
    rows = t * TOP_K_EXPERTS + n_e * bm
    w13 = jnp.concatenate([w1, w3], axis=2).astype(BF16)
    w2b = w2.astype(BF16)
    ws13 = jnp.concatenate([ws1, ws3], axis=1).astype(BF16)
    ws2b = ws2.astype(BF16)

    dest, cnt = _route(eidx, bm, min(512, t))
    pend = jnp.cumsum(jnp.ceil(cnt[0] / bm) * bm)
    block_e = jnp.clip(jnp.searchsorted(pend, jnp.arange(rows // bm, dtype=F32) * bm, side="right"), 0, n_e - 1)
    n_used = (pend[-1:] / bm).astype(jnp.int32)
    dest_kt = dest.T

    xs_a = _sc_scatter_rows(h2pa, dest_kt, rows)
    xs_b = _sc_scatter_rows(h2pb, dest_kt, rows)
    ys = _ffn(block_e.astype(jnp.int32), n_used, xs_a, xs_b, w13, w2b, bm)
    pair_rows = dest_kt.reshape(-1)
    y8s = [_sc_gather_rows(y, pair_rows).reshape(TOP_K_EXPERTS, t, SC_ROW_WORDS) for y in ys]
    return _combine(y8s, g8, h2, x1, gate2, ws13, ws2b, seq, min(256, seq))


def _layer(x, c, positions, norm1_g, norm2_g, w_ada, b_ada, w_in, conv_w, q_norm_g, k_norm_g, kidx_norm_g, w_out,
           w_router, router_bias, w1, w3, w2, ws1, ws3, ws2):
    bsz, seq, d = x.shape
    t = bsz * seq
    tm = min(512, seq)
    tq = min(512, seq)
    x2 = x.reshape(t, d)

    ada = _ada(c, w_ada, b_ada)
    shift1, scale1, gate1, shift2, scale2, gate2 = [a.reshape(bsz, 1, d) for a in jnp.split(ada, 6, axis=-1)]

    inv_freq = ROPE_THETA ** (-jnp.arange(ROPE_HALF, dtype=F32) / ROPE_HALF)
    invf_lane = inv_freq[(jnp.arange(LANES) % HEAD_DIM) % ROPE_HALF].reshape(1, LANES)
    ct, s1, s2 = _rope_tables(positions.reshape(t, 1), invf_lane, min(2048, t))

    conv, q, k, v, qi, kilo, kihi, wi = _inproj(x2, scale1, shift1, norm1_g, w_in, conv_w, q_norm_g, k_norm_g,
                                                kidx_norm_g, ct, s1, s2, seq, tm)
    attn = _attention(q, k, v, qi, kilo, kihi, wi, bsz, seq, tq, kc=tq)
    x1, h2, h2pa, h2pb, eidx, g8 = _outproj(conv, attn, x2, gate1, scale2, shift2, norm2_g, w_out, w_router,
                                            router_bias, seq, tm)
    out = _moe(h2, h2pa, h2pb, x1, eidx, g8, gate2, w1, w3, w2, ws1, ws3, ws2, seq)
    return out.reshape(bsz, seq, d)


def kernel(x, c, positions, norm1_g, norm2_g, w_ada, b_ada, w_in, conv_w, q_norm_g, k_norm_g, kidx_norm_g, w_out,
           w_router, router_bias, w1, w3, w2, ws1, ws3, ws2):
    for l in range(w_in.shape[0]):
        x = _layer(x, c, positions, norm1_g[l], norm2_g[l], w_ada[l], b_ada[l], w_in[l], conv_w[l], q_norm_g[l],
                   k_norm_g[l], kidx_norm_g[l], w_out[l], w_router[l], router_bias[l], w1[l], w3[l], w2[l], ws1[l],
                   ws3[l], ws2[l])
    return x
```

```python
import functools

import jax
import jax.numpy as jnp
from jax import lax
from jax.experimental import pallas as pl
from jax.experimental.pallas import tpu as pltpu
from jax.experimental.pallas import tpu_sc as plsc

F32 = jnp.float32
BF16 = jnp.bfloat16

HEAD_DIM = 64
ATTN_HEADS = 8
ATTN_DIM = ATTN_HEADS * HEAD_DIM
CONV_DIM = 512
IDX_HEADS = 8
IDX_DIM = 64
IDX_SCALE = (IDX_DIM ** -0.5) * (IDX_HEADS ** -0.5)
TOPK_KEYS_MAX = 256
ROPE_THETA = 500000.0
ROPE_DIM = HEAD_DIM // 4
ROPE_HALF = ROPE_DIM // 2
N_EXPERTS = 64
TOP_K_EXPERTS = 8
EXPERT_DIM = 256
ROUTED_SCALE = 2.5
EPS = 1e-6

LANES = 128
SUBLANES = 8
V7X_VMEM_BYTES = 64 * 1024 * 1024
VMEM_LIMIT = V7X_VMEM_BYTES * 3 // 4
V7X_SC_CORES = 2
V7X_SC_SUBCORES = 16
SC_WINDOW = 128
SC_ROW_WORDS = 256

MASKED = -1e30
Q_SCALE = HEAD_DIM ** -0.5 * 1.4426950408889634
MOE_BLOCK_ROWS = 512


def _params(*semantics):
    return pltpu.CompilerParams(dimension_semantics=semantics, vmem_limit_bytes=VMEM_LIMIT)


def _dot(a, b):
    return jnp.dot(a, b, preferred_element_type=F32)


def _dot_t(a, b):
    return lax.dot_general(a, b, (((1,), (1,)), ((), ())), preferred_element_type=F32)


def _split(a):
    hi = a.astype(BF16)
    lo = (a - hi.astype(F32)).astype(BF16)
    return hi, lo


def _dot3(a, b):
    a_hi, a_lo = _split(a)
    b_hi, b_lo = _split(b)
    return _dot(a_hi, b_hi) + _dot(a_hi, b_lo) + _dot(a_lo, b_hi)


def _silu(v):
    return v * jax.nn.sigmoid(v)


def _rms_mod(xv, g, scale, shift):
    ms = jnp.mean(xv * xv, axis=-1, keepdims=True)
    y = xv * lax.rsqrt(ms + EPS)
    return (y * g) * (1.0 + scale) + shift


def _ada_kernel(c_ref, w_ref, b_ref, o_ref):
    o_ref[...] = _dot3(_silu(c_ref[...]), w_ref[...]) + b_ref[...]


def _ada(c, w_ada, b_ada):
    bsz, d = c.shape
    n = w_ada.shape[1]
    bn = n // 4
    return pl.pallas_call(
        _ada_kernel,
        out_shape=jax.ShapeDtypeStruct((bsz, n), F32),
        grid=(n // bn,),
        in_specs=[
            pl.BlockSpec((bsz, d), lambda i: (0, 0)),
            pl.BlockSpec((d, bn), lambda i: (0, i)),
            pl.BlockSpec((1, bn), lambda i: (0, i)),
        ],
        out_specs=pl.BlockSpec((bsz, bn), lambda i: (0, i)),
        compiler_params=_params("parallel"),
        name="ada",
    )(c, w_ada, b_ada.reshape(1, n))


def _rope_kernel(pos_ref, invf_ref, c_ref, s1_ref, s2_ref):
    ang = pos_ref[...].astype(F32) * invf_ref[...]
    d = lax.broadcasted_iota(jnp.int32, ang.shape, 1) & (HEAD_DIM - 1)
    cos = jnp.cos(ang)
    sin = jnp.sin(ang)
    c_ref[...] = jnp.where(d < ROPE_DIM, cos, 1.0)
    s1_ref[...] = jnp.where(d < ROPE_HALF, -sin, 0.0)
    s2_ref[...] = jnp.where(d < ROPE_HALF, 0.0, jnp.where(d < ROPE_DIM, sin, 0.0))


def _rope_tables(pos, invf_lane, tm):
    t = pos.shape[0]
    spec = pl.BlockSpec((tm, LANES), lambda i: (i, 0))
    shp = jax.ShapeDtypeStruct((t, LANES), F32)
    return pl.pallas_call(
        _rope_kernel,
        out_shape=(shp, shp, shp),
        grid=(t // tm,),
        in_specs=[pl.BlockSpec((tm, 1), lambda i: (i, 0)), pl.BlockSpec((1, LANES), lambda i: (0, 0))],
        out_specs=(spec, spec, spec),
        compiler_params=_params("parallel"),
        name="rope_tables",
    )(pos, invf_lane)


def _rope(y, c, s1, s2):
    return y * c + pltpu.roll(y, LANES - ROPE_HALF, 1) * s1 + pltpu.roll(y, ROPE_HALF, 1) * s2


def _head_rms(xb, avg):
    hi, lo = _split(xb * xb)
    ms = _dot(hi, avg) + _dot(lo, avg)
    return xb * lax.rsqrt(ms + EPS)


def _inproj_kernel(x_ref, xh_ref, sc_ref, sh_ref, g1_ref, wmix_ref, wq_ref, wk_ref, wv_ref, wqi_ref, wl_ref, wlt_ref,
                   cw_ref, qg_ref, kg_ref, kig_ref, ct_ref, s1_ref, s2_ref, avg_ref,
                   conv_o, q_o, k_o, v_o, qi_o, kilo_o, kihi_o, wit_o, *, seq):
    tm = x_ref.shape[0]
    scale = sc_ref[0]
    shift = sh_ref[0]
    g1 = g1_ref[...]
    h = _rms_mod(x_ref[...], g1, scale, shift).astype(BF16)
    hh = _rms_mod(xh_ref[...], g1, scale, shift).astype(BF16)

    mix = _dot(h, wmix_ref[...])
    mixh = _dot(hh, wmix_ref[...])
    u = mix[:, 2 * CONV_DIM:] * mix[:, :CONV_DIM]
    uh = mixh[:, 2 * CONV_DIM:] * mixh[:, :CONV_DIM]
    seq_start = (pl.program_id(0) * tm) % seq == 0
    uh = jnp.where(seq_start, 0.0, uh)
    ext = jnp.concatenate([uh, u], axis=0)
    u1 = pltpu.roll(ext, 1, 0)[SUBLANES:]
    u2 = pltpu.roll(ext, 2, 0)[SUBLANES:]
    conv = u2 * cw_ref[0:1, :] + u1 * cw_ref[1:2, :] + u * cw_ref[2:3, :]
    conv_o[...] = (mix[:, CONV_DIM:2 * CONV_DIM] * conv).astype(BF16)

    ct = ct_ref[...]
    s1 = s1_ref[...]
    s2 = s2_ref[...]
    avg = avg_ref[...]
    qf = _dot(h, wq_ref[...])
    kf = _dot(h, wk_ref[...])
    qif = _dot(h, wqi_ref[...])
    for p in range(ATTN_DIM // LANES):
        sl = slice(p * LANES, (p + 1) * LANES)
        qn = _rope(_head_rms(qf[:, sl], avg) * qg_ref[...], ct, s1, s2)
        q_o[:, sl] = (qn * Q_SCALE).astype(BF16)
        kn = _rope(_head_rms(kf[:, sl], avg) * kg_ref[...], ct, s1, s2)
        k_o[:, sl] = kn.astype(BF16)
        qi_o[:, sl] = _rope(qif[:, sl], ct, s1, s2).astype(BF16)
    v_o[...] = _dot(h, wv_ref[...]).astype(BF16)

    wit_o[...] = _dot_t(wlt_ref[...], h)[IDX_DIM:IDX_DIM + IDX_HEADS, :]
    last = _dot(h, wl_ref[...])
    lane = lax.broadcasted_iota(jnp.int32, last.shape, 1)
    is_key = lane < IDX_DIM
    kin = _head_rms(last, avg) * kig_ref[...]
    kin = _rope(kin, jnp.where(is_key, ct, 1.0), jnp.where(is_key, s1, 0.0), jnp.where(is_key, s2, 0.0))
    klo = jnp.where(is_key, kin, 0.0)
    kilo_o[...] = klo.astype(BF16)
    kihi_o[...] = pltpu.roll(klo, IDX_DIM, 1).astype(BF16)


def _inproj(x2, scale1, shift1, norm1_g, w_in, conv_w, q_norm_g, k_norm_g, kidx_norm_g, ct, s1, s2, seq, tm):
    t, d = x2.shape
    cuts = [0, 3 * CONV_DIM, 3 * CONV_DIM + ATTN_DIM, 3 * CONV_DIM + 2 * ATTN_DIM, 3 * CONV_DIM + 3 * ATTN_DIM,
            3 * CONV_DIM + 3 * ATTN_DIM + IDX_HEADS * IDX_DIM]
    wb = w_in.astype(BF16)
    wmix, wq, wk, wv, wqi = [wb[:, a:b] for a, b in zip(cuts[:-1], cuts[1:])]
    wl = wb[:, cuts[-1]:]
    wl = jnp.pad(wl, ((0, 0), (0, LANES - wl.shape[1])))
    wlt = wl.T
    ones = jnp.ones((1, LANES - IDX_DIM), F32)
    qg = jnp.tile(q_norm_g.reshape(1, HEAD_DIM), (1, 2))
    kg = jnp.tile(k_norm_g.reshape(1, HEAD_DIM), (1, 2))
    kig = jnp.concatenate([kidx_norm_g.reshape(1, IDX_DIM), ones], axis=1)
    blk = jnp.arange(LANES) // HEAD_DIM
    avg = jnp.where(blk[:, None] == blk[None, :], 1.0 / HEAD_DIM, 0.0).astype(BF16)

    bsz = t // seq
    per_b = seq // tm
    row = lambda w: pl.BlockSpec((tm, w), lambda i: (i, 0))
    full = lambda a: pl.BlockSpec(a.shape, lambda i: (0,) * a.ndim)
    mod = pl.BlockSpec((1, 1, d), lambda i: (i // per_b, 0, 0))
    halo = pl.BlockSpec((SUBLANES, d), lambda i: (jnp.maximum(i * (tm // SUBLANES) - 1, 0), 0))
    g1 = norm1_g.reshape(1, d)
    out_shape = (
        jax.ShapeDtypeStruct((t, CONV_DIM), BF16),
        jax.ShapeDtypeStruct((t, ATTN_DIM), BF16),
        jax.ShapeDtypeStruct((t, ATTN_DIM), BF16),
        jax.ShapeDtypeStruct((t, ATTN_DIM), BF16),
        jax.ShapeDtypeStruct((t, IDX_HEADS * IDX_DIM), BF16),
        jax.ShapeDtypeStruct((t, LANES), BF16),
        jax.ShapeDtypeStruct((t, LANES), BF16),
        jax.ShapeDtypeStruct((IDX_HEADS, t), F32),
    )
    del bsz
    return pl.pallas_call(
        functools.partial(_inproj_kernel, seq=seq),
        out_shape=out_shape,
        grid=(t // tm,),
        in_specs=[row(d), halo, mod, mod, full(g1), full(wmix), full(wq), full(wk), full(wv), full(wqi), full(wl),
                  full(wlt), full(conv_w), full(qg), full(kg), full(kig), row(LANES), row(LANES), row(LANES), full(avg)],
        out_specs=(row(CONV_DIM), row(ATTN_DIM), row(ATTN_DIM), row(ATTN_DIM), row(IDX_HEADS * IDX_DIM),
                   row(LANES), row(LANES), pl.BlockSpec((IDX_HEADS, tm), lambda i: (0, i))),
        compiler_params=_params("parallel"),
        name="inproj",
    )(x2, x2, scale1, shift1, g1, wmix, wq, wk, wv, wqi, wl, wlt, conv_w, qg, kg, kig, ct, s1, s2, avg)


def _ukey_to_f32(u):
    s = u ^ jnp.int32(-2 ** 31)
    bits = s ^ ((s >> 31) & jnp.int32(0x7FFFFFFF))
    return lax.bitcast_convert_type(bits, F32)


def _attn_kernel(q_ref, k_ref, v_ref, qi_ref, kilo_ref, kihi_ref, wit_ref, o_ref,
                 sc_ref, qm_ref, acc_ref, m_ref, l_ref, *, n_sel):
    tq = q_ref.shape[0]
    seq = k_ref.shape[0]
    kc = sc_ref.shape[1]
    assert kc == tq
    lb_n = kc // LANES
    j = pl.program_id(1)
    nch = (j + 1) * (tq // kc)
    lane = lax.broadcasted_iota(jnp.int32, (tq, LANES), 1)

    for h in range(ATTN_HEADS):
        qp = q_ref[:, (h // 2) * LANES:(h // 2 + 1) * LANES].astype(F32)
        keep = (lane < HEAD_DIM) if h % 2 == 0 else (lane >= HEAD_DIM)
        qm_ref[h] = jnp.where(keep, qp, 0.0).astype(BF16)

    q_pos = j * tq + lax.broadcasted_iota(jnp.int32, (kc, tq), 1)
    k_off = lax.broadcasted_iota(jnp.int32, (kc, tq), 0)
    w_rows = wit_ref[...]

    def index_chunk(c, carry):
        r0 = pl.multiple_of(c * kc, kc)
        klo = kilo_ref[pl.ds(r0, kc), :]
        khi = kihi_ref[pl.ds(r0, kc), :]
        acc = jnp.zeros((kc, tq), F32)
        for p in range(IDX_HEADS // 2):
            qip = qi_ref[:, p * LANES:(p + 1) * LANES]
            acc = acc + jnp.maximum(_dot_t(klo, qip), 0.0) * w_rows[2 * p:2 * p + 1, :]
            acc = acc + jnp.maximum(_dot_t(khi, qip), 0.0) * w_rows[2 * p + 1:2 * p + 2, :]
        score = acc * IDX_SCALE
        score = jnp.where(r0 + k_off <= q_pos, score, -jnp.inf)
        sc_ref[c] = jnp.where(score == 0.0, 0.0, score)
        return carry

    lax.fori_loop(0, nch, index_chunk, 0)

    t_q = j * tq + lax.broadcasted_iota(jnp.int32, (1, tq), 1)
    k_row = jnp.minimum(t_q + 1, n_sel).astype(F32)
    sub = lax.broadcasted_iota(jnp.int32, (SUBLANES, tq), 0)
    idx_bits = (seq - 1).bit_length()

    def count(pred):
        def chunk(c, acc):
            blk = sc_ref[c]
            for g in range(kc // SUBLANES):
                kidx = sub + (c * kc + g * SUBLANES)
                acc = acc + jnp.where(pred(blk[g * SUBLANES:(g + 1) * SUBLANES, :], kidx), 1.0, 0.0)
            return acc

        acc = lax.fori_loop(0, nch, chunk, jnp.zeros((SUBLANES, tq), F32))
        return jnp.sum(acc, axis=0, keepdims=True)

    def rows8(v):
        return jnp.broadcast_to(v, (SUBLANES, tq))

    def value_bit(i, p):
        cand = p | jnp.left_shift(jnp.int32(1), 31 - i)
        cand_f = rows8(_ukey_to_f32(cand))
        cnt = count(lambda v, kidx: v >= cand_f)
        return jnp.where(cnt >= k_row, cand, p)

    thr = _ukey_to_f32(lax.fori_loop(0, 32, value_bit, jnp.zeros((1, tq), jnp.int32)))
    thr8 = rows8(thr)
    n_ge = count(lambda v, kidx: v >= thr8)
    tied = jnp.max(jnp.where(n_ge > k_row, 1.0, 0.0)) > 0.0

    def tie_cut():
        need = k_row - count(lambda v, kidx: v > thr8)

        def index_bit(i, p):
            cand = p | jnp.left_shift(jnp.int32(1), idx_bits - 1 - i)
            cand8 = rows8(cand)
            cnt = count(lambda v, kidx: jnp.where(v == thr8, kidx, seq) < cand8)
            return jnp.where(cnt < need, cand, p)

        return lax.fori_loop(0, idx_bits, index_bit, jnp.zeros((1, tq), jnp.int32))

    cut = lax.cond(tied, tie_cut, lambda: jnp.full((1, tq), seq, jnp.int32))

    def write_bias(c, carry):
        blk = sc_ref[c]
        tie_bias = jnp.where(c * kc + k_off <= cut, 0.0, MASKED)
        bias_t = jnp.where(blk > thr, 0.0, jnp.where(blk == thr, tie_bias, MASKED))
        sc_ref[c] = bias_t.T
        return carry

    lax.fori_loop(0, nch, write_bias, 0)

    for h in range(ATTN_HEADS):
        m_ref[h] = jnp.full((tq, LANES), MASKED, F32)
        l_ref[h] = jnp.zeros((tq, LANES), F32)
        acc_ref[h] = jnp.zeros((tq, LANES), F32)

    def attend(c, carry):
        r0 = pl.multiple_of(c * kc, kc)
        keys = pl.ds(r0, kc)
        bias = sc_ref[c]

        def qk(h):
            return _dot_t(qm_ref[h], k_ref[keys, (h // 2) * LANES:(h // 2 + 1) * LANES])

        s_next = qk(0)
        for h in range(ATTN_HEADS):
            s = s_next + bias
            if h + 1 < ATTN_HEADS:
                s_next = qk(h + 1)
            parts = [s[:, b * LANES:(b + 1) * LANES] for b in range(lb_n)]
            m_old = m_ref[h]
            row_max = jnp.max(functools.reduce(jnp.maximum, parts), axis=1, keepdims=True)
            m_new = jnp.maximum(m_old, row_max)
            alpha = jnp.exp2(m_old - m_new)
            p_parts = [jnp.exp2(part - m_new) for part in parts]
            l_ref[h] = alpha * l_ref[h] + functools.reduce(jnp.add, p_parts)
            p = jnp.concatenate(p_parts, axis=1).astype(BF16)
            acc_ref[h] = alpha * acc_ref[h] + _dot(p, v_ref[keys, (h // 2) * LANES:(h // 2 + 1) * LANES])
            m_ref[h] = m_new
        return carry

    lax.fori_loop(0, nch, attend, 0)
    for pair in range(ATTN_HEADS // 2):
        l_even = jnp.sum(l_ref[2 * pair], axis=1, keepdims=True)
        l_odd = jnp.sum(l_ref[2 * pair + 1], axis=1, keepdims=True)
        o_pair = jnp.where(lane < HEAD_DIM, acc_ref[2 * pair] / l_even, acc_ref[2 * pair + 1] / l_odd)
        o_ref[:, pair * LANES:(pair + 1) * LANES] = o_pair.astype(BF16)


def _attention(q, k, v, qi, kilo, kihi, wit, bsz, seq, tq, kc):
    n_sel = min(TOPK_KEYS_MAX, seq // 4)
    shape3 = lambda a: a.reshape(bsz, seq, a.shape[-1])
    q, k, v, qi, kilo, kihi = map(shape3, (q, k, v, qi, kilo, kihi))
    qblk = lambda w: pl.BlockSpec((None, tq, w), lambda b, j: (b, j, 0))
    kblk = lambda w: pl.BlockSpec((None, seq, w), lambda b, j: (b, 0, 0), pipeline_mode=pl.Buffered(1))
    out = pl.pallas_call(
        functools.partial(_attn_kernel, n_sel=n_sel),
        out_shape=jax.ShapeDtypeStruct((bsz, seq, ATTN_DIM), BF16),
        grid=(bsz, seq // tq),
        in_specs=[qblk(ATTN_DIM), kblk(ATTN_DIM), kblk(ATTN_DIM), qblk(IDX_HEADS * IDX_DIM), kblk(LANES), kblk(LANES),
                  pl.BlockSpec((IDX_HEADS, tq), lambda b, j: (0, b * (seq // tq) + j))],
        out_specs=qblk(ATTN_DIM),
        scratch_shapes=[
            pltpu.VMEM((seq // kc, kc, tq), F32),
            pltpu.VMEM((ATTN_HEADS, tq, LANES), BF16),
            pltpu.VMEM((ATTN_HEADS, tq, LANES), F32),
            pltpu.VMEM((ATTN_HEADS, tq, LANES), F32),
            pltpu.VMEM((ATTN_HEADS, tq, LANES), F32),
        ],
        compiler_params=_params("parallel", "arbitrary"),
        name="dsa_attention",
    )(q, k, v, qi, kilo, kihi, wit)
    return out.reshape(bsz * seq, ATTN_DIM)


def _outproj_kernel(conv_ref, attn_ref, x_ref, gate1_ref, sc_ref, sh_ref, g2_ref, wout_ref, wr_ref, rbias_ref,
                    x1_o, h2_o, h2pa_o, h2pb_o, eidx_o, g8_o):
    mix = _dot(conv_ref[...], wout_ref[:CONV_DIM, :]) + _dot(attn_ref[...], wout_ref[CONV_DIM:, :])
    x1 = x_ref[...] + gate1_ref[0] * mix
    x1_o[...] = x1
    h2 = _rms_mod(x1, g2_ref[...], sc_ref[0], sh_ref[0])
    h2b = h2.astype(BF16)
    h2_o[...] = h2b
    quarter = h2.shape[1] // 4
    bits = lax.bitcast_convert_type(h2b.astype(F32), jnp.int32)
    for half_o, c0 in ((h2pa_o, 0), (h2pb_o, 2 * quarter)):
        low = lax.shift_right_logical(bits[:, c0:c0 + quarter], 16)
        half_o[...] = low | (bits[:, c0 + quarter:c0 + 2 * quarter] & jnp.int32(-65536))

    scores = jax.nn.sigmoid(_dot3(h2, wr_ref[...]))
    work = scores + rbias_ref[...]
    lane = lax.broadcasted_iota(jnp.int32, work.shape, 1).astype(F32)
    slot = lax.broadcasted_iota(jnp.int32, (work.shape[0], TOP_K_EXPERTS), 1)
    eidx = jnp.zeros(slot.shape, F32)
    picked = jnp.zeros(slot.shape, F32)
    for r in range(TOP_K_EXPERTS):
        mx = jnp.max(work, axis=1, keepdims=True)
        first = jnp.min(jnp.where(work == mx, lane, float(N_EXPERTS)), axis=1, keepdims=True)
        onehot = lane == first
        score_r = jnp.sum(jnp.where(onehot, scores, 0.0), axis=1, keepdims=True)
        eidx = jnp.where(slot == r, first, eidx)
        picked = jnp.where(slot == r, score_r, picked)
        work = jnp.where(onehot, -jnp.inf, work)
    eidx_o[...] = eidx.astype(jnp.int32)
    g8_o[...] = picked / jnp.sum(picked, axis=1, keepdims=True) * ROUTED_SCALE


def _outproj(conv, attn, x2, gate1, scale2, shift2, norm2_g, w_out, w_router, router_bias, seq, tm):
    t, d = x2.shape
    e = w_router.shape[1]
    per_b = seq // tm
    row = lambda w: pl.BlockSpec((tm, w), lambda i: (i, 0))
    full = lambda a: pl.BlockSpec(a.shape, lambda i: (0,) * a.ndim)
    mod = pl.BlockSpec((1, 1, d), lambda i: (i // per_b, 0, 0))
    g2 = norm2_g.reshape(1, d)
    wo = w_out.astype(BF16)
    rbias = router_bias.reshape(1, e)
    return pl.pallas_call(
        _outproj_kernel,
        out_shape=(jax.ShapeDtypeStruct((t, d), F32), jax.ShapeDtypeStruct((t, d), BF16),
                   jax.ShapeDtypeStruct((t, d // 4), jnp.int32), jax.ShapeDtypeStruct((t, d // 4), jnp.int32),
                   jax.ShapeDtypeStruct((t, TOP_K_EXPERTS), jnp.int32), jax.ShapeDtypeStruct((t, TOP_K_EXPERTS), F32)),
        grid=(t // tm,),
        in_specs=[row(CONV_DIM), row(ATTN_DIM), row(d), mod, mod, mod, full(g2), full(wo), full(w_router), full(rbias)],
        out_specs=(row(d), row(d), row(d // 4), row(d // 4), row(TOP_K_EXPERTS), row(TOP_K_EXPERTS)),
        compiler_params=_params("parallel"),
        name="outproj_router",
    )(conv, attn, x2, gate1, scale2, shift2, g2, wo, w_router, rbias)


def _route_kernel(eidx_ref, dest_o, cnt_o, cnt_ref, run_ref, *, bm):
    phase = pl.program_id(0)
    i = pl.program_id(1)
    tm = eidx_ref.shape[0]
    e8 = eidx_ref[...]
    lane = lax.broadcasted_iota(jnp.int32, (tm, N_EXPERTS), 1)
    hits = [lane == e8[:, k:k + 1] for k in range(TOP_K_EXPERTS)]
    member = functools.reduce(jnp.add, [jnp.where(hit, 1.0, 0.0) for hit in hits])
    tile_cnt = jnp.sum(member, axis=0, keepdims=True)

    @pl.when(phase == 0)
    def _():
        @pl.when(i == 0)
        def _():
            cnt_ref[...] = jnp.zeros(cnt_ref.shape, F32)

        cnt_ref[...] += tile_cnt

    @pl.when(phase == 1)
    def _():
        @pl.when(i == 0)
        def _():
            blocks = jnp.ceil(cnt_ref[...] / bm)
            r = lax.broadcasted_iota(jnp.int32, (N_EXPERTS, N_EXPERTS), 0)
            c = lax.broadcasted_iota(jnp.int32, (N_EXPERTS, N_EXPERTS), 1)
            before = jnp.where(r < c, 1.0, 0.0).astype(BF16)
            b_hi, b_lo = _split(blocks)
            run_ref[...] = (_dot(b_hi, before) + _dot(b_lo, before)) * bm

        r = lax.broadcasted_iota(jnp.int32, (tm, tm), 0)
        c = lax.broadcasted_iota(jnp.int32, (tm, tm), 1)
        earlier = jnp.where(c < r, 1.0, 0.0).astype(BF16)
        base = run_ref[0:1, :] + _dot(earlier, member.astype(BF16))
        slot = lax.broadcasted_iota(jnp.int32, (tm, TOP_K_EXPERTS), 1)
        dest = jnp.zeros((tm, TOP_K_EXPERTS), F32)
        for k in range(TOP_K_EXPERTS):
            dest = jnp.where(slot == k, jnp.sum(jnp.where(hits[k], base, 0.0), axis=1, keepdims=True), dest)
        dest_o[...] = dest.astype(jnp.int32)
        run_ref[...] += tile_cnt

    cnt_o[...] = cnt_ref[...]


def _route(eidx, bm, tm):
    t = eidx.shape[0]
    return pl.pallas_call(
        functools.partial(_route_kernel, bm=bm),
        out_shape=(jax.ShapeDtypeStruct((t, TOP_K_EXPERTS), jnp.int32),
                   jax.ShapeDtypeStruct((SUBLANES, N_EXPERTS), F32)),
        grid=(2, t // tm),
        in_specs=[pl.BlockSpec((tm, TOP_K_EXPERTS), lambda p, i: (i, 0))],
        out_specs=(pl.BlockSpec((tm, TOP_K_EXPERTS), lambda p, i: (p * i, 0)),
                   pl.BlockSpec((SUBLANES, N_EXPERTS), lambda p, i: (0, 0))),
        scratch_shapes=[pltpu.VMEM((SUBLANES, N_EXPERTS), F32), pltpu.VMEM((SUBLANES, N_EXPERTS), F32)],
        compiler_params=_params("arbitrary", "arbitrary"),
        name="moe_route",
    )(eidx)


def _sc_mesh():
    return plsc.VectorSubcoreMesh(core_axis_name="core", subcore_axis_name="subcore", num_cores=V7X_SC_CORES,
                                  num_subcores=V7X_SC_SUBCORES)


def _sc_scatter_rows(rows, dest_kt, n_out):
    t, d = rows.shape
    n_k = dest_kt.shape[0]
    window = SC_WINDOW

    @functools.partial(pl.kernel, out_type=jax.ShapeDtypeStruct((n_out, d), rows.dtype), mesh=_sc_mesh(),
                       name="moe_dispatch_scatter")
    def scatter(x_hbm, i_hbm, o_hbm):
        def body(x_vmem, i_vmem):
            for k in range(n_k):
                pltpu.sync_copy(x_vmem, o_hbm.at[i_vmem.at[k]])

        pltpu.emit_pipeline(
            body,
            grid=(t // window,),
            in_specs=[pl.BlockSpec((window, d), lambda i: (i, 0)), pl.BlockSpec((n_k, window), lambda i: (0, i))],
            out_specs=[],
            core_axis_name=("core", "subcore"),
            dimension_semantics=(pltpu.PARALLEL,),
        )(x_hbm, i_hbm)

    return scatter(rows, dest_kt)


def _sc_gather_rows(table, idx):
    n = idx.shape[0]
    d = table.shape[1]
    window = SC_WINDOW

    @functools.partial(pl.kernel, out_type=jax.ShapeDtypeStruct((n, d), table.dtype), mesh=_sc_mesh(),
                       name="moe_combine_gather")
    def gather(tab_hbm, i_hbm, o_hbm):
        def body(i_vmem, o_vmem):
            pltpu.sync_copy(tab_hbm.at[i_vmem.at[0]], o_vmem)

        pltpu.emit_pipeline(
            body,
            grid=(n // window,),
            in_specs=[pl.BlockSpec((1, window), lambda i: (0, i))],
            out_specs=[pl.BlockSpec((window, d), lambda i: (i, 0))],
            core_axis_name=("core", "subcore"),
            dimension_semantics=(pltpu.PARALLEL,),
        )(i_hbm, o_hbm)

    return gather(table, idx.reshape(1, n))


def _ffn_kernel(be_ref, nused_ref, xa_ref, xb_ref, w13_ref, w2_ref, *ys_refs):
    del be_ref

    @pl.when(pl.program_id(0) < nused_ref[0])
    def _():
        f = w2_ref.shape[1]
        q = xa_ref.shape[1]
        hh = jnp.zeros((xa_ref.shape[0], 2 * f), F32)
        for x_ref, c0 in ((xa_ref, 0), (xb_ref, 2 * q)):
            xu = x_ref[...]
            lo = lax.bitcast_convert_type(lax.shift_left(xu, 16), F32).astype(BF16)
            hi = lax.bitcast_convert_type(xu & jnp.int32(-65536), F32).astype(BF16)
            hh = hh + _dot(lo, w13_ref[0, c0:c0 + q, :]) + _dot(hi, w13_ref[0, c0 + q:c0 + 2 * q, :])
        act = _silu(hh[:, :f]) * hh[:, f:]
        y = _dot(act.astype(BF16), w2_ref[0])
        for j, y_ref in enumerate(ys_refs):
            y_ref[...] = y[:, j * SC_ROW_WORDS:(j + 1) * SC_ROW_WORDS]


def _ffn(block_e, n_used, xs_a, xs_b, w13, w2b, bm):
    rows, q = xs_a.shape
    _, d, f2 = w13.shape
    n_out = d // SC_ROW_WORDS
    grid_spec = pltpu.PrefetchScalarGridSpec(
        num_scalar_prefetch=2,
        grid=(rows // bm,),
        in_specs=[pl.BlockSpec((bm, q), lambda b, be, nu: (b, 0)),
                  pl.BlockSpec((bm, q), lambda b, be, nu: (b, 0)),
                  pl.BlockSpec((1, d, f2), lambda b, be, nu: (be[b], 0, 0)),
                  pl.BlockSpec((1, f2 // 2, d), lambda b, be, nu: (be[b], 0, 0))],
        out_specs=tuple(pl.BlockSpec((bm, SC_ROW_WORDS), lambda b, be, nu: (b, 0)) for _ in range(n_out)),
    )
    return pl.pallas_call(
        _ffn_kernel,
        out_shape=tuple(jax.ShapeDtypeStruct((rows, SC_ROW_WORDS), F32) for _ in range(n_out)),
        grid_spec=grid_spec,
        compiler_params=_params("arbitrary"),
        name="moe_expert_ffn",
    )(block_e, n_used, xs_a, xs_b, w13, w2b)


def _combine_kernel(*refs):
    n_y = len(refs) - 7
    y_refs = refs[:n_y]
    g8_ref, h2_ref, x1_ref, gate2_ref, ws13_ref, ws2_ref, o_ref = refs[n_y:]
    f = ws2_ref.shape[0]
    hs = _dot(h2_ref[...], ws13_ref[...])
    acc = _dot((_silu(hs[:, :f]) * hs[:, f:]).astype(BF16), ws2_ref[...])
    g8 = g8_ref[...]
    for k in range(TOP_K_EXPERTS):
        acc = acc + g8[:, k:k + 1] * jnp.concatenate([y_ref[k] for y_ref in y_refs], axis=1)
    o_ref[...] = x1_ref[...] + gate2_ref[0] * acc


def _combine(y8s, g8, h2, x1, gate2, ws13, ws2b, seq, tm):
    t, d = x1.shape
    per_b = seq // tm
    row = lambda w: pl.BlockSpec((tm, w), lambda i: (i, 0))
    full = lambda a: pl.BlockSpec(a.shape, lambda i: (0,) * a.ndim)
    y_spec = pl.BlockSpec((TOP_K_EXPERTS, tm, SC_ROW_WORDS), lambda i: (0, i, 0))
    return pl.pallas_call(
        _combine_kernel,
        out_shape=jax.ShapeDtypeStruct((t, d), F32),
        grid=(t // tm,),
        in_specs=[y_spec] * len(y8s) + [row(TOP_K_EXPERTS), row(d), row(d),
                                         pl.BlockSpec((1, 1, d), lambda i: (i // per_b, 0, 0)), full(ws13), full(ws2b)],
        out_specs=row(d),
        compiler_params=_params("parallel"),
        name="moe_combine",
    )(*y8s, g8, h2, x1, gate2, ws13, ws2b)


def _moe(h2, h2pa, h2pb, x1, eidx, g8, gate2, w1, w3, w2, ws1, ws3, ws2, seq):
    t, d = x1.shape
    n_e = w1.shape[0]
    bm = MOE_BLOCK_ROWS
    rows = t * TOP_K_EXPERTS + n_e * bm
    w13 = jnp.concatenate([w1, w3], axis=2).astype(BF16)
    w2b = w2.astype(BF16)
    ws13 = jnp.concatenate([ws1, ws3], axis=1).astype(BF16)
    ws2b = ws2.astype(BF16)

    dest, cnt = _route(eidx, bm, min(512, t))
    pend = jnp.cumsum(jnp.ceil(cnt[0] / bm) * bm)
    block_row0 = jnp.arange(rows // bm, dtype=F32) * bm
    block_e = jnp.minimum(jnp.sum(pend[None, :] <= block_row0[:, None], axis=1), n_e - 1)
    n_used = (pend[-1:] / bm).astype(jnp.int32)
    dest_kt = dest.T

    xs_a = _sc_scatter_rows(h2pa, dest_kt, rows)
    xs_b = _sc_scatter_rows(h2pb, dest_kt, rows)
    ys = _ffn(block_e.astype(jnp.int32), n_used, xs_a, xs_b, w13, w2b, bm)
    pair_rows = dest_kt.reshape(-1)
    y8s = [_sc_gather_rows(y, pair_rows).reshape(TOP_K_EXPERTS, t, SC_ROW_WORDS) for y in ys]
    return _combine(y8s, g8, h2, x1, gate2, ws13, ws2b, seq, min(256, seq))


def _layer(x, c, positions, norm1_g, norm2_g, w_ada, b_ada, w_in, conv_w, q_norm_g, k_norm_g, kidx_norm_g, w_out,
           w_router, router_bias, w1, w3, w2, ws1, ws3, ws2):
    bsz, seq, d = x.shape
    t = bsz * seq
    tm = min(512, seq)
    tq = min(512, seq)
    x2 = x.reshape(t, d)

    ada = _ada(c, w_ada, b_ada)
    shift1, scale1, gate1, shift2, scale2, gate2 = [a.reshape(bsz, 1, d) for a in jnp.split(ada, 6, axis=-1)]

    inv_freq = ROPE_THETA ** (-jnp.arange(ROPE_HALF, dtype=F32) / ROPE_HALF)
    invf_lane = inv_freq[(jnp.arange(LANES) % HEAD_DIM) % ROPE_HALF].reshape(1, LANES)
    ct, s1, s2 = _rope_tables(positions.reshape(t, 1), invf_lane, min(2048, t))

    conv, q, k, v, qi, kilo, kihi, wit = _inproj(x2, scale1, shift1, norm1_g, w_in, conv_w, q_norm_g, k_norm_g,
                                                kidx_norm_g, ct, s1, s2, seq, tm)
    attn = _attention(q, k, v, qi, kilo, kihi, wit, bsz, seq, tq, kc=tq)
    x1, h2, h2pa, h2pb, eidx, g8 = _outproj(conv, attn, x2, gate1, scale2, shift2, norm2_g, w_out, w_router,
                                            router_bias, seq, tm)
    out = _moe(h2, h2pa, h2pb, x1, eidx, g8, gate2, w1, w3, w2, ws1, ws3, ws2, seq)
    return out.reshape(bsz, seq, d)


def kernel(x, c, positions, norm1_g, norm2_g, w_ada, b_ada, w_in, conv_w, q_norm_g, k_norm_g, kidx_norm_g, w_out,
           w_router, router_bias, w1, w3, w2, ws1, ws3, ws2):
    for l in range(w_in.shape[0]):
        x = _layer(x, c, positions, norm1_g[l], norm2_g[l], w_ada[l], b_ada[l], w_in[l], conv_w[l], q_norm_g[l],
                   k_norm_g[l], kidx_norm_g[l], w_out[l], w_router[l], router_bias[l], w1[l], w3[l], w2[l], ws1[l],
                   ws3[l], ws2[l])
    return x
```

```python
import functools

import jax
import jax.numpy as jnp
from jax import lax
from jax.experimental import pallas as pl
from jax.experimental.pallas import tpu as pltpu
from jax.experimental.pallas import tpu_sc as plsc

F32 = jnp.float32
BF16 = jnp.bfloat16

HEAD_DIM = 64
ATTN_HEADS = 8
ATTN_DIM = ATTN_HEADS * HEAD_DIM
CONV_DIM = 512
IDX_HEADS = 8
IDX_DIM = 64
IDX_SCALE = (IDX_DIM ** -0.5) * (IDX_HEADS ** -0.5)
TOPK_KEYS_MAX = 256
ROPE_THETA = 500000.0
ROPE_DIM = HEAD_DIM // 4
ROPE_HALF = ROPE_DIM // 2
N_EXPERTS = 64
TOP_K_EXPERTS = 8
EXPERT_DIM = 256
ROUTED_SCALE = 2.5
EPS = 1e-6

LANES = 128
SUBLANES = 8
V7X_VMEM_BYTES = 64 * 1024 * 1024
VMEM_LIMIT = V7X_VMEM_BYTES * 3 // 4
V7X_SC_CORES = 2
V7X_SC_SUBCORES = 16
SC_WINDOW = 128
SC_ROW_WORDS = 256

MASKED = -1e30
Q_SCALE = HEAD_DIM ** -0.5 * 1.4426950408889634
MOE_BLOCK_ROWS = 512


def _params(*semantics):
    return pltpu.CompilerParams(dimension_semantics=semantics, vmem_limit_bytes=VMEM_LIMIT)


def _dot(a, b):
    return jnp.dot(a, b, preferred_element_type=F32)


def _dot_t(a, b):
    return lax.dot_general(a, b, (((1,), (1,)), ((), ())), preferred_element_type=F32)


def _split(a):
    hi = a.astype(BF16)
    lo = (a - hi.astype(F32)).astype(BF16)
    return hi, lo


def _dot3(a, b):
    a_hi, a_lo = _split(a)
    b_hi, b_lo = _split(b)
    return _dot(a_hi, b_hi) + _dot(a_hi, b_lo) + _dot(a_lo, b_hi)


def _silu(v):
    return v * jax.nn.sigmoid(v)


def _rms_mod(xv, g, scale, shift):
    ms = jnp.mean(xv * xv, axis=-1, keepdims=True)
    y = xv * lax.rsqrt(ms + EPS)
    return (y * g) * (1.0 + scale) + shift


def _ada_kernel(c_ref, w_ref, b_ref, o_ref):
    o_ref[...] = _dot3(_silu(c_ref[...]), w_ref[...]) + b_ref[...]


def _ada(c, w_ada, b_ada):
    bsz, d = c.shape
    n = w_ada.shape[1]
    bn = n // 4
    return pl.pallas_call(
        _ada_kernel,
        out_shape=jax.ShapeDtypeStruct((bsz, n), F32),
        grid=(n // bn,),
        in_specs=[
            pl.BlockSpec((bsz, d), lambda i: (0, 0)),
            pl.BlockSpec((d, bn), lambda i: (0, i)),
            pl.BlockSpec((1, bn), lambda i: (0, i)),
        ],
        out_specs=pl.BlockSpec((bsz, bn), lambda i: (0, i)),
        compiler_params=_params("parallel"),
        name="ada",
    )(c, w_ada, b_ada.reshape(1, n))


def _rope_kernel(pos_ref, invf_ref, c_ref, s1_ref, s2_ref):
    ang = pos_ref[...].astype(F32) * invf_ref[...]
    d = lax.broadcasted_iota(jnp.int32, ang.shape, 1) & (HEAD_DIM - 1)
    cos = jnp.cos(ang)
    sin = jnp.sin(ang)
    c_ref[...] = jnp.where(d < ROPE_DIM, cos, 1.0)
    s1_ref[...] = jnp.where(d < ROPE_HALF, -sin, 0.0)
    s2_ref[...] = jnp.where(d < ROPE_HALF, 0.0, jnp.where(d < ROPE_DIM, sin, 0.0))


def _rope_tables(pos, invf_lane, tm):
    t = pos.shape[0]
    spec = pl.BlockSpec((tm, LANES), lambda i: (i, 0))
    shp = jax.ShapeDtypeStruct((t, LANES), F32)
    return pl.pallas_call(
        _rope_kernel,
        out_shape=(shp, shp, shp),
        grid=(t // tm,),
        in_specs=[pl.BlockSpec((tm, 1), lambda i: (i, 0)), pl.BlockSpec((1, LANES), lambda i: (0, 0))],
        out_specs=(spec, spec, spec),
        compiler_params=_params("parallel"),
        name="rope_tables",
    )(pos, invf_lane)


def _rope(y, c, s1, s2):
    return y * c + pltpu.roll(y, LANES - ROPE_HALF, 1) * s1 + pltpu.roll(y, ROPE_HALF, 1) * s2


def _head_rms(xb, avg):
    hi, lo = _split(xb * xb)
    ms = _dot(hi, avg) + _dot(lo, avg)
    return xb * lax.rsqrt(ms + EPS)


def _inproj_kernel(x_ref, xh_ref, sc_ref, sh_ref, g1_ref, wmix_ref, wq_ref, wk_ref, wv_ref, wqi_ref, wl_ref, wlt_ref,
                   cw_ref, qg_ref, kg_ref, kig_ref, ct_ref, s1_ref, s2_ref, avg_ref,
                   conv_o, q_o, k_o, v_o, qi_o, kilo_o, kihi_o, wit_o, *, seq):
    tm = x_ref.shape[0]
    scale = sc_ref[0]
    shift = sh_ref[0]
    g1 = g1_ref[...]
    h = _rms_mod(x_ref[...], g1, scale, shift).astype(BF16)
    hh = _rms_mod(xh_ref[...], g1, scale, shift).astype(BF16)

    mix = _dot(h, wmix_ref[...])
    mixh = _dot(hh, wmix_ref[...])
    u = mix[:, 2 * CONV_DIM:] * mix[:, :CONV_DIM]
    uh = mixh[:, 2 * CONV_DIM:] * mixh[:, :CONV_DIM]
    seq_start = (pl.program_id(0) * tm) % seq == 0
    uh = jnp.where(seq_start, 0.0, uh)
    ext = jnp.concatenate([uh, u], axis=0)
    u1 = pltpu.roll(ext, 1, 0)[SUBLANES:]
    u2 = pltpu.roll(ext, 2, 0)[SUBLANES:]
    conv = u2 * cw_ref[0:1, :] + u1 * cw_ref[1:2, :] + u * cw_ref[2:3, :]
    conv_o[...] = (mix[:, CONV_DIM:2 * CONV_DIM] * conv).astype(BF16)

    ct = ct_ref[...]
    s1 = s1_ref[...]
    s2 = s2_ref[...]
    avg = avg_ref[...]
    qf = _dot(h, wq_ref[...])
    kf = _dot(h, wk_ref[...])
    qif = _dot(h, wqi_ref[...])
    for p in range(ATTN_DIM // LANES):
        sl = slice(p * LANES, (p + 1) * LANES)
        qn = _rope(_head_rms(qf[:, sl], avg) * qg_ref[...], ct, s1, s2)
        q_o[:, sl] = (qn * Q_SCALE).astype(BF16)
        kn = _rope(_head_rms(kf[:, sl], avg) * kg_ref[...], ct, s1, s2)
        k_o[:, sl] = kn.astype(BF16)
        qi_o[:, sl] = _rope(qif[:, sl], ct, s1, s2).astype(BF16)
    v_o[...] = _dot(h, wv_ref[...]).astype(BF16)

    wit_o[...] = _dot_t(wlt_ref[...], h)[IDX_DIM:IDX_DIM + IDX_HEADS, :]
    last = _dot(h, wl_ref[...])
    lane = lax.broadcasted_iota(jnp.int32, last.shape, 1)
    is_key = lane < IDX_DIM
    kin = _head_rms(last, avg) * kig_ref[...]
    kin = _rope(kin, jnp.where(is_key, ct, 1.0), jnp.where(is_key, s1, 0.0), jnp.where(is_key, s2, 0.0))
    klo = jnp.where(is_key, kin, 0.0)
    kilo_o[...] = klo.astype(BF16)
    kihi_o[...] = pltpu.roll(klo, IDX_DIM, 1).astype(BF16)


def _inproj(x2, scale1, shift1, norm1_g, w_in, conv_w, q_norm_g, k_norm_g, kidx_norm_g, ct, s1, s2, seq, tm):
    t, d = x2.shape
    cuts = [0, 3 * CONV_DIM, 3 * CONV_DIM + ATTN_DIM, 3 * CONV_DIM + 2 * ATTN_DIM, 3 * CONV_DIM + 3 * ATTN_DIM,
            3 * CONV_DIM + 3 * ATTN_DIM + IDX_HEADS * IDX_DIM]
    wb = w_in.astype(BF16)
    wmix, wq, wk, wv, wqi = [wb[:, a:b] for a, b in zip(cuts[:-1], cuts[1:])]
    wl = wb[:, cuts[-1]:]
    wl = jnp.pad(wl, ((0, 0), (0, LANES - wl.shape[1])))
    wlt = wl.T
    ones = jnp.ones((1, LANES - IDX_DIM), F32)
    qg = jnp.tile(q_norm_g.reshape(1, HEAD_DIM), (1, 2))
    kg = jnp.tile(k_norm_g.reshape(1, HEAD_DIM), (1, 2))
    kig = jnp.concatenate([kidx_norm_g.reshape(1, IDX_DIM), ones], axis=1)
    blk = jnp.arange(LANES) // HEAD_DIM
    avg = jnp.where(blk[:, None] == blk[None, :], 1.0 / HEAD_DIM, 0.0).astype(BF16)

    bsz = t // seq
    per_b = seq // tm
    row = lambda w: pl.BlockSpec((tm, w), lambda i: (i, 0))
    full = lambda a: pl.BlockSpec(a.shape, lambda i: (0,) * a.ndim)
    mod = pl.BlockSpec((1, 1, d), lambda i: (i // per_b, 0, 0))
    halo = pl.BlockSpec((SUBLANES, d), lambda i: (jnp.maximum(i * (tm // SUBLANES) - 1, 0), 0))
    g1 = norm1_g.reshape(1, d)
    out_shape = (
        jax.ShapeDtypeStruct((t, CONV_DIM), BF16),
        jax.ShapeDtypeStruct((t, ATTN_DIM), BF16),
        jax.ShapeDtypeStruct((t, ATTN_DIM), BF16),
        jax.ShapeDtypeStruct((t, ATTN_DIM), BF16),
        jax.ShapeDtypeStruct((t, IDX_HEADS * IDX_DIM), BF16),
        jax.ShapeDtypeStruct((t, LANES), BF16),
        jax.ShapeDtypeStruct((t, LANES), BF16),
        jax.ShapeDtypeStruct((IDX_HEADS, t), F32),
    )
    del bsz
    return pl.pallas_call(
        functools.partial(_inproj_kernel, seq=seq),
        out_shape=out_shape,
        grid=(t // tm,),
        in_specs=[row(d), halo, mod, mod, full(g1), full(wmix), full(wq), full(wk), full(wv), full(wqi), full(wl),
                  full(wlt), full(conv_w), full(qg), full(kg), full(kig), row(LANES), row(LANES), row(LANES), full(avg)],
        out_specs=(row(CONV_DIM), row(ATTN_DIM), row(ATTN_DIM), row(ATTN_DIM), row(IDX_HEADS * IDX_DIM),
                   row(LANES), row(LANES), pl.BlockSpec((IDX_HEADS, tm), lambda i: (0, i))),
        compiler_params=_params("parallel"),
        name="inproj",
    )(x2, x2, scale1, shift1, g1, wmix, wq, wk, wv, wqi, wl, wlt, conv_w, qg, kg, kig, ct, s1, s2, avg)


def _ukey_to_f32(u):
    s = u ^ jnp.int32(-2 ** 31)
    bits = s ^ ((s >> 31) & jnp.int32(0x7FFFFFFF))
    return lax.bitcast_convert_type(bits, F32)


def _attn_kernel(q_ref, k_ref, v_ref, qi_ref, kilo_ref, kihi_ref, wit_ref, o_ref,
                 sc_ref, qm_ref, acc_ref, m_ref, l_ref, *, n_sel):
    tq = q_ref.shape[0]
    seq = k_ref.shape[0]
    kc = sc_ref.shape[1]
    assert kc == tq
    lb_n = kc // LANES
    j = pl.program_id(1)
    nch = (j + 1) * (tq // kc)
    lane = lax.broadcasted_iota(jnp.int32, (tq, LANES), 1)

    for h in range(ATTN_HEADS):
        qp = q_ref[:, (h // 2) * LANES:(h // 2 + 1) * LANES].astype(F32)
        keep = (lane < HEAD_DIM) if h % 2 == 0 else (lane >= HEAD_DIM)
        qm_ref[h] = jnp.where(keep, qp, 0.0).astype(BF16)

    q_pos = j * tq + lax.broadcasted_iota(jnp.int32, (kc, tq), 1)
    k_off = lax.broadcasted_iota(jnp.int32, (kc, tq), 0)
    w_rows = wit_ref[...]

    def index_chunk(c, carry):
        r0 = pl.multiple_of(c * kc, kc)
        klo = kilo_ref[pl.ds(r0, kc), :]
        khi = kihi_ref[pl.ds(r0, kc), :]
        acc = jnp.zeros((kc, tq), F32)
        for p in range(IDX_HEADS // 2):
            qip = qi_ref[:, p * LANES:(p + 1) * LANES]
            acc = acc + jnp.maximum(_dot_t(klo, qip), 0.0) * w_rows[2 * p:2 * p + 1, :]
            acc = acc + jnp.maximum(_dot_t(khi, qip), 0.0) * w_rows[2 * p + 1:2 * p + 2, :]
        score = acc * IDX_SCALE
        score = jnp.where(r0 + k_off <= q_pos, score, -jnp.inf)
        sc_ref[c] = jnp.where(score == 0.0, 0.0, score)
        return carry

    lax.fori_loop(0, nch, index_chunk, 0)

    t_q = j * tq + lax.broadcasted_iota(jnp.int32, (1, tq), 1)
    k_row = jnp.minimum(t_q + 1, n_sel).astype(F32)
    acc_rows = 4 * SUBLANES
    sub = lax.broadcasted_iota(jnp.int32, (acc_rows, tq), 0)
    idx_bits = (seq - 1).bit_length()

    def count(pred):
        def chunk(c, acc):
            blk = sc_ref[c]
            for g in range(kc // acc_rows):
                kidx = sub + (c * kc + g * acc_rows)
                acc = acc + jnp.where(pred(blk[g * acc_rows:(g + 1) * acc_rows, :], kidx), 1.0, 0.0)
            return acc

        acc = lax.fori_loop(0, nch, chunk, jnp.zeros((acc_rows, tq), F32))
        return jnp.sum(acc, axis=0, keepdims=True)

    def rows8(v):
        return jnp.broadcast_to(v, (acc_rows, tq))

    def value_bit(i, p):
        cand = p | jnp.left_shift(jnp.int32(1), 31 - i)
        cand_f = rows8(_ukey_to_f32(cand))
        cnt = count(lambda v, kidx: v >= cand_f)
        return jnp.where(cnt >= k_row, cand, p)

    thr = _ukey_to_f32(lax.fori_loop(0, 32, value_bit, jnp.zeros((1, tq), jnp.int32)))
    thr8 = rows8(thr)
    n_ge = count(lambda v, kidx: v >= thr8)
    tied = jnp.max(jnp.where(n_ge > k_row, 1.0, 0.0)) > 0.0

    def tie_cut():
        need = k_row - count(lambda v, kidx: v > thr8)

        def index_bit(i, p):
            cand = p | jnp.left_shift(jnp.int32(1), idx_bits - 1 - i)
            cand8 = rows8(cand)
            cnt = count(lambda v, kidx: jnp.where(v == thr8, kidx, seq) < cand8)
            return jnp.where(cnt < need, cand, p)

        return lax.fori_loop(0, idx_bits, index_bit, jnp.zeros((1, tq), jnp.int32))

    cut = lax.cond(tied, tie_cut, lambda: jnp.full((1, tq), seq, jnp.int32))

    def write_bias(c, carry):
        blk = sc_ref[c]
        tie_bias = jnp.where(c * kc + k_off <= cut, 0.0, MASKED)
        bias_t = jnp.where(blk > thr, 0.0, jnp.where(blk == thr, tie_bias, MASKED))
        sc_ref[c] = bias_t.T
        return carry

    lax.fori_loop(0, nch, write_bias, 0)

    for h in range(ATTN_HEADS):
        m_ref[h] = jnp.full((tq, LANES), MASKED, F32)
        l_ref[h] = jnp.zeros((tq, LANES), F32)
        acc_ref[h] = jnp.zeros((tq, LANES), F32)

    def attend(c, carry):
        r0 = pl.multiple_of(c * kc, kc)
        keys = pl.ds(r0, kc)
        bias = sc_ref[c]

        def qk(h):
            return _dot_t(qm_ref[h], k_ref[keys, (h // 2) * LANES:(h // 2 + 1) * LANES])

        s_next = qk(0)
        for h in range(ATTN_HEADS):
            s = s_next + bias
            if h + 1 < ATTN_HEADS:
                s_next = qk(h + 1)
            parts = [s[:, b * LANES:(b + 1) * LANES] for b in range(lb_n)]
            m_old = m_ref[h]
            row_max = jnp.max(functools.reduce(jnp.maximum, parts), axis=1, keepdims=True)
            m_new = jnp.maximum(m_old, row_max)
            alpha = jnp.exp2(m_old - m_new)
            p_parts = [jnp.exp2(part - m_new) for part in parts]
            l_ref[h] = alpha * l_ref[h] + functools.reduce(jnp.add, p_parts)
            p = jnp.concatenate(p_parts, axis=1).astype(BF16)
            acc_ref[h] = alpha * acc_ref[h] + _dot(p, v_ref[keys, (h // 2) * LANES:(h // 2 + 1) * LANES])
            m_ref[h] = m_new
        return carry

    lax.fori_loop(0, nch, attend, 0)
    for pair in range(ATTN_HEADS // 2):
        l_even = jnp.sum(l_ref[2 * pair], axis=1, keepdims=True)
        l_odd = jnp.sum(l_ref[2 * pair + 1], axis=1, keepdims=True)
        o_pair = jnp.where(lane < HEAD_DIM, acc_ref[2 * pair] / l_even, acc_ref[2 * pair + 1] / l_odd)
        o_ref[:, pair * LANES:(pair + 1) * LANES] = o_pair.astype(BF16)


def _attention(q, k, v, qi, kilo, kihi, wit, bsz, seq, tq, kc):
    n_sel = min(TOPK_KEYS_MAX, seq // 4)
    shape3 = lambda a: a.reshape(bsz, seq, a.shape[-1])
    q, k, v, qi, kilo, kihi = map(shape3, (q, k, v, qi, kilo, kihi))
    qblk = lambda w: pl.BlockSpec((None, tq, w), lambda b, j: (b, j, 0))
    kblk = lambda w: pl.BlockSpec((None, seq, w), lambda b, j: (b, 0, 0), pipeline_mode=pl.Buffered(1))
    out = pl.pallas_call(
        functools.partial(_attn_kernel, n_sel=n_sel),
        out_shape=jax.ShapeDtypeStruct((bsz, seq, ATTN_DIM), BF16),
        grid=(bsz, seq // tq),
        in_specs=[qblk(ATTN_DIM), kblk(ATTN_DIM), kblk(ATTN_DIM), qblk(IDX_HEADS * IDX_DIM), kblk(LANES), kblk(LANES),
                  pl.BlockSpec((IDX_HEADS, tq), lambda b, j: (0, b * (seq // tq) + j))],
        out_specs=qblk(ATTN_DIM),
        scratch_shapes=[
            pltpu.VMEM((seq // kc, kc, tq), F32),
            pltpu.VMEM((ATTN_HEADS, tq, LANES), BF16),
            pltpu.VMEM((ATTN_HEADS, tq, LANES), F32),
            pltpu.VMEM((ATTN_HEADS, tq, LANES), F32),
            pltpu.VMEM((ATTN_HEADS, tq, LANES), F32),
        ],
        compiler_params=_params("parallel", "arbitrary"),
        name="dsa_attention",
    )(q, k, v, qi, kilo, kihi, wit)
    return out.reshape(bsz * seq, ATTN_DIM)


def _outproj_kernel(conv_ref, attn_ref, x_ref, gate1_ref, sc_ref, sh_ref, g2_ref, wout_ref, wr_ref, rbias_ref,
                    x1_o, h2_o, h2pa_o, h2pb_o, eidx_o, g8_o):
    mix = _dot(conv_ref[...], wout_ref[:CONV_DIM, :]) + _dot(attn_ref[...], wout_ref[CONV_DIM:, :])
    x1 = x_ref[...] + gate1_ref[0] * mix
    x1_o[...] = x1
    h2 = _rms_mod(x1, g2_ref[...], sc_ref[0], sh_ref[0])
    h2b = h2.astype(BF16)
    h2_o[...] = h2b
    quarter = h2.shape[1] // 4
    bits = lax.bitcast_convert_type(h2b.astype(F32), jnp.int32)
    for half_o, c0 in ((h2pa_o, 0), (h2pb_o, 2 * quarter)):
        low = lax.shift_right_logical(bits[:, c0:c0 + quarter], 16)
        half_o[...] = low | (bits[:, c0 + quarter:c0 + 2 * quarter] & jnp.int32(-65536))

    scores = jax.nn.sigmoid(_dot3(h2, wr_ref[...]))
    work = scores + rbias_ref[...]
    lane = lax.broadcasted_iota(jnp.int32, work.shape, 1).astype(F32)
    slot = lax.broadcasted_iota(jnp.int32, (work.shape[0], TOP_K_EXPERTS), 1)
    eidx = jnp.zeros(slot.shape, F32)
    picked = jnp.zeros(slot.shape, F32)
    for r in range(TOP_K_EXPERTS):
        mx = jnp.max(work, axis=1, keepdims=True)
        first = jnp.min(jnp.where(work == mx, lane, float(N_EXPERTS)), axis=1, keepdims=True)
        onehot = lane == first
        score_r = jnp.sum(jnp.where(onehot, scores, 0.0), axis=1, keepdims=True)
        eidx = jnp.where(slot == r, first, eidx)
        picked = jnp.where(slot == r, score_r, picked)
        work = jnp.where(onehot, -jnp.inf, work)
    eidx_o[...] = eidx.astype(jnp.int32)
    g8_o[...] = picked / jnp.sum(picked, axis=1, keepdims=True) * ROUTED_SCALE


def _outproj(conv, attn, x2, gate1, scale2, shift2, norm2_g, w_out, w_router, router_bias, seq, tm):
    t, d = x2.shape
    e = w_router.shape[1]
    per_b = seq // tm
    row = lambda w: pl.BlockSpec((tm, w), lambda i: (i, 0))
    full = lambda a: pl.BlockSpec(a.shape, lambda i: (0,) * a.ndim)
    mod = pl.BlockSpec((1, 1, d), lambda i: (i // per_b, 0, 0))
    g2 = norm2_g.reshape(1, d)
    wo = w_out.astype(BF16)
    rbias = router_bias.reshape(1, e)
    return pl.pallas_call(
        _outproj_kernel,
        out_shape=(jax.ShapeDtypeStruct((t, d), F32), jax.ShapeDtypeStruct((t, d), BF16),
                   jax.ShapeDtypeStruct((t, d // 4), jnp.int32), jax.ShapeDtypeStruct((t, d // 4), jnp.int32),
                   jax.ShapeDtypeStruct((t, TOP_K_EXPERTS), jnp.int32), jax.ShapeDtypeStruct((t, TOP_K_EXPERTS), F32)),
        grid=(t // tm,),
        in_specs=[row(CONV_DIM), row(ATTN_DIM), row(d), mod, mod, mod, full(g2), full(wo), full(w_router), full(rbias)],
        out_specs=(row(d), row(d), row(d // 4), row(d // 4), row(TOP_K_EXPERTS), row(TOP_K_EXPERTS)),
        compiler_params=_params("parallel"),
        name="outproj_router",
    )(conv, attn, x2, gate1, scale2, shift2, g2, wo, w_router, rbias)


def _route_kernel(eidx_ref, dest_o, cnt_o, cnt_ref, run_ref, *, bm):
    phase = pl.program_id(0)
    i = pl.program_id(1)
    tm = eidx_ref.shape[0]
    e8 = eidx_ref[...]
    lane = lax.broadcasted_iota(jnp.int32, (tm, N_EXPERTS), 1)
    hits = [lane == e8[:, k:k + 1] for k in range(TOP_K_EXPERTS)]
    member = functools.reduce(jnp.add, [jnp.where(hit, 1.0, 0.0) for hit in hits])
    tile_cnt = jnp.sum(member, axis=0, keepdims=True)

    @pl.when(phase == 0)
    def _():
        @pl.when(i == 0)
        def _():
            cnt_ref[...] = jnp.zeros(cnt_ref.shape, F32)

        cnt_ref[...] += tile_cnt

    @pl.when(phase == 1)
    def _():
        @pl.when(i == 0)
        def _():
            blocks = jnp.ceil(cnt_ref[...] / bm)
            r = lax.broadcasted_iota(jnp.int32, (N_EXPERTS, N_EXPERTS), 0)
            c = lax.broadcasted_iota(jnp.int32, (N_EXPERTS, N_EXPERTS), 1)
            before = jnp.where(r < c, 1.0, 0.0).astype(BF16)
            b_hi, b_lo = _split(blocks)
            run_ref[...] = (_dot(b_hi, before) + _dot(b_lo, before)) * bm

        r = lax.broadcasted_iota(jnp.int32, (tm, tm), 0)
        c = lax.broadcasted_iota(jnp.int32, (tm, tm), 1)
        earlier = jnp.where(c < r, 1.0, 0.0).astype(BF16)
        base = run_ref[0:1, :] + _dot(earlier, member.astype(BF16))
        slot = lax.broadcasted_iota(jnp.int32, (tm, TOP_K_EXPERTS), 1)
        dest = jnp.zeros((tm, TOP_K_EXPERTS), F32)
        for k in range(TOP_K_EXPERTS):
            dest = jnp.where(slot == k, jnp.sum(jnp.where(hits[k], base, 0.0), axis=1, keepdims=True), dest)
        dest_o[...] = dest.astype(jnp.int32)
        run_ref[...] += tile_cnt

    cnt_o[...] = cnt_ref[...]


def _route(eidx, bm, tm):
    t = eidx.shape[0]
    return pl.pallas_call(
        functools.partial(_route_kernel, bm=bm),
        out_shape=(jax.ShapeDtypeStruct((t, TOP_K_EXPERTS), jnp.int32),
                   jax.ShapeDtypeStruct((SUBLANES, N_EXPERTS), F32)),
        grid=(2, t // tm),
        in_specs=[pl.BlockSpec((tm, TOP_K_EXPERTS), lambda p, i: (i, 0))],
        out_specs=(pl.BlockSpec((tm, TOP_K_EXPERTS), lambda p, i: (p * i, 0)),
                   pl.BlockSpec((SUBLANES, N_EXPERTS), lambda p, i: (0, 0))),
        scratch_shapes=[pltpu.VMEM((SUBLANES, N_EXPERTS), F32), pltpu.VMEM((SUBLANES, N_EXPERTS), F32)],
        compiler_params=_params("arbitrary", "arbitrary"),
        name="moe_route",
    )(eidx)


def _sc_mesh():
    return plsc.VectorSubcoreMesh(core_axis_name="core", subcore_axis_name="subcore", num_cores=V7X_SC_CORES,
                                  num_subcores=V7X_SC_SUBCORES)


def _sc_scatter_rows(rows, dest_kt, n_out):
    t, d = rows.shape
    n_k = dest_kt.shape[0]
    window = SC_WINDOW

    @functools.partial(pl.kernel, out_type=jax.ShapeDtypeStruct((n_out, d), rows.dtype), mesh=_sc_mesh(),
                       name="moe_dispatch_scatter")
    def scatter(x_hbm, i_hbm, o_hbm):
        def body(x_vmem, i_vmem):
            for k in range(n_k):
                pltpu.sync_copy(x_vmem, o_hbm.at[i_vmem.at[k]])

        pltpu.emit_pipeline(
            body,
            grid=(t // window,),
            in_specs=[pl.BlockSpec((window, d), lambda i: (i, 0)), pl.BlockSpec((n_k, window), lambda i: (0, i))],
            out_specs=[],
            core_axis_name=("core", "subcore"),
            dimension_semantics=(pltpu.PARALLEL,),
        )(x_hbm, i_hbm)

    return scatter(rows, dest_kt)


def _sc_gather_rows(table, idx):
    n = idx.shape[0]
    d = table.shape[1]
    window = SC_WINDOW

    @functools.partial(pl.kernel, out_type=jax.ShapeDtypeStruct((n, d), table.dtype), mesh=_sc_mesh(),
                       name="moe_combine_gather")
    def gather(tab_hbm, i_hbm, o_hbm):
        def body(i_vmem, o_vmem):
            pltpu.sync_copy(tab_hbm.at[i_vmem.at[0]], o_vmem)

        pltpu.emit_pipeline(
            body,
            grid=(n // window,),
            in_specs=[pl.BlockSpec((1, window), lambda i: (0, i))],
            out_specs=[pl.BlockSpec((window, d), lambda i: (i, 0))],
            core_axis_name=("core", "subcore"),
            dimension_semantics=(pltpu.PARALLEL,),
        )(i_hbm, o_hbm)

    return gather(table, idx.reshape(1, n))


def _ffn_kernel(be_ref, nused_ref, xa_ref, xb_ref, w1_ref, w3_ref, w2_ref, *refs):
    ys_refs, (w13_s, w2_s) = refs[:-2], refs[-2:]
    b = pl.program_id(0)
    f = w2_ref.shape[1]

    @pl.when((b == 0) | (be_ref[b] != be_ref[jnp.maximum(b - 1, 0)]))
    def _():
        w13_s[:, :f] = w1_ref[0].astype(BF16)
        w13_s[:, f:] = w3_ref[0].astype(BF16)
        w2_s[...] = w2_ref[0].astype(BF16)

    @pl.when(b < nused_ref[0])
    def _():
        q = xa_ref.shape[1]
        hh = jnp.zeros((xa_ref.shape[0], 2 * f), F32)
        for x_ref, c0 in ((xa_ref, 0), (xb_ref, 2 * q)):
            xu = x_ref[...]
            lo = lax.bitcast_convert_type(lax.shift_left(xu, 16), F32).astype(BF16)
            hi = lax.bitcast_convert_type(xu & jnp.int32(-65536), F32).astype(BF16)
            hh = hh + _dot(lo, w13_s[c0:c0 + q, :]) + _dot(hi, w13_s[c0 + q:c0 + 2 * q, :])
        act = _silu(hh[:, :f]) * hh[:, f:]
        y = _dot(act.astype(BF16), w2_s[...])
        bits = lax.bitcast_convert_type(y.astype(BF16).astype(F32), jnp.int32)
        for j, y_ref in enumerate(ys_refs):
            c0 = 2 * j * SC_ROW_WORDS
            low = lax.shift_right_logical(bits[:, c0:c0 + SC_ROW_WORDS], 16)
            y_ref[...] = low | (bits[:, c0 + SC_ROW_WORDS:c0 + 2 * SC_ROW_WORDS] & jnp.int32(-65536))


def _ffn(block_e, n_used, xs_a, xs_b, w1, w3, w2, bm):
    rows, q = xs_a.shape
    _, d, f = w1.shape
    n_out = d // (2 * SC_ROW_WORDS)
    grid_spec = pltpu.PrefetchScalarGridSpec(
        num_scalar_prefetch=2,
        grid=(rows // bm,),
        in_specs=[pl.BlockSpec((bm, q), lambda b, be, nu: (b, 0)),
                  pl.BlockSpec((bm, q), lambda b, be, nu: (b, 0)),
                  pl.BlockSpec((1, d, f), lambda b, be, nu: (be[b], 0, 0)),
                  pl.BlockSpec((1, d, f), lambda b, be, nu: (be[b], 0, 0)),
                  pl.BlockSpec((1, f, d), lambda b, be, nu: (be[b], 0, 0))],
        out_specs=tuple(pl.BlockSpec((bm, SC_ROW_WORDS), lambda b, be, nu: (b, 0)) for _ in range(n_out)),
        scratch_shapes=[pltpu.VMEM((d, 2 * f), BF16), pltpu.VMEM((f, d), BF16)],
    )
    return pl.pallas_call(
        _ffn_kernel,
        out_shape=tuple(jax.ShapeDtypeStruct((rows, SC_ROW_WORDS), jnp.int32) for _ in range(n_out)),
        grid_spec=grid_spec,
        compiler_params=_params("arbitrary"),
        name="moe_expert_ffn",
    )(block_e, n_used, xs_a, xs_b, w1, w3, w2)


def _combine_kernel(*refs):
    n_y = len(refs) - 7
    y_refs = refs[:n_y]
    g8_ref, h2_ref, x1_ref, gate2_ref, ws13_ref, ws2_ref, o_ref = refs[n_y:]
    f = ws2_ref.shape[0]
    hs = _dot(h2_ref[...], ws13_ref[...])
    acc = _dot((_silu(hs[:, :f]) * hs[:, f:]).astype(BF16), ws2_ref[...])
    g8 = g8_ref[...]
    for k in range(TOP_K_EXPERTS):
        cols = []
        for y_ref in y_refs:
            yu = y_ref[k]
            cols.append(lax.bitcast_convert_type(lax.shift_left(yu, 16), F32))
            cols.append(lax.bitcast_convert_type(yu & jnp.int32(-65536), F32))
        acc = acc + g8[:, k:k + 1] * jnp.concatenate(cols, axis=1)
    o_ref[...] = x1_ref[...] + gate2_ref[0] * acc


def _combine(y8s, g8, h2, x1, gate2, ws13, ws2b, seq, tm):
    t, d = x1.shape
    per_b = seq // tm
    row = lambda w: pl.BlockSpec((tm, w), lambda i: (i, 0))
    full = lambda a: pl.BlockSpec(a.shape, lambda i: (0,) * a.ndim)
    y_spec = pl.BlockSpec((TOP_K_EXPERTS, tm, SC_ROW_WORDS), lambda i: (0, i, 0))
    return pl.pallas_call(
        _combine_kernel,
        out_shape=jax.ShapeDtypeStruct((t, d), F32),
        grid=(t // tm,),
        in_specs=[y_spec] * len(y8s) + [row(TOP_K_EXPERTS), row(d), row(d),
                                         pl.BlockSpec((1, 1, d), lambda i: (i // per_b, 0, 0)), full(ws13), full(ws2b)],
        out_specs=row(d),
        compiler_params=_params("parallel"),
        name="moe_combine",
    )(*y8s, g8, h2, x1, gate2, ws13, ws2b)


def _moe(h2, h2pa, h2pb, x1, eidx, g8, gate2, w1, w3, w2, ws1, ws3, ws2, seq):
    t, d = x1.shape
    n_e = w1.shape[0]
    bm = MOE_BLOCK_ROWS
    rows = t * TOP_K_EXPERTS + n_e * bm
    ws13 = jnp.concatenate([ws1, ws3], axis=1).astype(BF16)
    ws2b = ws2.astype(BF16)

    dest, cnt = _route(eidx, bm, min(512, t))
    pend = jnp.cumsum(jnp.ceil(cnt[0] / bm) * bm)
    block_row0 = jnp.arange(rows // bm, dtype=F32) * bm
    block_e = jnp.minimum(jnp.sum(pend[None, :] <= block_row0[:, None], axis=1), n_e - 1)
    n_used = (pend[-1:] / bm).astype(jnp.int32)
    dest_kt = dest.T

    xs_a = _sc_scatter_rows(h2pa, dest_kt, rows)
    xs_b = _sc_scatter_rows(h2pb, dest_kt, rows)
    ys = _ffn(block_e.astype(jnp.int32), n_used, xs_a, xs_b, w1, w3, w2, bm)
    pair_rows = dest_kt.reshape(-1)
    y8s = [_sc_gather_rows(y, pair_rows).reshape(TOP_K_EXPERTS, t, SC_ROW_WORDS) for y in ys]
    return _combine(y8s, g8, h2, x1, gate2, ws13, ws2b, seq, min(256, seq))


def _layer(x, c, positions, norm1_g, norm2_g, w_ada, b_ada, w_in, conv_w, q_norm_g, k_norm_g, kidx_norm_g, w_out,
           w_router, router_bias, w1, w3, w2, ws1, ws3, ws2):
    bsz, seq, d = x.shape
    t = bsz * seq
    tm = min(512, seq)
    tq = min(512, seq)
    x2 = x.reshape(t, d)

    ada = _ada(c, w_ada, b_ada)
    shift1, scale1, gate1, shift2, scale2, gate2 = [a.reshape(bsz, 1, d) for a in jnp.split(ada, 6, axis=-1)]

    inv_freq = ROPE_THETA ** (-jnp.arange(ROPE_HALF, dtype=F32) / ROPE_HALF)
    invf_lane = inv_freq[(jnp.arange(LANES) % HEAD_DIM) % ROPE_HALF].reshape(1, LANES)
    ct, s1, s2 = _rope_tables(positions.reshape(t, 1), invf_lane, min(2048, t))

    conv, q, k, v, qi, kilo, kihi, wit = _inproj(x2, scale1, shift1, norm1_g, w_in, conv_w, q_norm_g, k_norm_g,
                                                kidx_norm_g, ct, s1, s2, seq, tm)
    attn = _attention(q, k, v, qi, kilo, kihi, wit, bsz, seq, tq, kc=tq)
    x1, h2, h2pa, h2pb, eidx, g8 = _outproj(conv, attn, x2, gate1, scale2, shift2, norm2_g, w_out, w_router,
                                            router_bias, seq, tm)
    out = _moe(h2, h2pa, h2pb, x1, eidx, g8, gate2, w1, w3, w2, ws1, ws3, ws2, seq)
    return out.reshape(bsz, seq, d)


def kernel(x, c, positions, norm1_g, norm2_g, w_ada, b_ada, w_in, conv_w, q_norm_g, k_norm_g, kidx_norm_g, w_out,
           w_router, router_bias, w1, w3, w2, ws1, ws3, ws2):
    for l in range(w_in.shape[0]):
        x = _layer(x, c, positions, norm1_g[l], norm2_g[l], w_ada[l], b_ada[l], w_in[l], conv_w[l], q_norm_g[l],
                   k_norm_g[l], kidx_norm_g[l], w_out[l], w_router[l], router_bias[l], w1[l], w3[l], w2[l], ws1[l],
                   ws3[l], ws2[l])
    return x
```

```python
import functools

import jax
import jax.numpy as jnp
from jax import lax
from jax.experimental import pallas as pl
from jax.experimental.pallas import tpu as pltpu
from jax.experimental.pallas import tpu_sc as plsc

F32 = jnp.float32
BF16 = jnp.bfloat16

HEAD_DIM = 64
ATTN_HEADS = 8
ATTN_DIM = ATTN_HEADS * HEAD_DIM
CONV_DIM = 512
IDX_HEADS = 8
IDX_DIM = 64
IDX_SCALE = (IDX_DIM ** -0.5) * (IDX_HEADS ** -0.5)
TOPK_KEYS_MAX = 256
ROPE_THETA = 500000.0
ROPE_DIM = HEAD_DIM // 4
ROPE_HALF = ROPE_DIM // 2
N_EXPERTS = 64
TOP_K_EXPERTS = 8
EXPERT_DIM = 256
ROUTED_SCALE = 2.5
EPS = 1e-6

LANES = 128
SUBLANES = 8
V7X_VMEM_BYTES = 64 * 1024 * 1024
VMEM_LIMIT = V7X_VMEM_BYTES * 3 // 4
V7X_SC_CORES = 2
V7X_SC_SUBCORES = 16
SC_WINDOW = 128
SC_ROW_WORDS = 256

MASKED = -1e30
Q_SCALE = HEAD_DIM ** -0.5 * 1.4426950408889634
MOE_BLOCK_ROWS = 512
MOE_TOKEN_GROUPS = 2


def _params(*semantics):
    return pltpu.CompilerParams(dimension_semantics=semantics, vmem_limit_bytes=VMEM_LIMIT)


def _dot(a, b):
    return jnp.dot(a, b, preferred_element_type=F32)


def _dot_t(a, b):
    return lax.dot_general(a, b, (((1,), (1,)), ((), ())), preferred_element_type=F32)


def _split(a):
    hi = a.astype(BF16)
    lo = (a - hi.astype(F32)).astype(BF16)
    return hi, lo


def _dot3(a, b):
    a_hi, a_lo = _split(a)
    b_hi, b_lo = _split(b)
    return _dot(a_hi, b_hi) + _dot(a_hi, b_lo) + _dot(a_lo, b_hi)


def _silu(v):
    return v * jax.nn.sigmoid(v)


def _rms_mod(xv, g, scale, shift):
    ms = jnp.mean(xv * xv, axis=-1, keepdims=True)
    y = xv * lax.rsqrt(ms + EPS)
    return (y * g) * (1.0 + scale) + shift


def _ada_kernel(c_ref, w_ref, b_ref, o_ref):
    o_ref[...] = _dot3(_silu(c_ref[...]), w_ref[...]) + b_ref[...]


def _ada(c, w_ada, b_ada):
    bsz, d = c.shape
    n = w_ada.shape[1]
    bn = n // 4
    return pl.pallas_call(
        _ada_kernel,
        out_shape=jax.ShapeDtypeStruct((bsz, n), F32),
        grid=(n // bn,),
        in_specs=[
            pl.BlockSpec((bsz, d), lambda i: (0, 0)),
            pl.BlockSpec((d, bn), lambda i: (0, i)),
            pl.BlockSpec((1, bn), lambda i: (0, i)),
        ],
        out_specs=pl.BlockSpec((bsz, bn), lambda i: (0, i)),
        compiler_params=_params("parallel"),
        name="ada",
    )(c, w_ada, b_ada.reshape(1, n))


def _rope_kernel(pos_ref, invf_ref, c_ref, s1_ref, s2_ref):
    ang = pos_ref[...].astype(F32) * invf_ref[...]
    d = lax.broadcasted_iota(jnp.int32, ang.shape, 1) & (HEAD_DIM - 1)
    cos = jnp.cos(ang)
    sin = jnp.sin(ang)
    c_ref[...] = jnp.where(d < ROPE_DIM, cos, 1.0)
    s1_ref[...] = jnp.where(d < ROPE_HALF, -sin, 0.0)
    s2_ref[...] = jnp.where(d < ROPE_HALF, 0.0, jnp.where(d < ROPE_DIM, sin, 0.0))


def _rope_tables(pos, invf_lane, tm):
    t = pos.shape[0]
    spec = pl.BlockSpec((tm, LANES), lambda i: (i, 0))
    shp = jax.ShapeDtypeStruct((t, LANES), F32)
    return pl.pallas_call(
        _rope_kernel,
        out_shape=(shp, shp, shp),
        grid=(t // tm,),
        in_specs=[pl.BlockSpec((tm, 1), lambda i: (i, 0)), pl.BlockSpec((1, LANES), lambda i: (0, 0))],
        out_specs=(spec, spec, spec),
        compiler_params=_params("parallel"),
        name="rope_tables",
    )(pos, invf_lane)


def _rope(y, c, s1, s2):
    return y * c + pltpu.roll(y, LANES - ROPE_HALF, 1) * s1 + pltpu.roll(y, ROPE_HALF, 1) * s2


def _head_rms(xb, avg):
    hi, lo = _split(xb * xb)
    ms = _dot(hi, avg) + _dot(lo, avg)
    return xb * lax.rsqrt(ms + EPS)


def _inproj_kernel(x_ref, xh_ref, sc_ref, sh_ref, g1_ref, wmix_ref, wq_ref, wk_ref, wv_ref, wqi_ref, wl_ref, wlt_ref,
                   cw_ref, qg_ref, kg_ref, kig_ref, ct_ref, s1_ref, s2_ref, avg_ref,
                   conv_o, q_o, k_o, v_o, qi_o, kilo_o, kihi_o, wit_o, *, seq):
    tm = x_ref.shape[0]
    scale = sc_ref[0]
    shift = sh_ref[0]
    g1 = g1_ref[...]
    h = _rms_mod(x_ref[...], g1, scale, shift).astype(BF16)
    hh = _rms_mod(xh_ref[...], g1, scale, shift).astype(BF16)

    mix = _dot(h, wmix_ref[...])
    mixh = _dot(hh, wmix_ref[...])
    u = mix[:, 2 * CONV_DIM:] * mix[:, :CONV_DIM]
    uh = mixh[:, 2 * CONV_DIM:] * mixh[:, :CONV_DIM]
    seq_start = (pl.program_id(0) * tm) % seq == 0
    uh = jnp.where(seq_start, 0.0, uh)
    ext = jnp.concatenate([uh, u], axis=0)
    u1 = pltpu.roll(ext, 1, 0)[SUBLANES:]
    u2 = pltpu.roll(ext, 2, 0)[SUBLANES:]
    conv = u2 * cw_ref[0:1, :] + u1 * cw_ref[1:2, :] + u * cw_ref[2:3, :]
    conv_o[...] = (mix[:, CONV_DIM:2 * CONV_DIM] * conv).astype(BF16)

    ct = ct_ref[...]
    s1 = s1_ref[...]
    s2 = s2_ref[...]
    avg = avg_ref[...]
    qf = _dot(h, wq_ref[...])
    kf = _dot(h, wk_ref[...])
    qif = _dot(h, wqi_ref[...])
    for p in range(ATTN_DIM // LANES):
        sl = slice(p * LANES, (p + 1) * LANES)
        qn = _rope(_head_rms(qf[:, sl], avg) * qg_ref[...], ct, s1, s2)
        q_o[:, sl] = (qn * Q_SCALE).astype(BF16)
        kn = _rope(_head_rms(kf[:, sl], avg) * kg_ref[...], ct, s1, s2)
        k_o[:, sl] = kn.astype(BF16)
        qi_o[:, sl] = _rope(qif[:, sl], ct, s1, s2).astype(BF16)
    v_o[...] = _dot(h, wv_ref[...]).astype(BF16)

    wit_o[...] = _dot_t(wlt_ref[...], h)[IDX_DIM:IDX_DIM + IDX_HEADS, :]
    last = _dot(h, wl_ref[...])
    lane = lax.broadcasted_iota(jnp.int32, last.shape, 1)
    is_key = lane < IDX_DIM
    kin = _head_rms(last, avg) * kig_ref[...]
    kin = _rope(kin, jnp.where(is_key, ct, 1.0), jnp.where(is_key, s1, 0.0), jnp.where(is_key, s2, 0.0))
    klo = jnp.where(is_key, kin, 0.0)
    kilo_o[...] = klo.astype(BF16)
    kihi_o[...] = pltpu.roll(klo, IDX_DIM, 1).astype(BF16)


def _inproj(x2, scale1, shift1, norm1_g, w_in, conv_w, q_norm_g, k_norm_g, kidx_norm_g, ct, s1, s2, seq, tm):
    t, d = x2.shape
    cuts = [0, 3 * CONV_DIM, 3 * CONV_DIM + ATTN_DIM, 3 * CONV_DIM + 2 * ATTN_DIM, 3 * CONV_DIM + 3 * ATTN_DIM,
            3 * CONV_DIM + 3 * ATTN_DIM + IDX_HEADS * IDX_DIM]
    wb = w_in.astype(BF16)
    wmix, wq, wk, wv, wqi = [wb[:, a:b] for a, b in zip(cuts[:-1], cuts[1:])]
    wl = wb[:, cuts[-1]:]
    wl = jnp.pad(wl, ((0, 0), (0, LANES - wl.shape[1])))
    wlt = wl.T
    ones = jnp.ones((1, LANES - IDX_DIM), F32)
    qg = jnp.tile(q_norm_g.reshape(1, HEAD_DIM), (1, 2))
    kg = jnp.tile(k_norm_g.reshape(1, HEAD_DIM), (1, 2))
    kig = jnp.concatenate([kidx_norm_g.reshape(1, IDX_DIM), ones], axis=1)
    blk = jnp.arange(LANES) // HEAD_DIM
    avg = jnp.where(blk[:, None] == blk[None, :], 1.0 / HEAD_DIM, 0.0).astype(BF16)

    bsz = t // seq
    per_b = seq // tm
    row = lambda w: pl.BlockSpec((tm, w), lambda i: (i, 0))
    full = lambda a: pl.BlockSpec(a.shape, lambda i: (0,) * a.ndim)
    mod = pl.BlockSpec((1, 1, d), lambda i: (i // per_b, 0, 0))
    halo = pl.BlockSpec((SUBLANES, d), lambda i: (jnp.maximum(i * (tm // SUBLANES) - 1, 0), 0))
    g1 = norm1_g.reshape(1, d)
    out_shape = (
        jax.ShapeDtypeStruct((t, CONV_DIM), BF16),
        jax.ShapeDtypeStruct((t, ATTN_DIM), BF16),
        jax.ShapeDtypeStruct((t, ATTN_DIM), BF16),
        jax.ShapeDtypeStruct((t, ATTN_DIM), BF16),
        jax.ShapeDtypeStruct((t, IDX_HEADS * IDX_DIM), BF16),
        jax.ShapeDtypeStruct((t, LANES), BF16),
        jax.ShapeDtypeStruct((t, LANES), BF16),
        jax.ShapeDtypeStruct((IDX_HEADS, t), F32),
    )
    del bsz
    return pl.pallas_call(
        functools.partial(_inproj_kernel, seq=seq),
        out_shape=out_shape,
        grid=(t // tm,),
        in_specs=[row(d), halo, mod, mod, full(g1), full(wmix), full(wq), full(wk), full(wv), full(wqi), full(wl),
                  full(wlt), full(conv_w), full(qg), full(kg), full(kig), row(LANES), row(LANES), row(LANES), full(avg)],
        out_specs=(row(CONV_DIM), row(ATTN_DIM), row(ATTN_DIM), row(ATTN_DIM), row(IDX_HEADS * IDX_DIM),
                   row(LANES), row(LANES), pl.BlockSpec((IDX_HEADS, tm), lambda i: (0, i))),
        compiler_params=_params("parallel"),
        name="inproj",
    )(x2, x2, scale1, shift1, g1, wmix, wq, wk, wv, wqi, wl, wlt, conv_w, qg, kg, kig, ct, s1, s2, avg)


def _ukey_to_f32(u):
    s = u ^ jnp.int32(-2 ** 31)
    bits = s ^ ((s >> 31) & jnp.int32(0x7FFFFFFF))
    return lax.bitcast_convert_type(bits, F32)


def _attn_kernel(q_ref, k_ref, v_ref, qi_ref, kilo_ref, kihi_ref, wit_ref, o_ref,
                 sc_ref, qm_ref, acc_ref, m_ref, l_ref, *, n_sel):
    tq = q_ref.shape[0]
    seq = k_ref.shape[0]
    kc = sc_ref.shape[1]
    assert kc == tq
    lb_n = kc // LANES
    j = pl.program_id(1)
    nch = (j + 1) * (tq // kc)
    lane = lax.broadcasted_iota(jnp.int32, (tq, LANES), 1)

    for h in range(ATTN_HEADS):
        qp = q_ref[:, (h // 2) * LANES:(h // 2 + 1) * LANES].astype(F32)
        keep = (lane < HEAD_DIM) if h % 2 == 0 else (lane >= HEAD_DIM)
        qm_ref[h] = jnp.where(keep, qp, 0.0).astype(BF16)

    q_pos = j * tq + lax.broadcasted_iota(jnp.int32, (kc, tq), 1)
    k_off = lax.broadcasted_iota(jnp.int32, (kc, tq), 0)
    w_rows = wit_ref[...]

    def index_chunk(c, carry):
        r0 = pl.multiple_of(c * kc, kc)
        klo = kilo_ref[pl.ds(r0, kc), :]
        khi = kihi_ref[pl.ds(r0, kc), :]
        acc = jnp.zeros((kc, tq), F32)
        for p in range(IDX_HEADS // 2):
            qip = qi_ref[:, p * LANES:(p + 1) * LANES]
            acc = acc + jnp.maximum(_dot_t(klo, qip), 0.0) * w_rows[2 * p:2 * p + 1, :]
            acc = acc + jnp.maximum(_dot_t(khi, qip), 0.0) * w_rows[2 * p + 1:2 * p + 2, :]
        score = acc * IDX_SCALE
        score = jnp.where(r0 + k_off <= q_pos, score, -jnp.inf)
        sc_ref[c] = jnp.where(score == 0.0, 0.0, score)
        return carry

    lax.fori_loop(0, nch, index_chunk, 0)

    t_q = j * tq + lax.broadcasted_iota(jnp.int32, (1, tq), 1)
    k_row = jnp.minimum(t_q + 1, n_sel).astype(F32)
    acc_rows = 4 * SUBLANES
    sub = lax.broadcasted_iota(jnp.int32, (acc_rows, tq), 0)
    idx_bits = (seq - 1).bit_length()

    def count(pred):
        def chunk(c, acc):
            blk = sc_ref[c]
            for g in range(kc // acc_rows):
                kidx = sub + (c * kc + g * acc_rows)
                acc = acc + jnp.where(pred(blk[g * acc_rows:(g + 1) * acc_rows, :], kidx), 1.0, 0.0)
            return acc

        acc = lax.fori_loop(0, nch, chunk, jnp.zeros((acc_rows, tq), F32))
        return jnp.sum(acc, axis=0, keepdims=True)

    def rows8(v):
        return jnp.broadcast_to(v, (acc_rows, tq))

    def value_bit(i, p):
        cand = p | jnp.left_shift(jnp.int32(1), 31 - i)
        cand_f = rows8(_ukey_to_f32(cand))
        cnt = count(lambda v, kidx: v >= cand_f)
        return jnp.where(cnt >= k_row, cand, p)

    thr = _ukey_to_f32(lax.fori_loop(0, 32, value_bit, jnp.zeros((1, tq), jnp.int32)))
    thr8 = rows8(thr)
    n_ge = count(lambda v, kidx: v >= thr8)
    tied = jnp.max(jnp.where(n_ge > k_row, 1.0, 0.0)) > 0.0

    def tie_cut():
        need = k_row - count(lambda v, kidx: v > thr8)

        def index_bit(i, p):
            cand = p | jnp.left_shift(jnp.int32(1), idx_bits - 1 - i)
            cand8 = rows8(cand)
            cnt = count(lambda v, kidx: jnp.where(v == thr8, kidx, seq) < cand8)
            return jnp.where(cnt < need, cand, p)

        return lax.fori_loop(0, idx_bits, index_bit, jnp.zeros((1, tq), jnp.int32))

    cut = lax.cond(tied, tie_cut, lambda: jnp.full((1, tq), seq, jnp.int32))

    def write_bias(c, carry):
        blk = sc_ref[c]
        tie_bias = jnp.where(c * kc + k_off <= cut, 0.0, MASKED)
        bias_t = jnp.where(blk > thr, 0.0, jnp.where(blk == thr, tie_bias, MASKED))
        sc_ref[c] = bias_t.T
        return carry

    lax.fori_loop(0, nch, write_bias, 0)

    for h in range(ATTN_HEADS):
        m_ref[h] = jnp.full((tq, LANES), MASKED, F32)
        l_ref[h] = jnp.zeros((tq, LANES), F32)
        acc_ref[h] = jnp.zeros((tq, LANES), F32)

    def attend(c, carry):
        r0 = pl.multiple_of(c * kc, kc)
        keys = pl.ds(r0, kc)
        bias = sc_ref[c]

        def qk(h):
            return _dot_t(qm_ref[h], k_ref[keys, (h // 2) * LANES:(h // 2 + 1) * LANES])

        s_next = qk(0)
        for h in range(ATTN_HEADS):
            s = s_next + bias
            if h + 1 < ATTN_HEADS:
                s_next = qk(h + 1)
            parts = [s[:, b * LANES:(b + 1) * LANES] for b in range(lb_n)]
            m_old = m_ref[h]
            row_max = jnp.max(functools.reduce(jnp.maximum, parts), axis=1, keepdims=True)
            m_new = jnp.maximum(m_old, row_max)
            alpha = jnp.exp2(m_old - m_new)
            p_parts = [jnp.exp2(part - m_new) for part in parts]
            l_ref[h] = alpha * l_ref[h] + functools.reduce(jnp.add, p_parts)
            p = jnp.concatenate(p_parts, axis=1).astype(BF16)
            acc_ref[h] = alpha * acc_ref[h] + _dot(p, v_ref[keys, (h // 2) * LANES:(h // 2 + 1) * LANES])
            m_ref[h] = m_new
        return carry

    lax.fori_loop(0, nch, attend, 0)
    for pair in range(ATTN_HEADS // 2):
        l_even = jnp.sum(l_ref[2 * pair], axis=1, keepdims=True)
        l_odd = jnp.sum(l_ref[2 * pair + 1], axis=1, keepdims=True)
        o_pair = jnp.where(lane < HEAD_DIM, acc_ref[2 * pair] / l_even, acc_ref[2 * pair + 1] / l_odd)
        o_ref[:, pair * LANES:(pair + 1) * LANES] = o_pair.astype(BF16)


def _attention(q, k, v, qi, kilo, kihi, wit, bsz, seq, tq, kc):
    n_sel = min(TOPK_KEYS_MAX, seq // 4)
    shape3 = lambda a: a.reshape(bsz, seq, a.shape[-1])
    q, k, v, qi, kilo, kihi = map(shape3, (q, k, v, qi, kilo, kihi))
    qblk = lambda w: pl.BlockSpec((None, tq, w), lambda b, j: (b, j, 0))
    kblk = lambda w: pl.BlockSpec((None, seq, w), lambda b, j: (b, 0, 0), pipeline_mode=pl.Buffered(1))
    out = pl.pallas_call(
        functools.partial(_attn_kernel, n_sel=n_sel),
        out_shape=jax.ShapeDtypeStruct((bsz, seq, ATTN_DIM), BF16),
        grid=(bsz, seq // tq),
        in_specs=[qblk(ATTN_DIM), kblk(ATTN_DIM), kblk(ATTN_DIM), qblk(IDX_HEADS * IDX_DIM), kblk(LANES), kblk(LANES),
                  pl.BlockSpec((IDX_HEADS, tq), lambda b, j: (0, b * (seq // tq) + j))],
        out_specs=qblk(ATTN_DIM),
        scratch_shapes=[
            pltpu.VMEM((seq // kc, kc, tq), F32),
            pltpu.VMEM((ATTN_HEADS, tq, LANES), BF16),
            pltpu.VMEM((ATTN_HEADS, tq, LANES), F32),
            pltpu.VMEM((ATTN_HEADS, tq, LANES), F32),
            pltpu.VMEM((ATTN_HEADS, tq, LANES), F32),
        ],
        compiler_params=_params("parallel", "arbitrary"),
        name="dsa_attention",
    )(q, k, v, qi, kilo, kihi, wit)
    return out.reshape(bsz * seq, ATTN_DIM)


def _outproj_kernel(conv_ref, attn_ref, x_ref, gate1_ref, sc_ref, sh_ref, g2_ref, wout_ref, wr_ref, rbias_ref,
                    x1_o, h2_o, h2pa_o, h2pb_o, eidx_o, g8_o):
    mix = _dot(conv_ref[...], wout_ref[:CONV_DIM, :]) + _dot(attn_ref[...], wout_ref[CONV_DIM:, :])
    x1 = x_ref[...] + gate1_ref[0] * mix
    x1_o[...] = x1
    h2 = _rms_mod(x1, g2_ref[...], sc_ref[0], sh_ref[0])
    h2b = h2.astype(BF16)
    h2_o[...] = h2b
    quarter = h2.shape[1] // 4
    bits = lax.bitcast_convert_type(h2b.astype(F32), jnp.int32)
    for half_o, c0 in ((h2pa_o, 0), (h2pb_o, 2 * quarter)):
        low = lax.shift_right_logical(bits[:, c0:c0 + quarter], 16)
        half_o[...] = low | (bits[:, c0 + quarter:c0 + 2 * quarter] & jnp.int32(-65536))

    scores = jax.nn.sigmoid(_dot3(h2, wr_ref[...]))
    work = scores + rbias_ref[...]
    lane = lax.broadcasted_iota(jnp.int32, work.shape, 1).astype(F32)
    slot = lax.broadcasted_iota(jnp.int32, (work.shape[0], TOP_K_EXPERTS), 1)
    eidx = jnp.zeros(slot.shape, F32)
    picked = jnp.zeros(slot.shape, F32)
    for r in range(TOP_K_EXPERTS):
        mx = jnp.max(work, axis=1, keepdims=True)
        first = jnp.min(jnp.where(work == mx, lane, float(N_EXPERTS)), axis=1, keepdims=True)
        onehot = lane == first
        score_r = jnp.sum(jnp.where(onehot, scores, 0.0), axis=1, keepdims=True)
        eidx = jnp.where(slot == r, first, eidx)
        picked = jnp.where(slot == r, score_r, picked)
        work = jnp.where(onehot, -jnp.inf, work)
    eidx_o[...] = eidx.astype(jnp.int32)
    g8_o[...] = picked / jnp.sum(picked, axis=1, keepdims=True) * ROUTED_SCALE


def _outproj(conv, attn, x2, gate1, scale2, shift2, norm2_g, w_out, w_router, router_bias, seq, tm):
    t, d = x2.shape
    e = w_router.shape[1]
    per_b = seq // tm
    row = lambda w: pl.BlockSpec((tm, w), lambda i: (i, 0))
    full = lambda a: pl.BlockSpec(a.shape, lambda i: (0,) * a.ndim)
    mod = pl.BlockSpec((1, 1, d), lambda i: (i // per_b, 0, 0))
    g2 = norm2_g.reshape(1, d)
    wo = w_out.astype(BF16)
    rbias = router_bias.reshape(1, e)
    return pl.pallas_call(
        _outproj_kernel,
        out_shape=(jax.ShapeDtypeStruct((t, d), F32), jax.ShapeDtypeStruct((t, d), BF16),
                   jax.ShapeDtypeStruct((t, d // 4), jnp.int32), jax.ShapeDtypeStruct((t, d // 4), jnp.int32),
                   jax.ShapeDtypeStruct((t, TOP_K_EXPERTS), jnp.int32), jax.ShapeDtypeStruct((t, TOP_K_EXPERTS), F32)),
        grid=(t // tm,),
        in_specs=[row(CONV_DIM), row(ATTN_DIM), row(d), mod, mod, mod, full(g2), full(wo), full(w_router), full(rbias)],
        out_specs=(row(d), row(d), row(d // 4), row(d // 4), row(TOP_K_EXPERTS), row(TOP_K_EXPERTS)),
        compiler_params=_params("parallel"),
        name="outproj_router",
    )(conv, attn, x2, gate1, scale2, shift2, g2, wo, w_router, rbias)


def _route_kernel(eidx_ref, dest_o, cnt_o, cnt_ref, run_ref, *, bm):
    phase = pl.program_id(0)
    i = pl.program_id(1)
    tm = eidx_ref.shape[0]
    e8 = eidx_ref[...]
    lane = lax.broadcasted_iota(jnp.int32, (tm, N_EXPERTS), 1)
    hits = [lane == e8[:, k:k + 1] for k in range(TOP_K_EXPERTS)]
    member = functools.reduce(jnp.add, [jnp.where(hit, 1.0, 0.0) for hit in hits])
    tile_cnt = jnp.sum(member, axis=0, keepdims=True)

    @pl.when(phase == 0)
    def _():
        @pl.when(i == 0)
        def _():
            cnt_ref[...] = jnp.zeros(cnt_ref.shape, F32)

        cnt_ref[...] += tile_cnt

    @pl.when(phase == 1)
    def _():
        @pl.when(i == 0)
        def _():
            blocks = jnp.ceil(cnt_ref[...] / bm)
            r = lax.broadcasted_iota(jnp.int32, (N_EXPERTS, N_EXPERTS), 0)
            c = lax.broadcasted_iota(jnp.int32, (N_EXPERTS, N_EXPERTS), 1)
            before = jnp.where(r < c, 1.0, 0.0).astype(BF16)
            b_hi, b_lo = _split(blocks)
            run_ref[...] = (_dot(b_hi, before) + _dot(b_lo, before)) * bm

        r = lax.broadcasted_iota(jnp.int32, (tm, tm), 0)
        c = lax.broadcasted_iota(jnp.int32, (tm, tm), 1)
        earlier = jnp.where(c < r, 1.0, 0.0).astype(BF16)
        base = run_ref[0:1, :] + _dot(earlier, member.astype(BF16))
        slot = lax.broadcasted_iota(jnp.int32, (tm, TOP_K_EXPERTS), 1)
        dest = jnp.zeros((tm, TOP_K_EXPERTS), F32)
        for k in range(TOP_K_EXPERTS):
            dest = jnp.where(slot == k, jnp.sum(jnp.where(hits[k], base, 0.0), axis=1, keepdims=True), dest)
        dest_o[...] = dest.astype(jnp.int32)
        run_ref[...] += tile_cnt

    cnt_o[...] = cnt_ref[...]


def _route(eidx, bm, tm, tok0, nt):
    i0 = tok0 // tm
    return pl.pallas_call(
        functools.partial(_route_kernel, bm=bm),
        out_shape=(jax.ShapeDtypeStruct((nt, TOP_K_EXPERTS), jnp.int32),
                   jax.ShapeDtypeStruct((SUBLANES, N_EXPERTS), F32)),
        grid=(2, nt // tm),
        in_specs=[pl.BlockSpec((tm, TOP_K_EXPERTS), lambda p, i: (i + i0, 0))],
        out_specs=(pl.BlockSpec((tm, TOP_K_EXPERTS), lambda p, i: (p * i, 0)),
                   pl.BlockSpec((SUBLANES, N_EXPERTS), lambda p, i: (0, 0))),
        scratch_shapes=[pltpu.VMEM((SUBLANES, N_EXPERTS), F32), pltpu.VMEM((SUBLANES, N_EXPERTS), F32)],
        compiler_params=_params("arbitrary", "arbitrary"),
        name="moe_route",
    )(eidx)


def _sc_mesh():
    return plsc.VectorSubcoreMesh(core_axis_name="core", subcore_axis_name="subcore", num_cores=V7X_SC_CORES,
                                  num_subcores=V7X_SC_SUBCORES)


def _sc_scatter_rows(rows, dest_kt, n_out, tok0):
    d = rows.shape[1]
    n_k, t = dest_kt.shape
    window = SC_WINDOW
    i0 = tok0 // window

    @functools.partial(pl.kernel, out_type=jax.ShapeDtypeStruct((n_out, d), rows.dtype), mesh=_sc_mesh(),
                       name="moe_dispatch_scatter")
    def scatter(x_hbm, i_hbm, o_hbm):
        def body(x_vmem, i_vmem):
            for k in range(n_k):
                pltpu.sync_copy(x_vmem, o_hbm.at[i_vmem.at[k]])

        pltpu.emit_pipeline(
            body,
            grid=(t // window,),
            in_specs=[pl.BlockSpec((window, d), lambda i: (i + i0, 0)),
                      pl.BlockSpec((n_k, window), lambda i: (0, i))],
            out_specs=[],
            core_axis_name=("core", "subcore"),
            dimension_semantics=(pltpu.PARALLEL,),
        )(x_hbm, i_hbm)

    return scatter(rows, dest_kt)


def _sc_gather_rows(table, idx):
    n = idx.shape[0]
    d = table.shape[1]
    window = SC_WINDOW

    @functools.partial(pl.kernel, out_type=jax.ShapeDtypeStruct((n, d), table.dtype), mesh=_sc_mesh(),
                       name="moe_combine_gather")
    def gather(tab_hbm, i_hbm, o_hbm):
        def body(i_vmem, o_vmem):
            pltpu.sync_copy(tab_hbm.at[i_vmem.at[0]], o_vmem)

        pltpu.emit_pipeline(
            body,
            grid=(n // window,),
            in_specs=[pl.BlockSpec((1, window), lambda i: (0, i))],
            out_specs=[pl.BlockSpec((window, d), lambda i: (i, 0))],
            core_axis_name=("core", "subcore"),
            dimension_semantics=(pltpu.PARALLEL,),
        )(i_hbm, o_hbm)

    return gather(table, idx.reshape(1, n))


def _ffn_kernel(be_ref, nused_ref, xa_ref, xb_ref, w1_ref, w3_ref, w2_ref, *refs):
    ys_refs, (w13_s, w2_s) = refs[:-2], refs[-2:]
    b = pl.program_id(0)
    f = w2_ref.shape[1]

    @pl.when((b == 0) | (be_ref[b] != be_ref[jnp.maximum(b - 1, 0)]))
    def _():
        w13_s[:, :f] = w1_ref[0].astype(BF16)
        w13_s[:, f:] = w3_ref[0].astype(BF16)
        w2_s[...] = w2_ref[0].astype(BF16)

    @pl.when(b < nused_ref[0])
    def _():
        cols = []
        for x_ref in (xa_ref, xb_ref):
            xu = x_ref[...]
            cols.append(lax.bitcast_convert_type(lax.shift_left(xu, 16), F32).astype(BF16))
            cols.append(lax.bitcast_convert_type(xu & jnp.int32(-65536), F32).astype(BF16))
        hh = _dot(jnp.concatenate(cols, axis=1), w13_s[...])
        act = _silu(hh[:, :f]) * hh[:, f:]
        y = _dot(act.astype(BF16), w2_s[...])
        bits = lax.bitcast_convert_type(y.astype(BF16).astype(F32), jnp.int32)
        for j, y_ref in enumerate(ys_refs):
            c0 = 2 * j * SC_ROW_WORDS
            low = lax.shift_right_logical(bits[:, c0:c0 + SC_ROW_WORDS], 16)
            y_ref[...] = low | (bits[:, c0 + SC_ROW_WORDS:c0 + 2 * SC_ROW_WORDS] & jnp.int32(-65536))


def _ffn(block_e, n_used, xs_a, xs_b, w1, w3, w2, bm):
    rows, q = xs_a.shape
    _, d, f = w1.shape
    n_out = d // (2 * SC_ROW_WORDS)
    grid_spec = pltpu.PrefetchScalarGridSpec(
        num_scalar_prefetch=2,
        grid=(rows // bm,),
        in_specs=[pl.BlockSpec((bm, q), lambda b, be, nu: (b, 0)),
                  pl.BlockSpec((bm, q), lambda b, be, nu: (b, 0)),
                  pl.BlockSpec((1, d, f), lambda b, be, nu: (be[b], 0, 0)),
                  pl.BlockSpec((1, d, f), lambda b, be, nu: (be[b], 0, 0)),
                  pl.BlockSpec((1, f, d), lambda b, be, nu: (be[b], 0, 0))],
        out_specs=tuple(pl.BlockSpec((bm, SC_ROW_WORDS), lambda b, be, nu: (b, 0)) for _ in range(n_out)),
        scratch_shapes=[pltpu.VMEM((d, 2 * f), BF16), pltpu.VMEM((f, d), BF16)],
    )
    return pl.pallas_call(
        _ffn_kernel,
        out_shape=tuple(jax.ShapeDtypeStruct((rows, SC_ROW_WORDS), jnp.int32) for _ in range(n_out)),
        grid_spec=grid_spec,
        compiler_params=_params("arbitrary"),
        name="moe_expert_ffn",
    )(block_e, n_used, xs_a, xs_b, w1, w3, w2)


def _combine_kernel(*refs, n_y):
    y_refs = refs[:n_y]
    g8_ref, h2_ref, x1_ref, gate2_ref, ws13_ref, ws2_ref = refs[n_y:n_y + 6]
    o_ref = refs[-1]
    f = ws2_ref.shape[0]
    hs = _dot(h2_ref[...], ws13_ref[...])
    acc = _dot((_silu(hs[:, :f]) * hs[:, f:]).astype(BF16), ws2_ref[...])
    g8 = g8_ref[...]
    for k in range(TOP_K_EXPERTS):
        cols = []
        for y_ref in y_refs:
            yu = y_ref[k]
            cols.append(lax.bitcast_convert_type(lax.shift_left(yu, 16), F32))
            cols.append(lax.bitcast_convert_type(yu & jnp.int32(-65536), F32))
        acc = acc + g8[:, k:k + 1] * jnp.concatenate(cols, axis=1)
    o_ref[...] = x1_ref[...] + gate2_ref[0] * acc


def _combine(y8s, g8, h2, x1, gate2, ws13, ws2b, seq, tm, tok0, out_prev):
    t, d = x1.shape
    nt = y8s[0].shape[1]
    per_b = seq // tm
    i0 = tok0 // tm
    row = lambda w: pl.BlockSpec((tm, w), lambda i: (i + i0, 0))
    full = lambda a: pl.BlockSpec(a.shape, lambda i: (0,) * a.ndim)
    y_spec = pl.BlockSpec((TOP_K_EXPERTS, tm, SC_ROW_WORDS), lambda i: (0, i, 0))
    in_specs = [y_spec] * len(y8s) + [row(TOP_K_EXPERTS), row(d), row(d),
                                      pl.BlockSpec((1, 1, d), lambda i: ((i + i0) // per_b, 0, 0)), full(ws13), full(ws2b)]
    args = [*y8s, g8, h2, x1, gate2, ws13, ws2b]
    aliases = {}
    if out_prev is not None:
        in_specs.append(pl.BlockSpec(memory_space=pl.ANY))
        aliases = {len(args): 0}
        args.append(out_prev)
    return pl.pallas_call(
        functools.partial(_combine_kernel, n_y=len(y8s)),
        out_shape=jax.ShapeDtypeStruct((t, d), F32),
        grid=(nt // tm,),
        in_specs=in_specs,
        out_specs=row(d),
        input_output_aliases=aliases,
        compiler_params=_params("parallel"),
        name="moe_combine",
    )(*args)


def _moe(h2, h2pa, h2pb, x1, eidx, g8, gate2, w1, w3, w2, ws1, ws3, ws2, seq):
    t, d = x1.shape
    n_e = w1.shape[0]
    bm = MOE_BLOCK_ROWS
    ws13 = jnp.concatenate([ws1, ws3], axis=1).astype(BF16)
    ws2b = ws2.astype(BF16)

    nt = t // MOE_TOKEN_GROUPS
    rows = nt * TOP_K_EXPERTS + n_e * bm
    out = None
    for grp in range(MOE_TOKEN_GROUPS):
        tok0 = grp * nt
        dest, cnt = _route(eidx, bm, min(512, nt), tok0, nt)
        pend = jnp.cumsum(jnp.ceil(cnt[0] / bm) * bm)
        block_row0 = jnp.arange(rows // bm, dtype=F32) * bm
        block_e = jnp.minimum(jnp.sum(pend[None, :] <= block_row0[:, None], axis=1), n_e - 1)
        n_used = (pend[-1:] / bm).astype(jnp.int32)
        dest_kt = dest.T

        xs_a = _sc_scatter_rows(h2pa, dest_kt, rows, tok0)
        xs_b = _sc_scatter_rows(h2pb, dest_kt, rows, tok0)
        ys = _ffn(block_e.astype(jnp.int32), n_used, xs_a, xs_b, w1, w3, w2, bm)
        pair_rows = dest_kt.reshape(-1)
        y8s = [_sc_gather_rows(y, pair_rows).reshape(TOP_K_EXPERTS, nt, SC_ROW_WORDS) for y in ys]
        out = _combine(y8s, g8, h2, x1, gate2, ws13, ws2b, seq, min(256, seq), tok0, out)
    return out


def _layer(x, c, positions, norm1_g, norm2_g, w_ada, b_ada, w_in, conv_w, q_norm_g, k_norm_g, kidx_norm_g, w_out,
           w_router, router_bias, w1, w3, w2, ws1, ws3, ws2):
    bsz, seq, d = x.shape
    t = bsz * seq
    tm = min(512, seq)
    tq = min(512, seq)
    x2 = x.reshape(t, d)

    ada = _ada(c, w_ada, b_ada)
    shift1, scale1, gate1, shift2, scale2, gate2 = [a.reshape(bsz, 1, d) for a in jnp.split(ada, 6, axis=-1)]

    inv_freq = ROPE_THETA ** (-jnp.arange(ROPE_HALF, dtype=F32) / ROPE_HALF)
    invf_lane = inv_freq[(jnp.arange(LANES) % HEAD_DIM) % ROPE_HALF].reshape(1, LANES)
    ct, s1, s2 = _rope_tables(positions.reshape(t, 1), invf_lane, min(2048, t))

    conv, q, k, v, qi, kilo, kihi, wit = _inproj(x2, scale1, shift1, norm1_g, w_in, conv_w, q_norm_g, k_norm_g,
                                                kidx_norm_g, ct, s1, s2, seq, tm)
    attn = _attention(q, k, v, qi, kilo, kihi, wit, bsz, seq, tq, kc=tq)
    x1, h2, h2pa, h2pb, eidx, g8 = _outproj(conv, attn, x2, gate1, scale2, shift2, norm2_g, w_out, w_router,
                                            router_bias, seq, tm)
    out = _moe(h2, h2pa, h2pb, x1, eidx, g8, gate2, w1, w3, w2, ws1, ws3, ws2, seq)
    return out.reshape(bsz, seq, d)


def kernel(x, c, positions, norm1_g, norm2_g, w_ada, b_ada, w_in, conv_w, q_norm_g, k_norm_g, kidx_norm_g, w_out,
           w_router, router_bias, w1, w3, w2, ws1, ws3, ws2):
    for l in range(w_in.shape[0]):
        x = _layer(x, c, positions, norm1_g[l], norm2_g[l], w_ada[l], b_ada[l], w_in[l], conv_w[l], q_norm_g[l],
                   k_norm_g[l], kidx_norm_g[l], w_out[l], w_router[l], router_bias[l], w1[l], w3[l], w2[l], ws1[l],
                   ws3[l], ws2[l])
    return x
```

```python
import functools

import jax
import jax.numpy as jnp
from jax import lax
from jax.experimental import pallas as pl
from jax.experimental.pallas import tpu as pltpu
from jax.experimental.pallas import tpu_sc as plsc

F32 = jnp.float32
BF16 = jnp.bfloat16

HEAD_DIM = 64
ATTN_HEADS = 8
ATTN_DIM = ATTN_HEADS * HEAD_DIM
CONV_DIM = 512
IDX_HEADS = 8
IDX_DIM = 64
IDX_SCALE = (IDX_DIM ** -0.5) * (IDX_HEADS ** -0.5)
TOPK_KEYS_MAX = 256
ROPE_THETA = 500000.0
ROPE_DIM = HEAD_DIM // 4
ROPE_HALF = ROPE_DIM // 2
N_EXPERTS = 64
TOP_K_EXPERTS = 8
EXPERT_DIM = 256
ROUTED_SCALE = 2.5
EPS = 1e-6

LANES = 128
SUBLANES = 8
V7X_VMEM_BYTES = 64 * 1024 * 1024
VMEM_LIMIT = V7X_VMEM_BYTES * 3 // 4
V7X_SC_CORES = 2
V7X_SC_SUBCORES = 16
SC_WINDOW = 128
SC_ROW_WORDS = 256

MASKED = -1e30
Q_SCALE = HEAD_DIM ** -0.5 * 1.4426950408889634
MOE_BLOCK_ROWS = 1024
MOE_TOKEN_GROUPS = 1


def _params(*semantics):
    return pltpu.CompilerParams(dimension_semantics=semantics, vmem_limit_bytes=VMEM_LIMIT)


def _dot(a, b):
    return jnp.dot(a, b, preferred_element_type=F32)


def _dot_t(a, b):
    return lax.dot_general(a, b, (((1,), (1,)), ((), ())), preferred_element_type=F32)


def _split(a):
    hi = a.astype(BF16)
    lo = (a - hi.astype(F32)).astype(BF16)
    return hi, lo


def _dot3(a, b):
    a_hi, a_lo = _split(a)
    b_hi, b_lo = _split(b)
    return _dot(a_hi, b_hi) + _dot(a_hi, b_lo) + _dot(a_lo, b_hi)


def _silu(v):
    return v * jax.nn.sigmoid(v)


def _rms_mod(xv, g, scale, shift):
    ms = jnp.mean(xv * xv, axis=-1, keepdims=True)
    y = xv * lax.rsqrt(ms + EPS)
    return (y * g) * (1.0 + scale) + shift


def _ada_kernel(c_ref, w_ref, b_ref, o_ref):
    o_ref[...] = _dot3(_silu(c_ref[...]), w_ref[...]) + b_ref[...]


def _ada(c, w_ada, b_ada):
    bsz, d = c.shape
    n = w_ada.shape[1]
    bn = n // 4
    return pl.pallas_call(
        _ada_kernel,
        out_shape=jax.ShapeDtypeStruct((bsz, n), F32),
        grid=(n // bn,),
        in_specs=[
            pl.BlockSpec((bsz, d), lambda i: (0, 0)),
            pl.BlockSpec((d, bn), lambda i: (0, i)),
            pl.BlockSpec((1, bn), lambda i: (0, i)),
        ],
        out_specs=pl.BlockSpec((bsz, bn), lambda i: (0, i)),
        compiler_params=_params("parallel"),
        name="ada",
    )(c, w_ada, b_ada.reshape(1, n))


def _rope_kernel(pos_ref, invf_ref, c_ref, s1_ref, s2_ref):
    ang = pos_ref[...].astype(F32) * invf_ref[...]
    d = lax.broadcasted_iota(jnp.int32, ang.shape, 1) & (HEAD_DIM - 1)
    cos = jnp.cos(ang)
    sin = jnp.sin(ang)
    c_ref[...] = jnp.where(d < ROPE_DIM, cos, 1.0)
    s1_ref[...] = jnp.where(d < ROPE_HALF, -sin, 0.0)
    s2_ref[...] = jnp.where(d < ROPE_HALF, 0.0, jnp.where(d < ROPE_DIM, sin, 0.0))


def _rope_tables(pos, invf_lane, tm):
    t = pos.shape[0]
    spec = pl.BlockSpec((tm, LANES), lambda i: (i, 0))
    shp = jax.ShapeDtypeStruct((t, LANES), F32)
    return pl.pallas_call(
        _rope_kernel,
        out_shape=(shp, shp, shp),
        grid=(t // tm,),
        in_specs=[pl.BlockSpec((tm, 1), lambda i: (i, 0)), pl.BlockSpec((1, LANES), lambda i: (0, 0))],
        out_specs=(spec, spec, spec),
        compiler_params=_params("parallel"),
        name="rope_tables",
    )(pos, invf_lane)


def _rope(y, c, s1, s2):
    return y * c + pltpu.roll(y, LANES - ROPE_HALF, 1) * s1 + pltpu.roll(y, ROPE_HALF, 1) * s2


def _head_rms(xb, avg):
    hi, lo = _split(xb * xb)
    ms = _dot(hi, avg) + _dot(lo, avg)
    return xb * lax.rsqrt(ms + EPS)


def _inproj_kernel(x_ref, xh_ref, sc_ref, sh_ref, g1_ref, wmix_ref, wq_ref, wk_ref, wv_ref, wqi_ref, wl_ref, wlt_ref,
                   cw_ref, qg_ref, kg_ref, kig_ref, ct_ref, s1_ref, s2_ref, avg_ref,
                   conv_o, q_o, k_o, v_o, qi_o, kilo_o, kihi_o, wit_o, *, seq):
    tm = x_ref.shape[0]
    scale = sc_ref[0]
    shift = sh_ref[0]
    g1 = g1_ref[...]
    h = _rms_mod(x_ref[...], g1, scale, shift).astype(BF16)
    hh = _rms_mod(xh_ref[...], g1, scale, shift).astype(BF16)

    mix = _dot(h, wmix_ref[...])
    mixh = _dot(hh, wmix_ref[...])
    u = mix[:, 2 * CONV_DIM:] * mix[:, :CONV_DIM]
    uh = mixh[:, 2 * CONV_DIM:] * mixh[:, :CONV_DIM]
    seq_start = (pl.program_id(0) * tm) % seq == 0
    uh = jnp.where(seq_start, 0.0, uh)
    ext = jnp.concatenate([uh, u], axis=0)
    u1 = pltpu.roll(ext, 1, 0)[SUBLANES:]
    u2 = pltpu.roll(ext, 2, 0)[SUBLANES:]
    conv = u2 * cw_ref[0:1, :] + u1 * cw_ref[1:2, :] + u * cw_ref[2:3, :]
    conv_o[...] = (mix[:, CONV_DIM:2 * CONV_DIM] * conv).astype(BF16)

    ct = ct_ref[...]
    s1 = s1_ref[...]
    s2 = s2_ref[...]
    avg = avg_ref[...]
    qf = _dot(h, wq_ref[...])
    kf = _dot(h, wk_ref[...])
    qif = _dot(h, wqi_ref[...])
    for p in range(ATTN_DIM // LANES):
        sl = slice(p * LANES, (p + 1) * LANES)
        qn = _rope(_head_rms(qf[:, sl], avg) * qg_ref[...], ct, s1, s2)
        q_o[:, sl] = (qn * Q_SCALE).astype(BF16)
        kn = _rope(_head_rms(kf[:, sl], avg) * kg_ref[...], ct, s1, s2)
        k_o[:, sl] = kn.astype(BF16)
        qi_o[:, sl] = _rope(qif[:, sl], ct, s1, s2).astype(BF16)
    v_o[...] = _dot(h, wv_ref[...]).astype(BF16)

    wit_o[...] = _dot_t(wlt_ref[...], h)[IDX_DIM:IDX_DIM + IDX_HEADS, :]
    last = _dot(h, wl_ref[...])
    lane = lax.broadcasted_iota(jnp.int32, last.shape, 1)
    is_key = lane < IDX_DIM
    kin = _head_rms(last, avg) * kig_ref[...]
    kin = _rope(kin, jnp.where(is_key, ct, 1.0), jnp.where(is_key, s1, 0.0), jnp.where(is_key, s2, 0.0))
    klo = jnp.where(is_key, kin, 0.0)
    kilo_o[...] = klo.astype(BF16)
    kihi_o[...] = pltpu.roll(klo, IDX_DIM, 1).astype(BF16)


def _inproj(x2, scale1, shift1, norm1_g, w_in, conv_w, q_norm_g, k_norm_g, kidx_norm_g, ct, s1, s2, seq, tm):
    t, d = x2.shape
    cuts = [0, 3 * CONV_DIM, 3 * CONV_DIM + ATTN_DIM, 3 * CONV_DIM + 2 * ATTN_DIM, 3 * CONV_DIM + 3 * ATTN_DIM,
            3 * CONV_DIM + 3 * ATTN_DIM + IDX_HEADS * IDX_DIM]
    wb = w_in.astype(BF16)
    wmix, wq, wk, wv, wqi = [wb[:, a:b] for a, b in zip(cuts[:-1], cuts[1:])]
    wl = wb[:, cuts[-1]:]
    wl = jnp.pad(wl, ((0, 0), (0, LANES - wl.shape[1])))
    wlt = wl.T
    ones = jnp.ones((1, LANES - IDX_DIM), F32)
    qg = jnp.tile(q_norm_g.reshape(1, HEAD_DIM), (1, 2))
    kg = jnp.tile(k_norm_g.reshape(1, HEAD_DIM), (1, 2))
    kig = jnp.concatenate([kidx_norm_g.reshape(1, IDX_DIM), ones], axis=1)
    blk = jnp.arange(LANES) // HEAD_DIM
    avg = jnp.where(blk[:, None] == blk[None, :], 1.0 / HEAD_DIM, 0.0).astype(BF16)

    bsz = t // seq
    per_b = seq // tm
    row = lambda w: pl.BlockSpec((tm, w), lambda i: (i, 0))
    full = lambda a: pl.BlockSpec(a.shape, lambda i: (0,) * a.ndim)
    mod = pl.BlockSpec((1, 1, d), lambda i: (i // per_b, 0, 0))
    halo = pl.BlockSpec((SUBLANES, d), lambda i: (jnp.maximum(i * (tm // SUBLANES) - 1, 0), 0))
    g1 = norm1_g.reshape(1, d)
    out_shape = (
        jax.ShapeDtypeStruct((t, CONV_DIM), BF16),
        jax.ShapeDtypeStruct((t, ATTN_DIM), BF16),
        jax.ShapeDtypeStruct((t, ATTN_DIM), BF16),
        jax.ShapeDtypeStruct((t, ATTN_DIM), BF16),
        jax.ShapeDtypeStruct((t, IDX_HEADS * IDX_DIM), BF16),
        jax.ShapeDtypeStruct((t, LANES), BF16),
        jax.ShapeDtypeStruct((t, LANES), BF16),
        jax.ShapeDtypeStruct((IDX_HEADS, t), F32),
    )
    del bsz
    return pl.pallas_call(
        functools.partial(_inproj_kernel, seq=seq),
        out_shape=out_shape,
        grid=(t // tm,),
        in_specs=[row(d), halo, mod, mod, full(g1), full(wmix), full(wq), full(wk), full(wv), full(wqi), full(wl),
                  full(wlt), full(conv_w), full(qg), full(kg), full(kig), row(LANES), row(LANES), row(LANES), full(avg)],
        out_specs=(row(CONV_DIM), row(ATTN_DIM), row(ATTN_DIM), row(ATTN_DIM), row(IDX_HEADS * IDX_DIM),
                   row(LANES), row(LANES), pl.BlockSpec((IDX_HEADS, tm), lambda i: (0, i))),
        compiler_params=_params("parallel"),
        name="inproj",
    )(x2, x2, scale1, shift1, g1, wmix, wq, wk, wv, wqi, wl, wlt, conv_w, qg, kg, kig, ct, s1, s2, avg)


def _ukey_to_f32(u):
    s = u ^ jnp.int32(-2 ** 31)
    bits = s ^ ((s >> 31) & jnp.int32(0x7FFFFFFF))
    return lax.bitcast_convert_type(bits, F32)


def _attn_kernel(q_ref, k_ref, v_ref, qi_ref, kilo_ref, kihi_ref, wit_ref, o_ref,
                 sc_ref, qm_ref, acc_ref, m_ref, l_ref, *, n_sel):
    tq = q_ref.shape[0]
    seq = k_ref.shape[0]
    kc = sc_ref.shape[1]
    assert kc == tq
    lb_n = kc // LANES
    j = pl.program_id(1)
    nch = (j + 1) * (tq // kc)
    lane = lax.broadcasted_iota(jnp.int32, (tq, LANES), 1)

    for h in range(ATTN_HEADS):
        qp = q_ref[:, (h // 2) * LANES:(h // 2 + 1) * LANES].astype(F32)
        keep = (lane < HEAD_DIM) if h % 2 == 0 else (lane >= HEAD_DIM)
        qm_ref[h] = jnp.where(keep, qp, 0.0).astype(BF16)

    q_pos = j * tq + lax.broadcasted_iota(jnp.int32, (kc, tq), 1)
    k_off = lax.broadcasted_iota(jnp.int32, (kc, tq), 0)
    w_rows = wit_ref[...]

    def index_chunk(c, carry):
        r0 = pl.multiple_of(c * kc, kc)
        klo = kilo_ref[pl.ds(r0, kc), :]
        khi = kihi_ref[pl.ds(r0, kc), :]
        acc = jnp.zeros((kc, tq), F32)
        for p in range(IDX_HEADS // 2):
            qip = qi_ref[:, p * LANES:(p + 1) * LANES]
            acc = acc + jnp.maximum(_dot_t(klo, qip), 0.0) * w_rows[2 * p:2 * p + 1, :]
            acc = acc + jnp.maximum(_dot_t(khi, qip), 0.0) * w_rows[2 * p + 1:2 * p + 2, :]
        score = acc * IDX_SCALE
        score = jnp.where(r0 + k_off <= q_pos, score, -jnp.inf)
        sc_ref[c] = jnp.where(score == 0.0, 0.0, score)
        return carry

    lax.fori_loop(0, nch, index_chunk, 0)

    t_q = j * tq + lax.broadcasted_iota(jnp.int32, (1, tq), 1)
    k_row = jnp.minimum(t_q + 1, n_sel).astype(F32)
    acc_rows = 4 * SUBLANES
    sub = lax.broadcasted_iota(jnp.int32, (acc_rows, tq), 0)
    idx_bits = (seq - 1).bit_length()

    def count(pred):
        def chunk(c, acc):
            blk = sc_ref[c]
            for g in range(kc // acc_rows):
                kidx = sub + (c * kc + g * acc_rows)
                acc = acc + jnp.where(pred(blk[g * acc_rows:(g + 1) * acc_rows, :], kidx), 1.0, 0.0)
            return acc

        acc = lax.fori_loop(0, nch, chunk, jnp.zeros((acc_rows, tq), F32))
        return jnp.sum(acc, axis=0, keepdims=True)

    def rows8(v):
        return jnp.broadcast_to(v, (acc_rows, tq))

    def value_bit(i, p):
        cand = p | jnp.left_shift(jnp.int32(1), 31 - i)
        cand_f = rows8(_ukey_to_f32(cand))
        cnt = count(lambda v, kidx: v >= cand_f)
        return jnp.where(cnt >= k_row, cand, p)

    thr = _ukey_to_f32(lax.fori_loop(0, 32, value_bit, jnp.zeros((1, tq), jnp.int32)))
    thr8 = rows8(thr)
    n_ge = count(lambda v, kidx: v >= thr8)
    tied = jnp.max(jnp.where(n_ge > k_row, 1.0, 0.0)) > 0.0

    def tie_cut():
        need = k_row - count(lambda v, kidx: v > thr8)

        def index_bit(i, p):
            cand = p | jnp.left_shift(jnp.int32(1), idx_bits - 1 - i)
            cand8 = rows8(cand)
            cnt = count(lambda v, kidx: jnp.where(v == thr8, kidx, seq) < cand8)
            return jnp.where(cnt < need, cand, p)

        return lax.fori_loop(0, idx_bits, index_bit, jnp.zeros((1, tq), jnp.int32))

    cut = lax.cond(tied, tie_cut, lambda: jnp.full((1, tq), seq, jnp.int32))

    def write_bias(c, carry):
        blk = sc_ref[c]
        tie_bias = jnp.where(c * kc + k_off <= cut, 0.0, MASKED)
        bias_t = jnp.where(blk > thr, 0.0, jnp.where(blk == thr, tie_bias, MASKED))
        sc_ref[c] = bias_t.T
        return carry

    lax.fori_loop(0, nch, write_bias, 0)

    for h in range(ATTN_HEADS):
        m_ref[h] = jnp.full((tq, LANES), MASKED, F32)
        l_ref[h] = jnp.zeros((tq, LANES), F32)
        acc_ref[h] = jnp.zeros((tq, LANES), F32)

    def attend(c, carry):
        r0 = pl.multiple_of(c * kc, kc)
        keys = pl.ds(r0, kc)
        bias = sc_ref[c]

        def qk(h):
            return _dot_t(qm_ref[h], k_ref[keys, (h // 2) * LANES:(h // 2 + 1) * LANES])

        s_next = qk(0)
        for h in range(ATTN_HEADS):
            s = s_next + bias
            if h + 1 < ATTN_HEADS:
                s_next = qk(h + 1)
            parts = [s[:, b * LANES:(b + 1) * LANES] for b in range(lb_n)]
            m_old = m_ref[h]
            row_max = jnp.max(functools.reduce(jnp.maximum, parts), axis=1, keepdims=True)
            m_new = jnp.maximum(m_old, row_max)
            alpha = jnp.exp2(m_old - m_new)
            p_parts = [jnp.exp2(part - m_new) for part in parts]
            l_ref[h] = alpha * l_ref[h] + functools.reduce(jnp.add, p_parts)
            p = jnp.concatenate(p_parts, axis=1).astype(BF16)
            acc_ref[h] = alpha * acc_ref[h] + _dot(p, v_ref[keys, (h // 2) * LANES:(h // 2 + 1) * LANES])
            m_ref[h] = m_new
        return carry

    lax.fori_loop(0, nch, attend, 0)
    for pair in range(ATTN_HEADS // 2):
        l_even = jnp.sum(l_ref[2 * pair], axis=1, keepdims=True)
        l_odd = jnp.sum(l_ref[2 * pair + 1], axis=1, keepdims=True)
        o_pair = jnp.where(lane < HEAD_DIM, acc_ref[2 * pair] / l_even, acc_ref[2 * pair + 1] / l_odd)
        o_ref[:, pair * LANES:(pair + 1) * LANES] = o_pair.astype(BF16)


def _attention(q, k, v, qi, kilo, kihi, wit, bsz, seq, tq, kc):
    n_sel = min(TOPK_KEYS_MAX, seq // 4)
    shape3 = lambda a: a.reshape(bsz, seq, a.shape[-1])
    q, k, v, qi, kilo, kihi = map(shape3, (q, k, v, qi, kilo, kihi))
    qblk = lambda w: pl.BlockSpec((None, tq, w), lambda b, j: (b, j, 0))
    kblk = lambda w: pl.BlockSpec((None, seq, w), lambda b, j: (b, 0, 0), pipeline_mode=pl.Buffered(1))
    out = pl.pallas_call(
        functools.partial(_attn_kernel, n_sel=n_sel),
        out_shape=jax.ShapeDtypeStruct((bsz, seq, ATTN_DIM), BF16),
        grid=(bsz, seq // tq),
        in_specs=[qblk(ATTN_DIM), kblk(ATTN_DIM), kblk(ATTN_DIM), qblk(IDX_HEADS * IDX_DIM), kblk(LANES), kblk(LANES),
                  pl.BlockSpec((IDX_HEADS, tq), lambda b, j: (0, b * (seq // tq) + j))],
        out_specs=qblk(ATTN_DIM),
        scratch_shapes=[
            pltpu.VMEM((seq // kc, kc, tq), F32),
            pltpu.VMEM((ATTN_HEADS, tq, LANES), BF16),
            pltpu.VMEM((ATTN_HEADS, tq, LANES), F32),
            pltpu.VMEM((ATTN_HEADS, tq, LANES), F32),
            pltpu.VMEM((ATTN_HEADS, tq, LANES), F32),
        ],
        compiler_params=_params("parallel", "arbitrary"),
        name="dsa_attention",
    )(q, k, v, qi, kilo, kihi, wit)
    return out.reshape(bsz * seq, ATTN_DIM)


def _outproj_kernel(conv_ref, attn_ref, x_ref, gate1_ref, sc_ref, sh_ref, g2_ref, wout_ref, wr_ref, rbias_ref,
                    x1_o, h2_o, h2pa_o, h2pb_o, eidx_o, g8_o):
    mix = _dot(conv_ref[...], wout_ref[:CONV_DIM, :]) + _dot(attn_ref[...], wout_ref[CONV_DIM:, :])
    x1 = x_ref[...] + gate1_ref[0] * mix
    x1_o[...] = x1
    h2 = _rms_mod(x1, g2_ref[...], sc_ref[0], sh_ref[0])
    h2b = h2.astype(BF16)
    h2_o[...] = h2b
    quarter = h2.shape[1] // 4
    bits = lax.bitcast_convert_type(h2b.astype(F32), jnp.int32)
    for half_o, c0 in ((h2pa_o, 0), (h2pb_o, 2 * quarter)):
        low = lax.shift_right_logical(bits[:, c0:c0 + quarter], 16)
        half_o[...] = low | (bits[:, c0 + quarter:c0 + 2 * quarter] & jnp.int32(-65536))

    scores = jax.nn.sigmoid(_dot3(h2, wr_ref[...]))
    work = scores + rbias_ref[...]
    lane = lax.broadcasted_iota(jnp.int32, work.shape, 1).astype(F32)
    slot = lax.broadcasted_iota(jnp.int32, (work.shape[0], TOP_K_EXPERTS), 1)
    eidx = jnp.zeros(slot.shape, F32)
    picked = jnp.zeros(slot.shape, F32)
    for r in range(TOP_K_EXPERTS):
        mx = jnp.max(work, axis=1, keepdims=True)
        first = jnp.min(jnp.where(work == mx, lane, float(N_EXPERTS)), axis=1, keepdims=True)
        onehot = lane == first
        score_r = jnp.sum(jnp.where(onehot, scores, 0.0), axis=1, keepdims=True)
        eidx = jnp.where(slot == r, first, eidx)
        picked = jnp.where(slot == r, score_r, picked)
        work = jnp.where(onehot, -jnp.inf, work)
    eidx_o[...] = eidx.astype(jnp.int32)
    g8_o[...] = picked / jnp.sum(picked, axis=1, keepdims=True) * ROUTED_SCALE


def _outproj(conv, attn, x2, gate1, scale2, shift2, norm2_g, w_out, w_router, router_bias, seq, tm):
    t, d = x2.shape
    e = w_router.shape[1]
    per_b = seq // tm
    row = lambda w: pl.BlockSpec((tm, w), lambda i: (i, 0))
    full = lambda a: pl.BlockSpec(a.shape, lambda i: (0,) * a.ndim)
    mod = pl.BlockSpec((1, 1, d), lambda i: (i // per_b, 0, 0))
    g2 = norm2_g.reshape(1, d)
    wo = w_out.astype(BF16)
    rbias = router_bias.reshape(1, e)
    return pl.pallas_call(
        _outproj_kernel,
        out_shape=(jax.ShapeDtypeStruct((t, d), F32), jax.ShapeDtypeStruct((t, d), BF16),
                   jax.ShapeDtypeStruct((t, d // 4), jnp.int32), jax.ShapeDtypeStruct((t, d // 4), jnp.int32),
                   jax.ShapeDtypeStruct((t, TOP_K_EXPERTS), jnp.int32), jax.ShapeDtypeStruct((t, TOP_K_EXPERTS), F32)),
        grid=(t // tm,),
        in_specs=[row(CONV_DIM), row(ATTN_DIM), row(d), mod, mod, mod, full(g2), full(wo), full(w_router), full(rbias)],
        out_specs=(row(d), row(d), row(d // 4), row(d // 4), row(TOP_K_EXPERTS), row(TOP_K_EXPERTS)),
        compiler_params=_params("parallel"),
        name="outproj_router",
    )(conv, attn, x2, gate1, scale2, shift2, g2, wo, w_router, rbias)


def _route_kernel(eidx_ref, dest_o, cnt_o, cnt_ref, run_ref, *, bm):
    phase = pl.program_id(0)
    i = pl.program_id(1)
    tm = eidx_ref.shape[0]
    e8 = eidx_ref[...]
    lane = lax.broadcasted_iota(jnp.int32, (tm, N_EXPERTS), 1)
    hits = [lane == e8[:, k:k + 1] for k in range(TOP_K_EXPERTS)]
    member = functools.reduce(jnp.add, [jnp.where(hit, 1.0, 0.0) for hit in hits])
    tile_cnt = jnp.sum(member, axis=0, keepdims=True)

    @pl.when(phase == 0)
    def _():
        @pl.when(i == 0)
        def _():
            cnt_ref[...] = jnp.zeros(cnt_ref.shape, F32)

        cnt_ref[...] += tile_cnt

    @pl.when(phase == 1)
    def _():
        @pl.when(i == 0)
        def _():
            blocks = jnp.ceil(cnt_ref[...] / bm)
            r = lax.broadcasted_iota(jnp.int32, (N_EXPERTS, N_EXPERTS), 0)
            c = lax.broadcasted_iota(jnp.int32, (N_EXPERTS, N_EXPERTS), 1)
            before = jnp.where(r < c, 1.0, 0.0).astype(BF16)
            b_hi, b_lo = _split(blocks)
            run_ref[...] = (_dot(b_hi, before) + _dot(b_lo, before)) * bm

        r = lax.broadcasted_iota(jnp.int32, (tm, tm), 0)
        c = lax.broadcasted_iota(jnp.int32, (tm, tm), 1)
        earlier = jnp.where(c < r, 1.0, 0.0).astype(BF16)
        base = run_ref[0:1, :] + _dot(earlier, member.astype(BF16))
        slot = lax.broadcasted_iota(jnp.int32, (tm, TOP_K_EXPERTS), 1)
        dest = jnp.zeros((tm, TOP_K_EXPERTS), F32)
        for k in range(TOP_K_EXPERTS):
            dest = jnp.where(slot == k, jnp.sum(jnp.where(hits[k], base, 0.0), axis=1, keepdims=True), dest)
        dest_o[...] = dest.astype(jnp.int32)
        run_ref[...] += tile_cnt

    cnt_o[...] = cnt_ref[...]


def _route(eidx, bm, tm, tok0, nt):
    i0 = tok0 // tm
    return pl.pallas_call(
        functools.partial(_route_kernel, bm=bm),
        out_shape=(jax.ShapeDtypeStruct((nt, TOP_K_EXPERTS), jnp.int32),
                   jax.ShapeDtypeStruct((SUBLANES, N_EXPERTS), F32)),
        grid=(2, nt // tm),
        in_specs=[pl.BlockSpec((tm, TOP_K_EXPERTS), lambda p, i: (i + i0, 0))],
        out_specs=(pl.BlockSpec((tm, TOP_K_EXPERTS), lambda p, i: (p * i, 0)),
                   pl.BlockSpec((SUBLANES, N_EXPERTS), lambda p, i: (0, 0))),
        scratch_shapes=[pltpu.VMEM((SUBLANES, N_EXPERTS), F32), pltpu.VMEM((SUBLANES, N_EXPERTS), F32)],
        compiler_params=_params("arbitrary", "arbitrary"),
        name="moe_route",
    )(eidx)


def _sc_mesh():
    return plsc.VectorSubcoreMesh(core_axis_name="core", subcore_axis_name="subcore", num_cores=V7X_SC_CORES,
                                  num_subcores=V7X_SC_SUBCORES)


def _sc_scatter_rows(rows, dest_kt, n_out, tok0):
    d = rows.shape[1]
    n_k, t = dest_kt.shape
    window = SC_WINDOW
    i0 = tok0 // window

    @functools.partial(pl.kernel, out_type=jax.ShapeDtypeStruct((n_out, d), rows.dtype), mesh=_sc_mesh(),
                       name="moe_dispatch_scatter")
    def scatter(x_hbm, i_hbm, o_hbm):
        def body(x_vmem, i_vmem):
            for k in range(n_k):
                pltpu.sync_copy(x_vmem, o_hbm.at[i_vmem.at[k]])

        pltpu.emit_pipeline(
            body,
            grid=(t // window,),
            in_specs=[pl.BlockSpec((window, d), lambda i: (i + i0, 0)),
                      pl.BlockSpec((n_k, window), lambda i: (0, i))],
            out_specs=[],
            core_axis_name=("core", "subcore"),
            dimension_semantics=(pltpu.PARALLEL,),
        )(x_hbm, i_hbm)

    return scatter(rows, dest_kt)


def _sc_gather_rows(table, idx):
    n = idx.shape[0]
    d = table.shape[1]
    window = SC_WINDOW

    @functools.partial(pl.kernel, out_type=jax.ShapeDtypeStruct((n, d), table.dtype), mesh=_sc_mesh(),
                       name="moe_combine_gather")
    def gather(tab_hbm, i_hbm, o_hbm):
        def body(i_vmem, o_vmem):
            pltpu.sync_copy(tab_hbm.at[i_vmem.at[0]], o_vmem)

        pltpu.emit_pipeline(
            body,
            grid=(n // window,),
            in_specs=[pl.BlockSpec((1, window), lambda i: (0, i))],
            out_specs=[pl.BlockSpec((window, d), lambda i: (i, 0))],
            core_axis_name=("core", "subcore"),
            dimension_semantics=(pltpu.PARALLEL,),
        )(i_hbm, o_hbm)

    return gather(table, idx.reshape(1, n))


def _ffn_kernel(be_ref, nused_ref, xa_ref, xb_ref, w1_ref, w3_ref, w2_ref, *refs):
    ys_refs, (w13_s, w2_s) = refs[:-2], refs[-2:]
    b = pl.program_id(0)
    f = w2_ref.shape[1]

    @pl.when((b == 0) | (be_ref[b] != be_ref[jnp.maximum(b - 1, 0)]))
    def _():
        w13_s[:, :f] = w1_ref[0].astype(BF16)
        w13_s[:, f:] = w3_ref[0].astype(BF16)
        w2_s[...] = w2_ref[0].astype(BF16)

    @pl.when(b < nused_ref[0])
    def _():
        cols = []
        for x_ref in (xa_ref, xb_ref):
            xu = x_ref[...]
            cols.append(lax.bitcast_convert_type(lax.shift_left(xu, 16), F32).astype(BF16))
            cols.append(lax.bitcast_convert_type(xu & jnp.int32(-65536), F32).astype(BF16))
        hh = _dot(jnp.concatenate(cols, axis=1), w13_s[...])
        act = _silu(hh[:, :f]) * hh[:, f:]
        y = _dot(act.astype(BF16), w2_s[...])
        bits = lax.bitcast_convert_type(y.astype(BF16).astype(F32), jnp.int32)
        for j, y_ref in enumerate(ys_refs):
            c0 = 2 * j * SC_ROW_WORDS
            low = lax.shift_right_logical(bits[:, c0:c0 + SC_ROW_WORDS], 16)
            y_ref[...] = low | (bits[:, c0 + SC_ROW_WORDS:c0 + 2 * SC_ROW_WORDS] & jnp.int32(-65536))


def _ffn(block_e, n_used, xs_a, xs_b, w1, w3, w2, bm):
    rows, q = xs_a.shape
    _, d, f = w1.shape
    n_out = d // (2 * SC_ROW_WORDS)
    grid_spec = pltpu.PrefetchScalarGridSpec(
        num_scalar_prefetch=2,
        grid=(rows // bm,),
        in_specs=[pl.BlockSpec((bm, q), lambda b, be, nu: (b, 0)),
                  pl.BlockSpec((bm, q), lambda b, be, nu: (b, 0)),
                  pl.BlockSpec((1, d, f), lambda b, be, nu: (be[b], 0, 0)),
                  pl.BlockSpec((1, d, f), lambda b, be, nu: (be[b], 0, 0)),
                  pl.BlockSpec((1, f, d), lambda b, be, nu: (be[b], 0, 0))],
        out_specs=tuple(pl.BlockSpec((bm, SC_ROW_WORDS), lambda b, be, nu: (b, 0)) for _ in range(n_out)),
        scratch_shapes=[pltpu.VMEM((d, 2 * f), BF16), pltpu.VMEM((f, d), BF16)],
    )
    return pl.pallas_call(
        _ffn_kernel,
        out_shape=tuple(jax.ShapeDtypeStruct((rows, SC_ROW_WORDS), jnp.int32) for _ in range(n_out)),
        grid_spec=grid_spec,
        compiler_params=_params("arbitrary"),
        name="moe_expert_ffn",
    )(block_e, n_used, xs_a, xs_b, w1, w3, w2)


def _combine_kernel(*refs, n_y):
    y_refs = refs[:n_y]
    g8_ref, h2_ref, x1_ref, gate2_ref, ws13_ref, ws2_ref = refs[n_y:n_y + 6]
    o_ref = refs[-1]
    f = ws2_ref.shape[0]
    hs = _dot(h2_ref[...], ws13_ref[...])
    acc = _dot((_silu(hs[:, :f]) * hs[:, f:]).astype(BF16), ws2_ref[...])
    g8 = g8_ref[...]
    for k in range(TOP_K_EXPERTS):
        cols = []
        for y_ref in y_refs:
            yu = y_ref[k]
            cols.append(lax.bitcast_convert_type(lax.shift_left(yu, 16), F32))
            cols.append(lax.bitcast_convert_type(yu & jnp.int32(-65536), F32))
        acc = acc + g8[:, k:k + 1] * jnp.concatenate(cols, axis=1)
    o_ref[...] = x1_ref[...] + gate2_ref[0] * acc


def _combine(y8s, g8, h2, x1, gate2, ws13, ws2b, seq, tm, tok0, out_prev):
    t, d = x1.shape
    nt = y8s[0].shape[1]
    per_b = seq // tm
    i0 = tok0 // tm
    row = lambda w: pl.BlockSpec((tm, w), lambda i: (i + i0, 0))
    full = lambda a: pl.BlockSpec(a.shape, lambda i: (0,) * a.ndim)
    y_spec = pl.BlockSpec((TOP_K_EXPERTS, tm, SC_ROW_WORDS), lambda i: (0, i, 0))
    in_specs = [y_spec] * len(y8s) + [row(TOP_K_EXPERTS), row(d), row(d),
                                      pl.BlockSpec((1, 1, d), lambda i: ((i + i0) // per_b, 0, 0)), full(ws13), full(ws2b)]
    args = [*y8s, g8, h2, x1, gate2, ws13, ws2b]
    aliases = {}
    if out_prev is not None:
        in_specs.append(pl.BlockSpec(memory_space=pl.ANY))
        aliases = {len(args): 0}
        args.append(out_prev)
    return pl.pallas_call(
        functools.partial(_combine_kernel, n_y=len(y8s)),
        out_shape=jax.ShapeDtypeStruct((t, d), F32),
        grid=(nt // tm,),
        in_specs=in_specs,
        out_specs=row(d),
        input_output_aliases=aliases,
        compiler_params=_params("parallel"),
        name="moe_combine",
    )(*args)


def _moe(h2, h2pa, h2pb, x1, eidx, g8, gate2, w1, w3, w2, ws1, ws3, ws2, seq):
    t, d = x1.shape
    n_e = w1.shape[0]
    bm = MOE_BLOCK_ROWS
    ws13 = jnp.concatenate([ws1, ws3], axis=1).astype(BF16)
    ws2b = ws2.astype(BF16)

    nt = t // MOE_TOKEN_GROUPS
    rows = nt * TOP_K_EXPERTS + n_e * bm
    out = None
    for grp in range(MOE_TOKEN_GROUPS):
        tok0 = grp * nt
        dest, cnt = _route(eidx, bm, min(512, nt), tok0, nt)
        pend = jnp.cumsum(jnp.ceil(cnt[0] / bm) * bm)
        block_row0 = jnp.arange(rows // bm, dtype=F32) * bm
        block_e = jnp.minimum(jnp.sum(pend[None, :] <= block_row0[:, None], axis=1), n_e - 1)
        n_used = (pend[-1:] / bm).astype(jnp.int32)
        dest_kt = dest.T

        xs_a = _sc_scatter_rows(h2pa, dest_kt, rows, tok0)
        xs_b = _sc_scatter_rows(h2pb, dest_kt, rows, tok0)
        ys = _ffn(block_e.astype(jnp.int32), n_used, xs_a, xs_b, w1, w3, w2, bm)
        pair_rows = dest_kt.reshape(-1)
        y8s = [_sc_gather_rows(y, pair_rows).reshape(TOP_K_EXPERTS, nt, SC_ROW_WORDS) for y in ys]
        out = _combine(y8s, g8, h2, x1, gate2, ws13, ws2b, seq, min(256, seq), tok0, out)
    return out


def _layer(x, c, positions, norm1_g, norm2_g, w_ada, b_ada, w_in, conv_w, q_norm_g, k_norm_g, kidx_norm_g, w_out,
           w_router, router_bias, w1, w3, w2, ws1, ws3, ws2):
    bsz, seq, d = x.shape
    t = bsz * seq
    tm = min(512, seq)
    tq = min(512, seq)
    x2 = x.reshape(t, d)

    ada = _ada(c, w_ada, b_ada)
    shift1, scale1, gate1, shift2, scale2, gate2 = [a.reshape(bsz, 1, d) for a in jnp.split(ada, 6, axis=-1)]

    inv_freq = ROPE_THETA ** (-jnp.arange(ROPE_HALF, dtype=F32) / ROPE_HALF)
    invf_lane = inv_freq[(jnp.arange(LANES) % HEAD_DIM) % ROPE_HALF].reshape(1, LANES)
    ct, s1, s2 = _rope_tables(positions.reshape(t, 1), invf_lane, min(2048, t))

    conv, q, k, v, qi, kilo, kihi, wit = _inproj(x2, scale1, shift1, norm1_g, w_in, conv_w, q_norm_g, k_norm_g,
                                                kidx_norm_g, ct, s1, s2, seq, tm)
    attn = _attention(q, k, v, qi, kilo, kihi, wit, bsz, seq, tq, kc=tq)
    x1, h2, h2pa, h2pb, eidx, g8 = _outproj(conv, attn, x2, gate1, scale2, shift2, norm2_g, w_out, w_router,
                                            router_bias, seq, tm)
    out = _moe(h2, h2pa, h2pb, x1, eidx, g8, gate2, w1, w3, w2, ws1, ws3, ws2, seq)
    return out.reshape(bsz, seq, d)


def kernel(x, c, positions, norm1_g, norm2_g, w_ada, b_ada, w_in, conv_w, q_norm_g, k_norm_g, kidx_norm_g, w_out,
           w_router, router_bias, w1, w3, w2, ws1, ws3, ws2):
    for l in range(w_in.shape[0]):
        x = _layer(x, c, positions, norm1_g[l], norm2_g[l], w_ada[l], b_ada[l], w_in[l], conv_w[l], q_norm_g[l],
                   k_norm_g[l], kidx_norm_g[l], w_out[l], w_router[l], router_bias[l], w1[l], w3[l], w2[l], ws1[l],
                   ws3[l], ws2[l])
    return x
```

```python
import functools

import jax
import jax.numpy as jnp
from jax import lax
from jax.experimental import pallas as pl
from jax.experimental.pallas import tpu as pltpu
from jax.experimental.pallas import tpu_sc as plsc

F32 = jnp.float32
BF16 = jnp.bfloat16

HEAD_DIM = 64
ATTN_HEADS = 8
ATTN_DIM = ATTN_HEADS * HEAD_DIM
CONV_DIM = 512
IDX_HEADS = 8
IDX_DIM = 64
IDX_SCALE = (IDX_DIM ** -0.5) * (IDX_HEADS ** -0.5)
TOPK_KEYS_MAX = 256
ROPE_THETA = 500000.0
ROPE_DIM = HEAD_DIM // 4
ROPE_HALF = ROPE_DIM // 2
N_EXPERTS = 64
TOP_K_EXPERTS = 8
EXPERT_DIM = 256
ROUTED_SCALE = 2.5
EPS = 1e-6

LANES = 128
SUBLANES = 8
V7X_VMEM_BYTES = 64 * 1024 * 1024
VMEM_LIMIT = V7X_VMEM_BYTES * 3 // 4
V7X_SC_CORES = 2
V7X_SC_SUBCORES = 16
SC_WINDOW = 128
SC_ROW_WORDS = 256

MASKED = -1e30
Q_SCALE = HEAD_DIM ** -0.5 * 1.4426950408889634
MOE_BLOCK_ROWS = 1024
MOE_TOKEN_GROUPS = 1


def _params(*semantics):
    return pltpu.CompilerParams(dimension_semantics=semantics, vmem_limit_bytes=VMEM_LIMIT)


def _dot(a, b):
    return jnp.dot(a, b, preferred_element_type=F32)


def _dot_t(a, b):
    return lax.dot_general(a, b, (((1,), (1,)), ((), ())), preferred_element_type=F32)


def _split(a):
    hi = a.astype(BF16)
    lo = (a - hi.astype(F32)).astype(BF16)
    return hi, lo


def _dot3(a, b):
    a_hi, a_lo = _split(a)
    b_hi, b_lo = _split(b)
    return _dot(a_hi, b_hi) + _dot(a_hi, b_lo) + _dot(a_lo, b_hi)


def _silu(v):
    return v * jax.nn.sigmoid(v)


def _rms_mod(xv, g, scale, shift):
    ms = jnp.mean(xv * xv, axis=-1, keepdims=True)
    y = xv * lax.rsqrt(ms + EPS)
    return (y * g) * (1.0 + scale) + shift


def _ada_kernel(c_ref, w_ref, b_ref, o_ref):
    o_ref[...] = _dot3(_silu(c_ref[...]), w_ref[...]) + b_ref[...]


def _ada(c, w_ada, b_ada):
    bsz, d = c.shape
    n = w_ada.shape[1]
    bn = n // 4
    return pl.pallas_call(
        _ada_kernel,
        out_shape=jax.ShapeDtypeStruct((bsz, n), F32),
        grid=(n // bn,),
        in_specs=[
            pl.BlockSpec((bsz, d), lambda i: (0, 0)),
            pl.BlockSpec((d, bn), lambda i: (0, i)),
            pl.BlockSpec((1, bn), lambda i: (0, i)),
        ],
        out_specs=pl.BlockSpec((bsz, bn), lambda i: (0, i)),
        compiler_params=_params("parallel"),
        name="ada",
    )(c, w_ada, b_ada.reshape(1, n))


def _rope_kernel(pos_ref, invf_ref, c_ref, s1_ref, s2_ref):
    ang = pos_ref[...].astype(F32) * invf_ref[...]
    d = lax.broadcasted_iota(jnp.int32, ang.shape, 1) & (HEAD_DIM - 1)
    cos = jnp.cos(ang)
    sin = jnp.sin(ang)
    c_ref[...] = jnp.where(d < ROPE_DIM, cos, 1.0)
    s1_ref[...] = jnp.where(d < ROPE_HALF, -sin, 0.0)
    s2_ref[...] = jnp.where(d < ROPE_HALF, 0.0, jnp.where(d < ROPE_DIM, sin, 0.0))


def _rope_tables(pos, invf_lane, tm):
    t = pos.shape[0]
    spec = pl.BlockSpec((tm, LANES), lambda i: (i, 0))
    shp = jax.ShapeDtypeStruct((t, LANES), F32)
    return pl.pallas_call(
        _rope_kernel,
        out_shape=(shp, shp, shp),
        grid=(t // tm,),
        in_specs=[pl.BlockSpec((tm, 1), lambda i: (i, 0)), pl.BlockSpec((1, LANES), lambda i: (0, 0))],
        out_specs=(spec, spec, spec),
        compiler_params=_params("parallel"),
        name="rope_tables",
    )(pos, invf_lane)


def _rope(y, c, s1, s2):
    return y * c + pltpu.roll(y, LANES - ROPE_HALF, 1) * s1 + pltpu.roll(y, ROPE_HALF, 1) * s2


def _head_rms(xb, avg):
    hi, lo = _split(xb * xb)
    ms = _dot(hi, avg) + _dot(lo, avg)
    return xb * lax.rsqrt(ms + EPS)


def _inproj_kernel(x_ref, xh_ref, sc_ref, sh_ref, g1_ref, wmix_ref, wq_ref, wk_ref, wv_ref, wqi_ref, wl_ref, wlt_ref,
                   cw_ref, qg_ref, kg_ref, kig_ref, ct_ref, s1_ref, s2_ref, avg_ref,
                   conv_o, q_o, k_o, v_o, qi_o, kilo_o, kihi_o, wit_o, *, seq):
    tm = x_ref.shape[0]
    scale = sc_ref[0]
    shift = sh_ref[0]
    g1 = g1_ref[...]
    h = _rms_mod(x_ref[...], g1, scale, shift).astype(BF16)
    hh = _rms_mod(xh_ref[...], g1, scale, shift).astype(BF16)

    mix = _dot(h, wmix_ref[...])
    mixh = _dot(hh, wmix_ref[...])
    u = mix[:, 2 * CONV_DIM:] * mix[:, :CONV_DIM]
    uh = mixh[:, 2 * CONV_DIM:] * mixh[:, :CONV_DIM]
    seq_start = (pl.program_id(0) * tm) % seq == 0
    uh = jnp.where(seq_start, 0.0, uh)
    ext = jnp.concatenate([uh, u], axis=0)
    u1 = pltpu.roll(ext, 1, 0)[SUBLANES:]
    u2 = pltpu.roll(ext, 2, 0)[SUBLANES:]
    conv = u2 * cw_ref[0:1, :] + u1 * cw_ref[1:2, :] + u * cw_ref[2:3, :]
    conv_o[...] = (mix[:, CONV_DIM:2 * CONV_DIM] * conv).astype(BF16)

    ct = ct_ref[...]
    s1 = s1_ref[...]
    s2 = s2_ref[...]
    avg = avg_ref[...]
    qf = _dot(h, wq_ref[...])
    kf = _dot(h, wk_ref[...])
    qif = _dot(h, wqi_ref[...])
    for p in range(ATTN_DIM // LANES):
        sl = slice(p * LANES, (p + 1) * LANES)
        qn = _rope(_head_rms(qf[:, sl], avg) * qg_ref[...], ct, s1, s2)
        q_o[:, sl] = (qn * Q_SCALE).astype(BF16)
        kn = _rope(_head_rms(kf[:, sl], avg) * kg_ref[...], ct, s1, s2)
        k_o[:, sl] = kn.astype(BF16)
        qi_o[:, sl] = _rope(qif[:, sl], ct, s1, s2).astype(BF16)
    v_o[...] = _dot(h, wv_ref[...]).astype(BF16)

    wit_o[...] = _dot_t(wlt_ref[...], h)[IDX_DIM:IDX_DIM + IDX_HEADS, :]
    last = _dot(h, wl_ref[...])
    lane = lax.broadcasted_iota(jnp.int32, last.shape, 1)
    is_key = lane < IDX_DIM
    kin = _head_rms(last, avg) * kig_ref[...]
    kin = _rope(kin, jnp.where(is_key, ct, 1.0), jnp.where(is_key, s1, 0.0), jnp.where(is_key, s2, 0.0))
    klo = jnp.where(is_key, kin, 0.0)
    kilo_o[...] = klo.astype(BF16)
    kihi_o[...] = pltpu.roll(klo, IDX_DIM, 1).astype(BF16)


def _inproj(x2, scale1, shift1, norm1_g, w_in, conv_w, q_norm_g, k_norm_g, kidx_norm_g, ct, s1, s2, seq, tm):
    t, d = x2.shape
    cuts = [0, 3 * CONV_DIM, 3 * CONV_DIM + ATTN_DIM, 3 * CONV_DIM + 2 * ATTN_DIM, 3 * CONV_DIM + 3 * ATTN_DIM,
            3 * CONV_DIM + 3 * ATTN_DIM + IDX_HEADS * IDX_DIM]
    wb = w_in.astype(BF16)
    wmix, wq, wk, wv, wqi = [wb[:, a:b] for a, b in zip(cuts[:-1], cuts[1:])]
    wl = wb[:, cuts[-1]:]
    wl = jnp.pad(wl, ((0, 0), (0, LANES - wl.shape[1])))
    wlt = wl.T
    ones = jnp.ones((1, LANES - IDX_DIM), F32)
    qg = jnp.tile(q_norm_g.reshape(1, HEAD_DIM), (1, 2))
    kg = jnp.tile(k_norm_g.reshape(1, HEAD_DIM), (1, 2))
    kig = jnp.concatenate([kidx_norm_g.reshape(1, IDX_DIM), ones], axis=1)
    blk = jnp.arange(LANES) // HEAD_DIM
    avg = jnp.where(blk[:, None] == blk[None, :], 1.0 / HEAD_DIM, 0.0).astype(BF16)

    bsz = t // seq
    per_b = seq // tm
    row = lambda w: pl.BlockSpec((tm, w), lambda i: (i, 0))
    full = lambda a: pl.BlockSpec(a.shape, lambda i: (0,) * a.ndim)
    mod = pl.BlockSpec((1, 1, d), lambda i: (i // per_b, 0, 0))
    halo = pl.BlockSpec((SUBLANES, d), lambda i: (jnp.maximum(i * (tm // SUBLANES) - 1, 0), 0))
    g1 = norm1_g.reshape(1, d)
    out_shape = (
        jax.ShapeDtypeStruct((t, CONV_DIM), BF16),
        jax.ShapeDtypeStruct((t, ATTN_DIM), BF16),
        jax.ShapeDtypeStruct((t, ATTN_DIM), BF16),
        jax.ShapeDtypeStruct((t, ATTN_DIM), BF16),
        jax.ShapeDtypeStruct((t, IDX_HEADS * IDX_DIM), BF16),
        jax.ShapeDtypeStruct((t, LANES), BF16),
        jax.ShapeDtypeStruct((t, LANES), BF16),
        jax.ShapeDtypeStruct((IDX_HEADS, t), F32),
    )
    del bsz
    return pl.pallas_call(
        functools.partial(_inproj_kernel, seq=seq),
        out_shape=out_shape,
        grid=(t // tm,),
        in_specs=[row(d), halo, mod, mod, full(g1), full(wmix), full(wq), full(wk), full(wv), full(wqi), full(wl),
                  full(wlt), full(conv_w), full(qg), full(kg), full(kig), row(LANES), row(LANES), row(LANES), full(avg)],
        out_specs=(row(CONV_DIM), row(ATTN_DIM), row(ATTN_DIM), row(ATTN_DIM), row(IDX_HEADS * IDX_DIM),
                   row(LANES), row(LANES), pl.BlockSpec((IDX_HEADS, tm), lambda i: (0, i))),
        compiler_params=_params("parallel"),
        name="inproj",
    )(x2, x2, scale1, shift1, g1, wmix, wq, wk, wv, wqi, wl, wlt, conv_w, qg, kg, kig, ct, s1, s2, avg)


def _ukey_to_f32(u):
    s = u ^ jnp.int32(-2 ** 31)
    bits = s ^ ((s >> 31) & jnp.int32(0x7FFFFFFF))
    return lax.bitcast_convert_type(bits, F32)


def _attn_kernel(q_ref, k_ref, v_ref, qi_ref, kilo_ref, kihi_ref, wit_ref, o_ref,
                 sc_ref, sc16_ref, qm_ref, acc_ref, m_ref, l_ref, *, n_sel):
    tq = q_ref.shape[0]
    seq = k_ref.shape[0]
    kc = sc_ref.shape[1]
    assert kc == tq
    lb_n = kc // LANES
    j = pl.program_id(1)
    nch = (j + 1) * (tq // kc)
    lane = lax.broadcasted_iota(jnp.int32, (tq, LANES), 1)

    for h in range(ATTN_HEADS):
        qp = q_ref[:, (h // 2) * LANES:(h // 2 + 1) * LANES].astype(F32)
        keep = (lane < HEAD_DIM) if h % 2 == 0 else (lane >= HEAD_DIM)
        qm_ref[h] = jnp.where(keep, qp, 0.0).astype(BF16)

    q_pos = j * tq + lax.broadcasted_iota(jnp.int32, (kc, tq), 1)
    k_off = lax.broadcasted_iota(jnp.int32, (kc, tq), 0)
    w_rows = wit_ref[...]

    def index_chunk(c, carry):
        r0 = pl.multiple_of(c * kc, kc)
        klo = kilo_ref[pl.ds(r0, kc), :]
        khi = kihi_ref[pl.ds(r0, kc), :]
        acc = jnp.zeros((kc, tq), F32)
        for p in range(IDX_HEADS // 2):
            qip = qi_ref[:, p * LANES:(p + 1) * LANES]
            acc = acc + jnp.maximum(_dot_t(klo, qip), 0.0) * w_rows[2 * p:2 * p + 1, :]
            acc = acc + jnp.maximum(_dot_t(khi, qip), 0.0) * w_rows[2 * p + 1:2 * p + 2, :]
        score = acc * IDX_SCALE
        score = jnp.where(r0 + k_off <= q_pos, score, -jnp.inf)
        score = jnp.where(score == 0.0, 0.0, score)
        sc_ref[c] = score
        top = lax.bitcast_convert_type(score, jnp.int32) & jnp.int32(-65536)
        sc16_ref[c] = lax.bitcast_convert_type(top, F32).astype(BF16)
        return carry

    lax.fori_loop(0, nch, index_chunk, 0)

    t_q = j * tq + lax.broadcasted_iota(jnp.int32, (1, tq), 1)
    k_row = jnp.minimum(t_q + 1, n_sel).astype(F32)
    acc_rows = 4 * SUBLANES
    sub = lax.broadcasted_iota(jnp.int32, (acc_rows, tq), 0)
    idx_bits = (seq - 1).bit_length()

    def count(pred):
        def chunk(c, acc):
            blk = sc_ref[c]
            for g in range(kc // acc_rows):
                kidx = sub + (c * kc + g * acc_rows)
                acc = acc + jnp.where(pred(blk[g * acc_rows:(g + 1) * acc_rows, :], kidx), 1.0, 0.0)
            return acc

        acc = lax.fori_loop(0, nch, chunk, jnp.zeros((acc_rows, tq), F32))
        return jnp.sum(acc, axis=0, keepdims=True)

    def rows8(v):
        return jnp.broadcast_to(v, (acc_rows, tq))

    def count16(cand16):
        def chunk(c, acc):
            blk = sc16_ref[c]
            for g in range(kc // acc_rows):
                hit = blk[g * acc_rows:(g + 1) * acc_rows, :] >= cand16
                acc = acc + jnp.where(hit, jnp.ones_like(cand16), jnp.zeros_like(cand16))
            return acc

        acc = lax.fori_loop(0, nch, chunk, jnp.zeros((acc_rows, tq), BF16))
        return jnp.sum(acc.astype(F32), axis=0, keepdims=True)

    def coarse_bit(i, p):
        cand = p | jnp.left_shift(jnp.int32(1), 31 - i)
        top = lax.bitcast_convert_type(_ukey_to_f32(cand), jnp.int32) & jnp.int32(-65536)
        cand16 = rows8(lax.bitcast_convert_type(top, F32)).astype(BF16)
        return jnp.where(count16(cand16) >= k_row, cand, p)

    def value_bit(i, p):
        cand = p | jnp.left_shift(jnp.int32(1), 31 - i)
        cand_f = rows8(_ukey_to_f32(cand))
        cnt = count(lambda v, kidx: v >= cand_f)
        return jnp.where(cnt >= k_row, cand, p)

    p_hi = lax.fori_loop(0, 16, coarse_bit, jnp.zeros((1, tq), jnp.int32))
    thr = _ukey_to_f32(lax.fori_loop(16, 32, value_bit, p_hi))
    thr8 = rows8(thr)
    n_ge = count(lambda v, kidx: v >= thr8)
    tied = jnp.max(jnp.where(n_ge > k_row, 1.0, 0.0)) > 0.0

    def tie_cut():
        need = k_row - count(lambda v, kidx: v > thr8)

        def index_bit(i, p):
            cand = p | jnp.left_shift(jnp.int32(1), idx_bits - 1 - i)
            cand8 = rows8(cand)
            cnt = count(lambda v, kidx: jnp.where(v == thr8, kidx, seq) < cand8)
            return jnp.where(cnt < need, cand, p)

        return lax.fori_loop(0, idx_bits, index_bit, jnp.zeros((1, tq), jnp.int32))

    cut = lax.cond(tied, tie_cut, lambda: jnp.full((1, tq), seq, jnp.int32))

    def write_bias(c, carry):
        blk = sc_ref[c]
        tie_bias = jnp.where(c * kc + k_off <= cut, 0.0, MASKED)
        bias_t = jnp.where(blk > thr, 0.0, jnp.where(blk == thr, tie_bias, MASKED))
        sc_ref[c] = bias_t.T
        return carry

    lax.fori_loop(0, nch, write_bias, 0)

    for h in range(ATTN_HEADS):
        m_ref[h] = jnp.full((tq, LANES), MASKED, F32)
        l_ref[h] = jnp.zeros((tq, LANES), F32)
        acc_ref[h] = jnp.zeros((tq, LANES), F32)

    def attend(c, carry):
        r0 = pl.multiple_of(c * kc, kc)
        keys = pl.ds(r0, kc)
        bias = sc_ref[c]

        def qk(h):
            return _dot_t(qm_ref[h], k_ref[keys, (h // 2) * LANES:(h // 2 + 1) * LANES])

        s_next = qk(0)
        for h in range(ATTN_HEADS):
            s = s_next + bias
            if h + 1 < ATTN_HEADS:
                s_next = qk(h + 1)
            parts = [s[:, b * LANES:(b + 1) * LANES] for b in range(lb_n)]
            m_old = m_ref[h]
            row_max = jnp.max(functools.reduce(jnp.maximum, parts), axis=1, keepdims=True)
            m_new = jnp.maximum(m_old, row_max)
            alpha = jnp.exp2(m_old - m_new)
            p_parts = [jnp.exp2(part - m_new) for part in parts]
            l_ref[h] = alpha * l_ref[h] + functools.reduce(jnp.add, p_parts)
            p = jnp.concatenate(p_parts, axis=1).astype(BF16)
            acc_ref[h] = alpha * acc_ref[h] + _dot(p, v_ref[keys, (h // 2) * LANES:(h // 2 + 1) * LANES])
            m_ref[h] = m_new
        return carry

    lax.fori_loop(0, nch, attend, 0)
    for pair in range(ATTN_HEADS // 2):
        l_even = jnp.sum(l_ref[2 * pair], axis=1, keepdims=True)
        l_odd = jnp.sum(l_ref[2 * pair + 1], axis=1, keepdims=True)
        o_pair = jnp.where(lane < HEAD_DIM, acc_ref[2 * pair] / l_even, acc_ref[2 * pair + 1] / l_odd)
        o_ref[:, pair * LANES:(pair + 1) * LANES] = o_pair.astype(BF16)


def _attention(q, k, v, qi, kilo, kihi, wit, bsz, seq, tq, kc):
    n_sel = min(TOPK_KEYS_MAX, seq // 4)
    shape3 = lambda a: a.reshape(bsz, seq, a.shape[-1])
    q, k, v, qi, kilo, kihi = map(shape3, (q, k, v, qi, kilo, kihi))
    qblk = lambda w: pl.BlockSpec((None, tq, w), lambda b, j: (b, j, 0))
    kblk = lambda w: pl.BlockSpec((None, seq, w), lambda b, j: (b, 0, 0), pipeline_mode=pl.Buffered(1))
    out = pl.pallas_call(
        functools.partial(_attn_kernel, n_sel=n_sel),
        out_shape=jax.ShapeDtypeStruct((bsz, seq, ATTN_DIM), BF16),
        grid=(bsz, seq // tq),
        in_specs=[qblk(ATTN_DIM), kblk(ATTN_DIM), kblk(ATTN_DIM), qblk(IDX_HEADS * IDX_DIM), kblk(LANES), kblk(LANES),
                  pl.BlockSpec((IDX_HEADS, tq), lambda b, j: (0, b * (seq // tq) + j))],
        out_specs=qblk(ATTN_DIM),
        scratch_shapes=[
            pltpu.VMEM((seq // kc, kc, tq), F32),
            pltpu.VMEM((seq // kc, kc, tq), BF16),
            pltpu.VMEM((ATTN_HEADS, tq, LANES), BF16),
            pltpu.VMEM((ATTN_HEADS, tq, LANES), F32),
            pltpu.VMEM((ATTN_HEADS, tq, LANES), F32),
            pltpu.VMEM((ATTN_HEADS, tq, LANES), F32),
        ],
        compiler_params=_params("parallel", "arbitrary"),
        name="dsa_attention",
    )(q, k, v, qi, kilo, kihi, wit)
    return out.reshape(bsz * seq, ATTN_DIM)


def _outproj_kernel(conv_ref, attn_ref, x_ref, gate1_ref, sc_ref, sh_ref, g2_ref, wout_ref, wr_ref, rbias_ref,
                    x1_o, h2_o, h2pa_o, h2pb_o, eidx_o, g8_o):
    mix = _dot(conv_ref[...], wout_ref[:CONV_DIM, :]) + _dot(attn_ref[...], wout_ref[CONV_DIM:, :])
    x1 = x_ref[...] + gate1_ref[0] * mix
    x1_o[...] = x1
    h2 = _rms_mod(x1, g2_ref[...], sc_ref[0], sh_ref[0])
    h2b = h2.astype(BF16)
    h2_o[...] = h2b
    quarter = h2.shape[1] // 4
    bits = lax.bitcast_convert_type(h2b.astype(F32), jnp.int32)
    for half_o, c0 in ((h2pa_o, 0), (h2pb_o, 2 * quarter)):
        low = lax.shift_right_logical(bits[:, c0:c0 + quarter], 16)
        half_o[...] = low | (bits[:, c0 + quarter:c0 + 2 * quarter] & jnp.int32(-65536))

    scores = jax.nn.sigmoid(_dot3(h2, wr_ref[...]))
    work = scores + rbias_ref[...]
    lane = lax.broadcasted_iota(jnp.int32, work.shape, 1).astype(F32)
    slot = lax.broadcasted_iota(jnp.int32, (work.shape[0], TOP_K_EXPERTS), 1)
    eidx = jnp.zeros(slot.shape, F32)
    picked = jnp.zeros(slot.shape, F32)
    for r in range(TOP_K_EXPERTS):
        mx = jnp.max(work, axis=1, keepdims=True)
        first = jnp.min(jnp.where(work == mx, lane, float(N_EXPERTS)), axis=1, keepdims=True)
        onehot = lane == first
        score_r = jnp.sum(jnp.where(onehot, scores, 0.0), axis=1, keepdims=True)
        eidx = jnp.where(slot == r, first, eidx)
        picked = jnp.where(slot == r, score_r, picked)
        work = jnp.where(onehot, -jnp.inf, work)
    eidx_o[...] = eidx.astype(jnp.int32)
    g8_o[...] = picked / jnp.sum(picked, axis=1, keepdims=True) * ROUTED_SCALE


def _outproj(conv, attn, x2, gate1, scale2, shift2, norm2_g, w_out, w_router, router_bias, seq, tm):
    t, d = x2.shape
    e = w_router.shape[1]
    per_b = seq // tm
    row = lambda w: pl.BlockSpec((tm, w), lambda i: (i, 0))
    full = lambda a: pl.BlockSpec(a.shape, lambda i: (0,) * a.ndim)
    mod = pl.BlockSpec((1, 1, d), lambda i: (i // per_b, 0, 0))
    g2 = norm2_g.reshape(1, d)
    wo = w_out.astype(BF16)
    rbias = router_bias.reshape(1, e)
    return pl.pallas_call(
        _outproj_kernel,
        out_shape=(jax.ShapeDtypeStruct((t, d), F32), jax.ShapeDtypeStruct((t, d), BF16),
                   jax.ShapeDtypeStruct((t, d // 4), jnp.int32), jax.ShapeDtypeStruct((t, d // 4), jnp.int32),
                   jax.ShapeDtypeStruct((t, TOP_K_EXPERTS), jnp.int32), jax.ShapeDtypeStruct((t, TOP_K_EXPERTS), F32)),
        grid=(t // tm,),
        in_specs=[row(CONV_DIM), row(ATTN_DIM), row(d), mod, mod, mod, full(g2), full(wo), full(w_router), full(rbias)],
        out_specs=(row(d), row(d), row(d // 4), row(d // 4), row(TOP_K_EXPERTS), row(TOP_K_EXPERTS)),
        compiler_params=_params("parallel"),
        name="outproj_router",
    )(conv, attn, x2, gate1, scale2, shift2, g2, wo, w_router, rbias)


def _route_kernel(eidx_ref, dest_o, cnt_o, cnt_ref, run_ref, *, bm):
    phase = pl.program_id(0)
    i = pl.program_id(1)
    tm = eidx_ref.shape[0]
    e8 = eidx_ref[...]
    lane = lax.broadcasted_iota(jnp.int32, (tm, N_EXPERTS), 1)
    hits = [lane == e8[:, k:k + 1] for k in range(TOP_K_EXPERTS)]
    member = functools.reduce(jnp.add, [jnp.where(hit, 1.0, 0.0) for hit in hits])
    tile_cnt = jnp.sum(member, axis=0, keepdims=True)

    @pl.when(phase == 0)
    def _():
        @pl.when(i == 0)
        def _():
            cnt_ref[...] = jnp.zeros(cnt_ref.shape, F32)

        cnt_ref[...] += tile_cnt

    @pl.when(phase == 1)
    def _():
        @pl.when(i == 0)
        def _():
            blocks = jnp.ceil(cnt_ref[...] / bm)
            r = lax.broadcasted_iota(jnp.int32, (N_EXPERTS, N_EXPERTS), 0)
            c = lax.broadcasted_iota(jnp.int32, (N_EXPERTS, N_EXPERTS), 1)
            before = jnp.where(r < c, 1.0, 0.0).astype(BF16)
            b_hi, b_lo = _split(blocks)
            run_ref[...] = (_dot(b_hi, before) + _dot(b_lo, before)) * bm

        r = lax.broadcasted_iota(jnp.int32, (tm, tm), 0)
        c = lax.broadcasted_iota(jnp.int32, (tm, tm), 1)
        earlier = jnp.where(c < r, 1.0, 0.0).astype(BF16)
        base = run_ref[0:1, :] + _dot(earlier, member.astype(BF16))
        slot = lax.broadcasted_iota(jnp.int32, (tm, TOP_K_EXPERTS), 1)
        dest = jnp.zeros((tm, TOP_K_EXPERTS), F32)
        for k in range(TOP_K_EXPERTS):
            dest = jnp.where(slot == k, jnp.sum(jnp.where(hits[k], base, 0.0), axis=1, keepdims=True), dest)
        dest_o[...] = dest.astype(jnp.int32)
        run_ref[...] += tile_cnt

    cnt_o[...] = cnt_ref[...]


def _route(eidx, bm, tm, tok0, nt):
    i0 = tok0 // tm
    return pl.pallas_call(
        functools.partial(_route_kernel, bm=bm),
        out_shape=(jax.ShapeDtypeStruct((nt, TOP_K_EXPERTS), jnp.int32),
                   jax.ShapeDtypeStruct((SUBLANES, N_EXPERTS), F32)),
        grid=(2, nt // tm),
        in_specs=[pl.BlockSpec((tm, TOP_K_EXPERTS), lambda p, i: (i + i0, 0))],
        out_specs=(pl.BlockSpec((tm, TOP_K_EXPERTS), lambda p, i: (p * i, 0)),
                   pl.BlockSpec((SUBLANES, N_EXPERTS), lambda p, i: (0, 0))),
        scratch_shapes=[pltpu.VMEM((SUBLANES, N_EXPERTS), F32), pltpu.VMEM((SUBLANES, N_EXPERTS), F32)],
        compiler_params=_params("arbitrary", "arbitrary"),
        name="moe_route",
    )(eidx)


def _sc_mesh():
    return plsc.VectorSubcoreMesh(core_axis_name="core", subcore_axis_name="subcore", num_cores=V7X_SC_CORES,
                                  num_subcores=V7X_SC_SUBCORES)


def _sc_scatter_rows(rows, dest_kt, n_out, tok0):
    d = rows.shape[1]
    n_k, t = dest_kt.shape
    window = SC_WINDOW
    i0 = tok0 // window

    @functools.partial(pl.kernel, out_type=jax.ShapeDtypeStruct((n_out, d), rows.dtype), mesh=_sc_mesh(),
                       name="moe_dispatch_scatter")
    def scatter(x_hbm, i_hbm, o_hbm):
        def body(x_vmem, i_vmem):
            for k in range(n_k):
                pltpu.sync_copy(x_vmem, o_hbm.at[i_vmem.at[k]])

        pltpu.emit_pipeline(
            body,
            grid=(t // window,),
            in_specs=[pl.BlockSpec((window, d), lambda i: (i + i0, 0)),
                      pl.BlockSpec((n_k, window), lambda i: (0, i))],
            out_specs=[],
            core_axis_name=("core", "subcore"),
            dimension_semantics=(pltpu.PARALLEL,),
        )(x_hbm, i_hbm)

    return scatter(rows, dest_kt)


def _sc_gather_rows(table, idx):
    n = idx.shape[0]
    d = table.shape[1]
    window = SC_WINDOW

    @functools.partial(pl.kernel, out_type=jax.ShapeDtypeStruct((n, d), table.dtype), mesh=_sc_mesh(),
                       name="moe_combine_gather")
    def gather(tab_hbm, i_hbm, o_hbm):
        def body(i_vmem, o_vmem):
            pltpu.sync_copy(tab_hbm.at[i_vmem.at[0]], o_vmem)

        pltpu.emit_pipeline(
            body,
            grid=(n // window,),
            in_specs=[pl.BlockSpec((1, window), lambda i: (0, i))],
            out_specs=[pl.BlockSpec((window, d), lambda i: (i, 0))],
            core_axis_name=("core", "subcore"),
            dimension_semantics=(pltpu.PARALLEL,),
        )(i_hbm, o_hbm)

    return gather(table, idx.reshape(1, n))


def _ffn_kernel(be_ref, nused_ref, xa_ref, xb_ref, w1_ref, w3_ref, w2_ref, *refs):
    ys_refs, (w13_s, w2_s) = refs[:-2], refs[-2:]
    b = pl.program_id(0)
    f = w2_ref.shape[1]

    @pl.when((b == 0) | (be_ref[b] != be_ref[jnp.maximum(b - 1, 0)]))
    def _():
        w13_s[:, :f] = w1_ref[0].astype(BF16)
        w13_s[:, f:] = w3_ref[0].astype(BF16)
        w2_s[...] = w2_ref[0].astype(BF16)

    @pl.when(b < nused_ref[0])
    def _():
        cols = []
        for x_ref in (xa_ref, xb_ref):
            xu = x_ref[...]
            cols.append(lax.bitcast_convert_type(lax.shift_left(xu, 16), F32).astype(BF16))
            cols.append(lax.bitcast_convert_type(xu & jnp.int32(-65536), F32).astype(BF16))
        hh = _dot(jnp.concatenate(cols, axis=1), w13_s[...])
        act = _silu(hh[:, :f]) * hh[:, f:]
        y = _dot(act.astype(BF16), w2_s[...])
        bits = lax.bitcast_convert_type(y.astype(BF16).astype(F32), jnp.int32)
        for j, y_ref in enumerate(ys_refs):
            c0 = 2 * j * SC_ROW_WORDS
            low = lax.shift_right_logical(bits[:, c0:c0 + SC_ROW_WORDS], 16)
            y_ref[...] = low | (bits[:, c0 + SC_ROW_WORDS:c0 + 2 * SC_ROW_WORDS] & jnp.int32(-65536))


def _ffn(block_e, n_used, xs_a, xs_b, w1, w3, w2, bm):
    rows, q = xs_a.shape
    _, d, f = w1.shape
    n_out = d // (2 * SC_ROW_WORDS)
    grid_spec = pltpu.PrefetchScalarGridSpec(
        num_scalar_prefetch=2,
        grid=(rows // bm,),
        in_specs=[pl.BlockSpec((bm, q), lambda b, be, nu: (b, 0)),
                  pl.BlockSpec((bm, q), lambda b, be, nu: (b, 0)),
                  pl.BlockSpec((1, d, f), lambda b, be, nu: (be[b], 0, 0)),
                  pl.BlockSpec((1, d, f), lambda b, be, nu: (be[b], 0, 0)),
                  pl.BlockSpec((1, f, d), lambda b, be, nu: (be[b], 0, 0))],
        out_specs=tuple(pl.BlockSpec((bm, SC_ROW_WORDS), lambda b, be, nu: (b, 0)) for _ in range(n_out)),
        scratch_shapes=[pltpu.VMEM((d, 2 * f), BF16), pltpu.VMEM((f, d), BF16)],
    )
    return pl.pallas_call(
        _ffn_kernel,
        out_shape=tuple(jax.ShapeDtypeStruct((rows, SC_ROW_WORDS), jnp.int32) for _ in range(n_out)),
        grid_spec=grid_spec,
        compiler_params=_params("arbitrary"),
        name="moe_expert_ffn",
    )(block_e, n_used, xs_a, xs_b, w1, w3, w2)


def _combine_kernel(*refs, n_y):
    y_refs = refs[:n_y]
    g8_ref, h2_ref, x1_ref, gate2_ref, ws13_ref, ws2_ref = refs[n_y:n_y + 6]
    o_ref = refs[-1]
    f = ws2_ref.shape[0]
    hs = _dot(h2_ref[...], ws13_ref[...])
    acc = _dot((_silu(hs[:, :f]) * hs[:, f:]).astype(BF16), ws2_ref[...])
    g8 = g8_ref[...]
    for k in range(TOP_K_EXPERTS):
        cols = []
        for y_ref in y_refs:
            yu = y_ref[k]
            cols.append(lax.bitcast_convert_type(lax.shift_left(yu, 16), F32))
            cols.append(lax.bitcast_convert_type(yu & jnp.int32(-65536), F32))
        acc = acc + g8[:, k:k + 1] * jnp.concatenate(cols, axis=1)
    o_ref[...] = x1_ref[...] + gate2_ref[0] * acc


def _combine(y8s, g8, h2, x1, gate2, ws13, ws2b, seq, tm, tok0, out_prev):
    t, d = x1.shape
    nt = y8s[0].shape[1]
    per_b = seq // tm
    i0 = tok0 // tm
    row = lambda w: pl.BlockSpec((tm, w), lambda i: (i + i0, 0))
    full = lambda a: pl.BlockSpec(a.shape, lambda i: (0,) * a.ndim)
    y_spec = pl.BlockSpec((TOP_K_EXPERTS, tm, SC_ROW_WORDS), lambda i: (0, i, 0))
    in_specs = [y_spec] * len(y8s) + [row(TOP_K_EXPERTS), row(d), row(d),
                                      pl.BlockSpec((1, 1, d), lambda i: ((i + i0) // per_b, 0, 0)), full(ws13), full(ws2b)]
    args = [*y8s, g8, h2, x1, gate2, ws13, ws2b]
    aliases = {}
    if out_prev is not None:
        in_specs.append(pl.BlockSpec(memory_space=pl.ANY))
        aliases = {len(args): 0}
        args.append(out_prev)
    return pl.pallas_call(
        functools.partial(_combine_kernel, n_y=len(y8s)),
        out_shape=jax.ShapeDtypeStruct((t, d), F32),
        grid=(nt // tm,),
        in_specs=in_specs,
        out_specs=row(d),
        input_output_aliases=aliases,
        compiler_params=_params("parallel"),
        name="moe_combine",
    )(*args)


def _moe(h2, h2pa, h2pb, x1, eidx, g8, gate2, w1, w3, w2, ws1, ws3, ws2, seq):
    t, d = x1.shape
    n_e = w1.shape[0]
    bm = MOE_BLOCK_ROWS
    ws13 = jnp.concatenate([ws1, ws3], axis=1).astype(BF16)
    ws2b = ws2.astype(BF16)

    nt = t // MOE_TOKEN_GROUPS
    rows = nt * TOP_K_EXPERTS + n_e * bm
    out = None
    for grp in range(MOE_TOKEN_GROUPS):
        tok0 = grp * nt
        dest, cnt = _route(eidx, bm, min(512, nt), tok0, nt)
        pend = jnp.cumsum(jnp.ceil(cnt[0] / bm) * bm)
        block_row0 = jnp.arange(rows // bm, dtype=F32) * bm
        block_e = jnp.minimum(jnp.sum(pend[None, :] <= block_row0[:, None], axis=1), n_e - 1)
        n_used = (pend[-1:] / bm).astype(jnp.int32)
        dest_kt = dest.T

        xs_a = _sc_scatter_rows(h2pa, dest_kt, rows, tok0)
        xs_b = _sc_scatter_rows(h2pb, dest_kt, rows, tok0)
        ys = _ffn(block_e.astype(jnp.int32), n_used, xs_a, xs_b, w1, w3, w2, bm)
        pair_rows = dest_kt.reshape(-1)
        y8s = [_sc_gather_rows(y, pair_rows).reshape(TOP_K_EXPERTS, nt, SC_ROW_WORDS) for y in ys]
        out = _combine(y8s, g8, h2, x1, gate2, ws13, ws2b, seq, min(256, seq), tok0, out)
    return out


def _layer(x, c, positions, norm1_g, norm2_g, w_ada, b_ada, w_in, conv_w, q_norm_g, k_norm_g, kidx_norm_g, w_out,
           w_router, router_bias, w1, w3, w2, ws1, ws3, ws2):
    bsz, seq, d = x.shape
    t = bsz * seq
    tm = min(512, seq)
    tq = min(512, seq)
    x2 = x.reshape(t, d)

    ada = _ada(c, w_ada, b_ada)
    shift1, scale1, gate1, shift2, scale2, gate2 = [a.reshape(bsz, 1, d) for a in jnp.split(ada, 6, axis=-1)]

    inv_freq = ROPE_THETA ** (-jnp.arange(ROPE_HALF, dtype=F32) / ROPE_HALF)
    invf_lane = inv_freq[(jnp.arange(LANES) % HEAD_DIM) % ROPE_HALF].reshape(1, LANES)
    ct, s1, s2 = _rope_tables(positions.reshape(t, 1), invf_lane, min(2048, t))

    conv, q, k, v, qi, kilo, kihi, wit = _inproj(x2, scale1, shift1, norm1_g, w_in, conv_w, q_norm_g, k_norm_g,
                                                kidx_norm_g, ct, s1, s2, seq, tm)
    attn = _attention(q, k, v, qi, kilo, kihi, wit, bsz, seq, tq, kc=tq)
    x1, h2, h2pa, h2pb, eidx, g8 = _outproj(conv, attn, x2, gate1, scale2, shift2, norm2_g, w_out, w_router,
                                            router_bias, seq, tm)
    out = _moe(h2, h2pa, h2pb, x1, eidx, g8, gate2, w1, w3, w2, ws1, ws3, ws2, seq)
    return out.reshape(bsz, seq, d)


def kernel(x, c, positions, norm1_g, norm2_g, w_ada, b_ada, w_in, conv_w, q_norm_g, k_norm_g, kidx_norm_g, w_out,
           w_router, router_bias, w1, w3, w2, ws1, ws3, ws2):
    for l in range(w_in.shape[0]):
        x = _layer(x, c, positions, norm1_g[l], norm2_g[l], w_ada[l], b_ada[l], w_in[l], conv_w[l], q_norm_g[l],
                   k_norm_g[l], kidx_norm_g[l], w_out[l], w_router[l], router_bias[l], w1[l], w3[l], w2[l], ws1[l],
                   ws3[l], ws2[l])
    return x
```

```python
import functools

import jax
import jax.numpy as jnp
from jax import lax
from jax.experimental import pallas as pl
from jax.experimental.pallas import tpu as pltpu
from jax.experimental.pallas import tpu_sc as plsc

F32 = jnp.float32
BF16 = jnp.bfloat16

HEAD_DIM = 64
ATTN_HEADS = 8
ATTN_DIM = ATTN_HEADS * HEAD_DIM
CONV_DIM = 512
IDX_HEADS = 8
IDX_DIM = 64
IDX_SCALE = (IDX_DIM ** -0.5) * (IDX_HEADS ** -0.5)
TOPK_KEYS_MAX = 256
ROPE_THETA = 500000.0
ROPE_DIM = HEAD_DIM // 4
ROPE_HALF = ROPE_DIM // 2
N_EXPERTS = 64
TOP_K_EXPERTS = 8
EXPERT_DIM = 256
ROUTED_SCALE = 2.5
EPS = 1e-6

LANES = 128
SUBLANES = 8
V7X_VMEM_BYTES = 64 * 1024 * 1024
VMEM_LIMIT = V7X_VMEM_BYTES * 3 // 4
V7X_SC_CORES = 2
V7X_SC_SUBCORES = 16
SC_WINDOW = 128
SC_ROW_WORDS = 256

MASKED = -1e30
Q_SCALE = HEAD_DIM ** -0.5 * 1.4426950408889634
MOE_BLOCK_ROWS = 1024
MOE_TOKEN_GROUPS = 1


def _params(*semantics):
    return pltpu.CompilerParams(dimension_semantics=semantics, vmem_limit_bytes=VMEM_LIMIT)


def _dot(a, b):
    return jnp.dot(a, b, preferred_element_type=F32)


def _dot_t(a, b):
    return lax.dot_general(a, b, (((1,), (1,)), ((), ())), preferred_element_type=F32)


def _split(a):
    hi = a.astype(BF16)
    lo = (a - hi.astype(F32)).astype(BF16)
    return hi, lo


def _dot3(a, b):
    a_hi, a_lo = _split(a)
    b_hi, b_lo = _split(b)
    return _dot(a_hi, b_hi) + _dot(a_hi, b_lo) + _dot(a_lo, b_hi)


def _silu(v):
    return v * jax.nn.sigmoid(v)


def _rms_mod(xv, g, scale, shift):
    ms = jnp.mean(xv * xv, axis=-1, keepdims=True)
    y = xv * lax.rsqrt(ms + EPS)
    return (y * g) * (1.0 + scale) + shift


def _ada_kernel(c_ref, w_ref, b_ref, o_ref):
    o_ref[...] = _dot3(_silu(c_ref[...]), w_ref[...]) + b_ref[...]


def _ada(c, w_ada, b_ada):
    bsz, d = c.shape
    n = w_ada.shape[1]
    bn = n // 4
    return pl.pallas_call(
        _ada_kernel,
        out_shape=jax.ShapeDtypeStruct((bsz, n), F32),
        grid=(n // bn,),
        in_specs=[
            pl.BlockSpec((bsz, d), lambda i: (0, 0)),
            pl.BlockSpec((d, bn), lambda i: (0, i)),
            pl.BlockSpec((1, bn), lambda i: (0, i)),
        ],
        out_specs=pl.BlockSpec((bsz, bn), lambda i: (0, i)),
        compiler_params=_params("parallel"),
        name="ada",
    )(c, w_ada, b_ada.reshape(1, n))


def _rope_kernel(pos_ref, invf_ref, c_ref, s1_ref, s2_ref):
    ang = pos_ref[...].astype(F32) * invf_ref[...]
    d = lax.broadcasted_iota(jnp.int32, ang.shape, 1) & (HEAD_DIM - 1)
    cos = jnp.cos(ang)
    sin = jnp.sin(ang)
    c_ref[...] = jnp.where(d < ROPE_DIM, cos, 1.0)
    s1_ref[...] = jnp.where(d < ROPE_HALF, -sin, 0.0)
    s2_ref[...] = jnp.where(d < ROPE_HALF, 0.0, jnp.where(d < ROPE_DIM, sin, 0.0))


def _rope_tables(pos, invf_lane, tm):
    t = pos.shape[0]
    spec = pl.BlockSpec((tm, LANES), lambda i: (i, 0))
    shp = jax.ShapeDtypeStruct((t, LANES), F32)
    return pl.pallas_call(
        _rope_kernel,
        out_shape=(shp, shp, shp),
        grid=(t // tm,),
        in_specs=[pl.BlockSpec((tm, 1), lambda i: (i, 0)), pl.BlockSpec((1, LANES), lambda i: (0, 0))],
        out_specs=(spec, spec, spec),
        compiler_params=_params("parallel"),
        name="rope_tables",
    )(pos, invf_lane)


def _rope(y, c, s1, s2):
    return y * c + pltpu.roll(y, LANES - ROPE_HALF, 1) * s1 + pltpu.roll(y, ROPE_HALF, 1) * s2


def _head_rms(xb, avg):
    hi, lo = _split(xb * xb)
    ms = _dot(hi, avg) + _dot(lo, avg)
    return xb * lax.rsqrt(ms + EPS)


def _inproj_kernel(x_ref, xh_ref, sc_ref, sh_ref, g1_ref, wmix_ref, wq_ref, wk_ref, wv_ref, wqi_ref, wl_ref, wlt_ref,
                   cw_ref, qg_ref, kg_ref, kig_ref, ct_ref, s1_ref, s2_ref, avg_ref,
                   conv_o, q_o, k_o, v_o, qi_o, kilo_o, kihi_o, wit_o, *, seq):
    tm = x_ref.shape[0]
    scale = sc_ref[0]
    shift = sh_ref[0]
    g1 = g1_ref[...]
    h = _rms_mod(x_ref[...], g1, scale, shift).astype(BF16)
    hh = _rms_mod(xh_ref[...], g1, scale, shift).astype(BF16)

    mix = _dot(h, wmix_ref[...])
    mixh = _dot(hh, wmix_ref[...])
    u = mix[:, 2 * CONV_DIM:] * mix[:, :CONV_DIM]
    uh = mixh[:, 2 * CONV_DIM:] * mixh[:, :CONV_DIM]
    seq_start = (pl.program_id(0) * tm) % seq == 0
    uh = jnp.where(seq_start, 0.0, uh)
    ext = jnp.concatenate([uh, u], axis=0)
    u1 = pltpu.roll(ext, 1, 0)[SUBLANES:]
    u2 = pltpu.roll(ext, 2, 0)[SUBLANES:]
    conv = u2 * cw_ref[0:1, :] + u1 * cw_ref[1:2, :] + u * cw_ref[2:3, :]
    conv_o[...] = (mix[:, CONV_DIM:2 * CONV_DIM] * conv).astype(BF16)

    ct = ct_ref[...]
    s1 = s1_ref[...]
    s2 = s2_ref[...]
    avg = avg_ref[...]
    qf = _dot(h, wq_ref[...])
    kf = _dot(h, wk_ref[...])
    qif = _dot(h, wqi_ref[...])
    for p in range(ATTN_DIM // LANES):
        sl = slice(p * LANES, (p + 1) * LANES)
        qn = _rope(_head_rms(qf[:, sl], avg) * qg_ref[...], ct, s1, s2)
        q_o[:, sl] = (qn * Q_SCALE).astype(BF16)
        kn = _rope(_head_rms(kf[:, sl], avg) * kg_ref[...], ct, s1, s2)
        k_o[:, sl] = kn.astype(BF16)
        qi_o[:, sl] = _rope(qif[:, sl], ct, s1, s2).astype(BF16)
    v_o[...] = _dot(h, wv_ref[...]).astype(BF16)

    wit_o[...] = _dot_t(wlt_ref[...], h)[IDX_DIM:IDX_DIM + IDX_HEADS, :]
    last = _dot(h, wl_ref[...])
    lane = lax.broadcasted_iota(jnp.int32, last.shape, 1)
    is_key = lane < IDX_DIM
    kin = _head_rms(last, avg) * kig_ref[...]
    kin = _rope(kin, jnp.where(is_key, ct, 1.0), jnp.where(is_key, s1, 0.0), jnp.where(is_key, s2, 0.0))
    klo = jnp.where(is_key, kin, 0.0)
    kilo_o[...] = klo.astype(BF16)
    kihi_o[...] = pltpu.roll(klo, IDX_DIM, 1).astype(BF16)


def _inproj(x2, scale1, shift1, norm1_g, w_in, conv_w, q_norm_g, k_norm_g, kidx_norm_g, ct, s1, s2, seq, tm):
    t, d = x2.shape
    cuts = [0, 3 * CONV_DIM, 3 * CONV_DIM + ATTN_DIM, 3 * CONV_DIM + 2 * ATTN_DIM, 3 * CONV_DIM + 3 * ATTN_DIM,
            3 * CONV_DIM + 3 * ATTN_DIM + IDX_HEADS * IDX_DIM]
    wb = w_in.astype(BF16)
    wmix, wq, wk, wv, wqi = [wb[:, a:b] for a, b in zip(cuts[:-1], cuts[1:])]
    wl = wb[:, cuts[-1]:]
    wl = jnp.pad(wl, ((0, 0), (0, LANES - wl.shape[1])))
    wlt = wl.T
    ones = jnp.ones((1, LANES - IDX_DIM), F32)
    qg = jnp.tile(q_norm_g.reshape(1, HEAD_DIM), (1, 2))
    kg = jnp.tile(k_norm_g.reshape(1, HEAD_DIM), (1, 2))
    kig = jnp.concatenate([kidx_norm_g.reshape(1, IDX_DIM), ones], axis=1)
    blk = jnp.arange(LANES) // HEAD_DIM
    avg = jnp.where(blk[:, None] == blk[None, :], 1.0 / HEAD_DIM, 0.0).astype(BF16)

    bsz = t // seq
    per_b = seq // tm
    row = lambda w: pl.BlockSpec((tm, w), lambda i: (i, 0))
    full = lambda a: pl.BlockSpec(a.shape, lambda i: (0,) * a.ndim)
    mod = pl.BlockSpec((1, 1, d), lambda i: (i // per_b, 0, 0))
    halo = pl.BlockSpec((SUBLANES, d), lambda i: (jnp.maximum(i * (tm // SUBLANES) - 1, 0), 0))
    g1 = norm1_g.reshape(1, d)
    out_shape = (
        jax.ShapeDtypeStruct((t, CONV_DIM), BF16),
        jax.ShapeDtypeStruct((t, ATTN_DIM), BF16),
        jax.ShapeDtypeStruct((t, ATTN_DIM), BF16),
        jax.ShapeDtypeStruct((t, ATTN_DIM), BF16),
        jax.ShapeDtypeStruct((t, IDX_HEADS * IDX_DIM), BF16),
        jax.ShapeDtypeStruct((t, LANES), BF16),
        jax.ShapeDtypeStruct((t, LANES), BF16),
        jax.ShapeDtypeStruct((IDX_HEADS, t), F32),
    )
    del bsz
    return pl.pallas_call(
        functools.partial(_inproj_kernel, seq=seq),
        out_shape=out_shape,
        grid=(t // tm,),
        in_specs=[row(d), halo, mod, mod, full(g1), full(wmix), full(wq), full(wk), full(wv), full(wqi), full(wl),
                  full(wlt), full(conv_w), full(qg), full(kg), full(kig), row(LANES), row(LANES), row(LANES), full(avg)],
        out_specs=(row(CONV_DIM), row(ATTN_DIM), row(ATTN_DIM), row(ATTN_DIM), row(IDX_HEADS * IDX_DIM),
                   row(LANES), row(LANES), pl.BlockSpec((IDX_HEADS, tm), lambda i: (0, i))),
        compiler_params=_params("parallel"),
        name="inproj",
    )(x2, x2, scale1, shift1, g1, wmix, wq, wk, wv, wqi, wl, wlt, conv_w, qg, kg, kig, ct, s1, s2, avg)


def _ukey_to_f32(u):
    s = u ^ jnp.int32(-2 ** 31)
    bits = s ^ ((s >> 31) & jnp.int32(0x7FFFFFFF))
    return lax.bitcast_convert_type(bits, F32)


def _attn_kernel(q_ref, k_ref, v_ref, qi_ref, kilo_ref, kihi_ref, wit_ref, o_ref,
                 sc_ref, sc16_ref, qm_ref, acc_ref, m_ref, l_ref, *, n_sel):
    tq = q_ref.shape[0]
    seq = k_ref.shape[0]
    kc = sc_ref.shape[1]
    assert kc == tq
    lb_n = kc // LANES
    j = pl.program_id(1)
    nch = (j + 1) * (tq // kc)
    lane = lax.broadcasted_iota(jnp.int32, (tq, LANES), 1)

    for h in range(ATTN_HEADS):
        qp = q_ref[:, (h // 2) * LANES:(h // 2 + 1) * LANES].astype(F32)
        keep = (lane < HEAD_DIM) if h % 2 == 0 else (lane >= HEAD_DIM)
        qm_ref[h] = jnp.where(keep, qp, 0.0).astype(BF16)

    q_pos = j * tq + lax.broadcasted_iota(jnp.int32, (kc, tq), 1)
    k_off = lax.broadcasted_iota(jnp.int32, (kc, tq), 0)
    w_rows = wit_ref[...]

    def index_chunk(c, carry):
        r0 = pl.multiple_of(c * kc, kc)
        klo = kilo_ref[pl.ds(r0, kc), :]
        khi = kihi_ref[pl.ds(r0, kc), :]
        acc = jnp.zeros((kc, tq), F32)
        for p in range(IDX_HEADS // 2):
            qip = qi_ref[:, p * LANES:(p + 1) * LANES]
            acc = acc + jnp.maximum(_dot_t(klo, qip), 0.0) * w_rows[2 * p:2 * p + 1, :]
            acc = acc + jnp.maximum(_dot_t(khi, qip), 0.0) * w_rows[2 * p + 1:2 * p + 2, :]
        score = acc * IDX_SCALE
        score = jnp.where(r0 + k_off <= q_pos, score, -jnp.inf)
        score = jnp.where(score == 0.0, 0.0, score)
        sc_ref[c] = score
        top = lax.bitcast_convert_type(score, jnp.int32) & jnp.int32(-65536)
        sc16_ref[c] = lax.bitcast_convert_type(top, F32).astype(BF16)
        return carry

    lax.fori_loop(0, nch, index_chunk, 0)

    t_q = j * tq + lax.broadcasted_iota(jnp.int32, (1, tq), 1)
    k_row = jnp.minimum(t_q + 1, n_sel).astype(F32)
    acc_rows = 4 * SUBLANES
    sub = lax.broadcasted_iota(jnp.int32, (acc_rows, tq), 0)
    idx_bits = (seq - 1).bit_length()

    def count(pred):
        def chunk(c, acc):
            blk = sc_ref[c]
            for g in range(kc // acc_rows):
                kidx = sub + (c * kc + g * acc_rows)
                acc = acc + jnp.where(pred(blk[g * acc_rows:(g + 1) * acc_rows, :], kidx), 1.0, 0.0)
            return acc

        acc = lax.fori_loop(0, nch, chunk, jnp.zeros((acc_rows, tq), F32))
        return jnp.sum(acc, axis=0, keepdims=True)

    def rows8(v):
        return jnp.broadcast_to(v, (acc_rows, tq))

    def count16(cand16):
        def chunk(c, acc):
            blk = sc16_ref[c]
            for g in range(kc // acc_rows):
                hit = blk[g * acc_rows:(g + 1) * acc_rows, :] >= cand16
                acc = acc + jnp.where(hit, jnp.ones_like(cand16), jnp.zeros_like(cand16))
            return acc

        acc = lax.fori_loop(0, nch, chunk, jnp.zeros((acc_rows, tq), BF16))
        return jnp.sum(acc.astype(F32), axis=0, keepdims=True)

    def coarse_bit(i, p):
        cand = p | jnp.left_shift(jnp.int32(1), 31 - i)
        top = lax.bitcast_convert_type(_ukey_to_f32(cand), jnp.int32) & jnp.int32(-65536)
        cand16 = rows8(lax.bitcast_convert_type(top, F32)).astype(BF16)
        return jnp.where(count16(cand16) >= k_row, cand, p)

    def value_bit(i, p):
        cand = p | jnp.left_shift(jnp.int32(1), 31 - i)
        cand_f = rows8(_ukey_to_f32(cand))
        cnt = count(lambda v, kidx: v >= cand_f)
        return jnp.where(cnt >= k_row, cand, p)

    p_hi = lax.fori_loop(0, 16, coarse_bit, jnp.zeros((1, tq), jnp.int32))
    thr = _ukey_to_f32(lax.fori_loop(16, 32, value_bit, p_hi))
    thr8 = rows8(thr)
    n_ge = count(lambda v, kidx: v >= thr8)
    tied = jnp.max(jnp.where(n_ge > k_row, 1.0, 0.0)) > 0.0

    def tie_cut():
        need = k_row - count(lambda v, kidx: v > thr8)

        def index_bit(i, p):
            cand = p | jnp.left_shift(jnp.int32(1), idx_bits - 1 - i)
            cand8 = rows8(cand)
            cnt = count(lambda v, kidx: jnp.where(v == thr8, kidx, seq) < cand8)
            return jnp.where(cnt < need, cand, p)

        return lax.fori_loop(0, idx_bits, index_bit, jnp.zeros((1, tq), jnp.int32))

    cut = lax.cond(tied, tie_cut, lambda: jnp.full((1, tq), seq, jnp.int32))

    def write_bias(c, carry):
        blk = sc_ref[c]
        tie_bias = jnp.where(c * kc + k_off <= cut, 0.0, MASKED)
        bias_t = jnp.where(blk > thr, 0.0, jnp.where(blk == thr, tie_bias, MASKED))
        sc_ref[c] = bias_t.T
        return carry

    lax.fori_loop(0, nch, write_bias, 0)

    for h in range(ATTN_HEADS):
        m_ref[h] = jnp.full((tq, LANES), MASKED, F32)
        l_ref[h] = jnp.zeros((tq, LANES), F32)
        acc_ref[h] = jnp.zeros((tq, LANES), F32)

    def attend(c, carry):
        r0 = pl.multiple_of(c * kc, kc)
        keys = pl.ds(r0, kc)
        bias = sc_ref[c]

        def qk(h):
            return _dot_t(qm_ref[h], k_ref[keys, (h // 2) * LANES:(h // 2 + 1) * LANES])

        s_next = qk(0)
        for h in range(ATTN_HEADS):
            s = s_next + bias
            if h + 1 < ATTN_HEADS:
                s_next = qk(h + 1)
            parts = [s[:, b * LANES:(b + 1) * LANES] for b in range(lb_n)]
            m_old = m_ref[h]
            row_max = jnp.max(functools.reduce(jnp.maximum, parts), axis=1, keepdims=True)
            m_new = jnp.maximum(m_old, row_max)
            alpha = jnp.exp2(m_old - m_new)
            p_parts = [jnp.exp2(part - m_new) for part in parts]
            l_ref[h] = alpha * l_ref[h] + functools.reduce(jnp.add, p_parts)
            p = jnp.concatenate(p_parts, axis=1).astype(BF16)
            acc_ref[h] = alpha * acc_ref[h] + _dot(p, v_ref[keys, (h // 2) * LANES:(h // 2 + 1) * LANES])
            m_ref[h] = m_new
        return carry

    lax.fori_loop(0, nch, attend, 0)
    for pair in range(ATTN_HEADS // 2):
        l_even = jnp.sum(l_ref[2 * pair], axis=1, keepdims=True)
        l_odd = jnp.sum(l_ref[2 * pair + 1], axis=1, keepdims=True)
        o_pair = jnp.where(lane < HEAD_DIM, acc_ref[2 * pair] / l_even, acc_ref[2 * pair + 1] / l_odd)
        o_ref[:, pair * LANES:(pair + 1) * LANES] = o_pair.astype(BF16)


def _attention(q, k, v, qi, kilo, kihi, wit, bsz, seq, tq, kc):
    n_sel = min(TOPK_KEYS_MAX, seq // 4)
    shape3 = lambda a: a.reshape(bsz, seq, a.shape[-1])
    q, k, v, qi, kilo, kihi = map(shape3, (q, k, v, qi, kilo, kihi))
    qblk = lambda w: pl.BlockSpec((None, tq, w), lambda b, j: (b, j, 0))
    kblk = lambda w: pl.BlockSpec((None, seq, w), lambda b, j: (b, 0, 0), pipeline_mode=pl.Buffered(1))
    out = pl.pallas_call(
        functools.partial(_attn_kernel, n_sel=n_sel),
        out_shape=jax.ShapeDtypeStruct((bsz, seq, ATTN_DIM), BF16),
        grid=(bsz, seq // tq),
        in_specs=[qblk(ATTN_DIM), kblk(ATTN_DIM), kblk(ATTN_DIM), qblk(IDX_HEADS * IDX_DIM), kblk(LANES), kblk(LANES),
                  pl.BlockSpec((IDX_HEADS, tq), lambda b, j: (0, b * (seq // tq) + j))],
        out_specs=qblk(ATTN_DIM),
        scratch_shapes=[
            pltpu.VMEM((seq // kc, kc, tq), F32),
            pltpu.VMEM((seq // kc, kc, tq), BF16),
            pltpu.VMEM((ATTN_HEADS, tq, LANES), BF16),
            pltpu.VMEM((ATTN_HEADS, tq, LANES), F32),
            pltpu.VMEM((ATTN_HEADS, tq, LANES), F32),
            pltpu.VMEM((ATTN_HEADS, tq, LANES), F32),
        ],
        compiler_params=_params("parallel", "arbitrary"),
        name="dsa_attention",
    )(q, k, v, qi, kilo, kihi, wit)
    return out.reshape(bsz * seq, ATTN_DIM)


def _outproj_kernel(conv_ref, attn_ref, x_ref, gate1_ref, sc_ref, sh_ref, g2_ref, wout_ref, wrt_ref, rbias_ref,
                    x1_o, h2_o, h2pa_o, h2pb_o, eidx_o, g8_o):
    mix = _dot(conv_ref[...], wout_ref[:CONV_DIM, :]) + _dot(attn_ref[...], wout_ref[CONV_DIM:, :])
    x1 = x_ref[...] + gate1_ref[0] * mix
    x1_o[...] = x1
    h2 = _rms_mod(x1, g2_ref[...], sc_ref[0], sh_ref[0])
    h2b = h2.astype(BF16)
    h2_o[...] = h2b
    quarter = h2.shape[1] // 4
    bits = lax.bitcast_convert_type(h2b.astype(F32), jnp.int32)
    for half_o, c0 in ((h2pa_o, 0), (h2pb_o, 2 * quarter)):
        low = lax.shift_right_logical(bits[:, c0:c0 + quarter], 16)
        half_o[...] = low | (bits[:, c0 + quarter:c0 + 2 * quarter] & jnp.int32(-65536))

    wrt = wrt_ref[...]
    wrt_hi, wrt_lo = _split(wrt)
    h2_hi, h2_lo = _split(h2)
    scores = jax.nn.sigmoid(_dot_t(wrt_hi, h2_hi) + _dot_t(wrt_hi, h2_lo) + _dot_t(wrt_lo, h2_hi))
    work = scores + rbias_ref[...]
    n_e, tm = work.shape
    row_in_tile = lax.broadcasted_iota(jnp.int32, (SUBLANES, tm), 0)
    tiles = [work[j * SUBLANES:(j + 1) * SUBLANES, :] for j in range(n_e // SUBLANES)]
    rank = [jnp.zeros((SUBLANES, tm), F32) for _ in tiles]
    for e2 in range(n_e):
        other = jnp.broadcast_to(work[e2:e2 + 1, :], (SUBLANES, tm))
        for j, tile in enumerate(tiles):
            if j > e2 // SUBLANES:
                beats = other >= tile
            elif j < e2 // SUBLANES:
                beats = other > tile
            else:
                later = row_in_tile > e2 % SUBLANES
                beats = jnp.where(later, jnp.where(other >= tile, 1.0, 0.0), jnp.where(other > tile, 1.0, 0.0)) > 0.0
            rank[j] = rank[j] + jnp.where(beats, 1.0, 0.0)
    eidx_rows, pick_rows = [], []
    for k in range(TOP_K_EXPERTS):
        e_sum = jnp.zeros((SUBLANES, tm), F32)
        s_sum = jnp.zeros((SUBLANES, tm), F32)
        for j in range(len(tiles)):
            is_k = rank[j] == float(k)
            e_sum = e_sum + jnp.where(is_k, (row_in_tile + j * SUBLANES).astype(F32), 0.0)
            s_sum = s_sum + jnp.where(is_k, scores[j * SUBLANES:(j + 1) * SUBLANES, :], 0.0)
        eidx_rows.append(jnp.sum(e_sum, axis=0, keepdims=True))
        pick_rows.append(jnp.sum(s_sum, axis=0, keepdims=True))
    eidx_t = jnp.concatenate(eidx_rows, axis=0)
    picked_t = jnp.concatenate(pick_rows, axis=0)
    g8_t = picked_t / jnp.sum(picked_t, axis=0, keepdims=True) * ROUTED_SCALE
    eidx_o[...] = eidx_t.T.astype(jnp.int32)
    g8_o[...] = g8_t.T


def _outproj(conv, attn, x2, gate1, scale2, shift2, norm2_g, w_out, w_router, router_bias, seq, tm):
    t, d = x2.shape
    e = w_router.shape[1]
    per_b = seq // tm
    row = lambda w: pl.BlockSpec((tm, w), lambda i: (i, 0))
    full = lambda a: pl.BlockSpec(a.shape, lambda i: (0,) * a.ndim)
    mod = pl.BlockSpec((1, 1, d), lambda i: (i // per_b, 0, 0))
    g2 = norm2_g.reshape(1, d)
    wo = w_out.astype(BF16)
    rbias = router_bias.reshape(e, 1)
    wrt = w_router.T
    return pl.pallas_call(
        _outproj_kernel,
        out_shape=(jax.ShapeDtypeStruct((t, d), F32), jax.ShapeDtypeStruct((t, d), BF16),
                   jax.ShapeDtypeStruct((t, d // 4), jnp.int32), jax.ShapeDtypeStruct((t, d // 4), jnp.int32),
                   jax.ShapeDtypeStruct((t, TOP_K_EXPERTS), jnp.int32), jax.ShapeDtypeStruct((t, TOP_K_EXPERTS), F32)),
        grid=(t // tm,),
        in_specs=[row(CONV_DIM), row(ATTN_DIM), row(d), mod, mod, mod, full(g2), full(wo), full(wrt), full(rbias)],
        out_specs=(row(d), row(d), row(d // 4), row(d // 4), row(TOP_K_EXPERTS), row(TOP_K_EXPERTS)),
        compiler_params=_params("parallel"),
        name="outproj_router",
    )(conv, attn, x2, gate1, scale2, shift2, g2, wo, wrt, rbias)


def _route_kernel(eidx_ref, dest_o, cnt_o, cnt_ref, run_ref, *, bm):
    phase = pl.program_id(0)
    i = pl.program_id(1)
    tm = eidx_ref.shape[0]
    e8 = eidx_ref[...]
    lane = lax.broadcasted_iota(jnp.int32, (tm, N_EXPERTS), 1)
    hits = [lane == e8[:, k:k + 1] for k in range(TOP_K_EXPERTS)]
    member = functools.reduce(jnp.add, [jnp.where(hit, 1.0, 0.0) for hit in hits])
    tile_cnt = jnp.sum(member, axis=0, keepdims=True)

    @pl.when(phase == 0)
    def _():
        @pl.when(i == 0)
        def _():
            cnt_ref[...] = jnp.zeros(cnt_ref.shape, F32)

        cnt_ref[...] += tile_cnt

    @pl.when(phase == 1)
    def _():
        @pl.when(i == 0)
        def _():
            blocks = jnp.ceil(cnt_ref[...] / bm)
            r = lax.broadcasted_iota(jnp.int32, (N_EXPERTS, N_EXPERTS), 0)
            c = lax.broadcasted_iota(jnp.int32, (N_EXPERTS, N_EXPERTS), 1)
            before = jnp.where(r < c, 1.0, 0.0).astype(BF16)
            b_hi, b_lo = _split(blocks)
            run_ref[...] = (_dot(b_hi, before) + _dot(b_lo, before)) * bm

        r = lax.broadcasted_iota(jnp.int32, (tm, tm), 0)
        c = lax.broadcasted_iota(jnp.int32, (tm, tm), 1)
        earlier = jnp.where(c < r, 1.0, 0.0).astype(BF16)
        base = run_ref[0:1, :] + _dot(earlier, member.astype(BF16))
        slot = lax.broadcasted_iota(jnp.int32, (tm, TOP_K_EXPERTS), 1)
        dest = jnp.zeros((tm, TOP_K_EXPERTS), F32)
        for k in range(TOP_K_EXPERTS):
            dest = jnp.where(slot == k, jnp.sum(jnp.where(hits[k], base, 0.0), axis=1, keepdims=True), dest)
        dest_o[...] = dest.astype(jnp.int32)
        run_ref[...] += tile_cnt

    cnt_o[...] = cnt_ref[...]


def _route(eidx, bm, tm, tok0, nt):
    i0 = tok0 // tm
    return pl.pallas_call(
        functools.partial(_route_kernel, bm=bm),
        out_shape=(jax.ShapeDtypeStruct((nt, TOP_K_EXPERTS), jnp.int32),
                   jax.ShapeDtypeStruct((SUBLANES, N_EXPERTS), F32)),
        grid=(2, nt // tm),
        in_specs=[pl.BlockSpec((tm, TOP_K_EXPERTS), lambda p, i: (i + i0, 0))],
        out_specs=(pl.BlockSpec((tm, TOP_K_EXPERTS), lambda p, i: (p * i, 0)),
                   pl.BlockSpec((SUBLANES, N_EXPERTS), lambda p, i: (0, 0))),
        scratch_shapes=[pltpu.VMEM((SUBLANES, N_EXPERTS), F32), pltpu.VMEM((SUBLANES, N_EXPERTS), F32)],
        compiler_params=_params("arbitrary", "arbitrary"),
        name="moe_route",
    )(eidx)


def _sc_mesh():
    return plsc.VectorSubcoreMesh(core_axis_name="core", subcore_axis_name="subcore", num_cores=V7X_SC_CORES,
                                  num_subcores=V7X_SC_SUBCORES)


def _sc_scatter_rows(rows, dest_kt, n_out, tok0):
    d = rows.shape[1]
    n_k, t = dest_kt.shape
    window = SC_WINDOW
    i0 = tok0 // window

    @functools.partial(pl.kernel, out_type=jax.ShapeDtypeStruct((n_out, d), rows.dtype), mesh=_sc_mesh(),
                       name="moe_dispatch_scatter")
    def scatter(x_hbm, i_hbm, o_hbm):
        def body(x_vmem, i_vmem):
            for k in range(n_k):
                pltpu.sync_copy(x_vmem, o_hbm.at[i_vmem.at[k]])

        pltpu.emit_pipeline(
            body,
            grid=(t // window,),
            in_specs=[pl.BlockSpec((window, d), lambda i: (i + i0, 0)),
                      pl.BlockSpec((n_k, window), lambda i: (0, i))],
            out_specs=[],
            core_axis_name=("core", "subcore"),
            dimension_semantics=(pltpu.PARALLEL,),
        )(x_hbm, i_hbm)

    return scatter(rows, dest_kt)


def _sc_gather_rows(table, idx):
    n = idx.shape[0]
    d = table.shape[1]
    window = SC_WINDOW

    @functools.partial(pl.kernel, out_type=jax.ShapeDtypeStruct((n, d), table.dtype), mesh=_sc_mesh(),
                       name="moe_combine_gather")
    def gather(tab_hbm, i_hbm, o_hbm):
        def body(i_vmem, o_vmem):
            pltpu.sync_copy(tab_hbm.at[i_vmem.at[0]], o_vmem)

        pltpu.emit_pipeline(
            body,
            grid=(n // window,),
            in_specs=[pl.BlockSpec((1, window), lambda i: (0, i))],
            out_specs=[pl.BlockSpec((window, d), lambda i: (i, 0))],
            core_axis_name=("core", "subcore"),
            dimension_semantics=(pltpu.PARALLEL,),
        )(i_hbm, o_hbm)

    return gather(table, idx.reshape(1, n))


def _ffn_kernel(be_ref, nused_ref, xa_ref, xb_ref, w1_ref, w3_ref, w2_ref, *refs):
    ys_refs, (w13_s, w2_s) = refs[:-2], refs[-2:]
    b = pl.program_id(0)
    f = w2_ref.shape[1]

    @pl.when((b == 0) | (be_ref[b] != be_ref[jnp.maximum(b - 1, 0)]))
    def _():
        w13_s[:, :f] = w1_ref[0].astype(BF16)
        w13_s[:, f:] = w3_ref[0].astype(BF16)
        w2_s[...] = w2_ref[0].astype(BF16)

    @pl.when(b < nused_ref[0])
    def _():
        cols = []
        for x_ref in (xa_ref, xb_ref):
            xu = x_ref[...]
            cols.append(lax.bitcast_convert_type(lax.shift_left(xu, 16), F32).astype(BF16))
            cols.append(lax.bitcast_convert_type(xu & jnp.int32(-65536), F32).astype(BF16))
        hh = _dot(jnp.concatenate(cols, axis=1), w13_s[...])
        act = _silu(hh[:, :f]) * hh[:, f:]
        y = _dot(act.astype(BF16), w2_s[...])
        bits = lax.bitcast_convert_type(y.astype(BF16).astype(F32), jnp.int32)
        for j, y_ref in enumerate(ys_refs):
            c0 = 2 * j * SC_ROW_WORDS
            low = lax.shift_right_logical(bits[:, c0:c0 + SC_ROW_WORDS], 16)
            y_ref[...] = low | (bits[:, c0 + SC_ROW_WORDS:c0 + 2 * SC_ROW_WORDS] & jnp.int32(-65536))


def _ffn(block_e, n_used, xs_a, xs_b, w1, w3, w2, bm):
    rows, q = xs_a.shape
    _, d, f = w1.shape
    n_out = d // (2 * SC_ROW_WORDS)
    grid_spec = pltpu.PrefetchScalarGridSpec(
        num_scalar_prefetch=2,
        grid=(rows // bm,),
        in_specs=[pl.BlockSpec((bm, q), lambda b, be, nu: (b, 0)),
                  pl.BlockSpec((bm, q), lambda b, be, nu: (b, 0)),
                  pl.BlockSpec((1, d, f), lambda b, be, nu: (be[b], 0, 0)),
                  pl.BlockSpec((1, d, f), lambda b, be, nu: (be[b], 0, 0)),
                  pl.BlockSpec((1, f, d), lambda b, be, nu: (be[b], 0, 0))],
        out_specs=tuple(pl.BlockSpec((bm, SC_ROW_WORDS), lambda b, be, nu: (b, 0)) for _ in range(n_out)),
        scratch_shapes=[pltpu.VMEM((d, 2 * f), BF16), pltpu.VMEM((f, d), BF16)],
    )
    return pl.pallas_call(
        _ffn_kernel,
        out_shape=tuple(jax.ShapeDtypeStruct((rows, SC_ROW_WORDS), jnp.int32) for _ in range(n_out)),
        grid_spec=grid_spec,
        compiler_params=_params("arbitrary"),
        name="moe_expert_ffn",
    )(block_e, n_used, xs_a, xs_b, w1, w3, w2)


def _combine_kernel(*refs, n_y):
    y_refs = refs[:n_y]
    g8_ref, h2_ref, x1_ref, gate2_ref, ws13_ref, ws2_ref = refs[n_y:n_y + 6]
    o_ref = refs[-1]
    f = ws2_ref.shape[0]
    hs = _dot(h2_ref[...], ws13_ref[...])
    acc = _dot((_silu(hs[:, :f]) * hs[:, f:]).astype(BF16), ws2_ref[...])
    g8 = g8_ref[...]
    for k in range(TOP_K_EXPERTS):
        cols = []
        for y_ref in y_refs:
            yu = y_ref[k]
            cols.append(lax.bitcast_convert_type(lax.shift_left(yu, 16), F32))
            cols.append(lax.bitcast_convert_type(yu & jnp.int32(-65536), F32))
        acc = acc + g8[:, k:k + 1] * jnp.concatenate(cols, axis=1)
    o_ref[...] = x1_ref[...] + gate2_ref[0] * acc


def _combine(y8s, g8, h2, x1, gate2, ws13, ws2b, seq, tm, tok0, out_prev):
    t, d = x1.shape
    nt = y8s[0].shape[1]
    per_b = seq // tm
    i0 = tok0 // tm
    row = lambda w: pl.BlockSpec((tm, w), lambda i: (i + i0, 0))
    full = lambda a: pl.BlockSpec(a.shape, lambda i: (0,) * a.ndim)
    y_spec = pl.BlockSpec((TOP_K_EXPERTS, tm, SC_ROW_WORDS), lambda i: (0, i, 0))
    in_specs = [y_spec] * len(y8s) + [row(TOP_K_EXPERTS), row(d), row(d),
                                      pl.BlockSpec((1, 1, d), lambda i: ((i + i0) // per_b, 0, 0)), full(ws13), full(ws2b)]
    args = [*y8s, g8, h2, x1, gate2, ws13, ws2b]
    aliases = {}
    if out_prev is not None:
        in_specs.append(pl.BlockSpec(memory_space=pl.ANY))
        aliases = {len(args): 0}
        args.append(out_prev)
    return pl.pallas_call(
        functools.partial(_combine_kernel, n_y=len(y8s)),
        out_shape=jax.ShapeDtypeStruct((t, d), F32),
        grid=(nt // tm,),
        in_specs=in_specs,
        out_specs=row(d),
        input_output_aliases=aliases,
        compiler_params=_params("parallel"),
        name="moe_combine",
    )(*args)


def _moe(h2, h2pa, h2pb, x1, eidx, g8, gate2, w1, w3, w2, ws1, ws3, ws2, seq):
    t, d = x1.shape
    n_e = w1.shape[0]
    bm = MOE_BLOCK_ROWS
    ws13 = jnp.concatenate([ws1, ws3], axis=1).astype(BF16)
    ws2b = ws2.astype(BF16)

    nt = t // MOE_TOKEN_GROUPS
    rows = nt * TOP_K_EXPERTS + n_e * bm
    out = None
    for grp in range(MOE_TOKEN_GROUPS):
        tok0 = grp * nt
        dest, cnt = _route(eidx, bm, min(512, nt), tok0, nt)
        pend = jnp.cumsum(jnp.ceil(cnt[0] / bm) * bm)
        block_row0 = jnp.arange(rows // bm, dtype=F32) * bm
        block_e = jnp.minimum(jnp.sum(pend[None, :] <= block_row0[:, None], axis=1), n_e - 1)
        n_used = (pend[-1:] / bm).astype(jnp.int32)
        dest_kt = dest.T

        xs_a = _sc_scatter_rows(h2pa, dest_kt, rows, tok0)
        xs_b = _sc_scatter_rows(h2pb, dest_kt, rows, tok0)
        ys = _ffn(block_e.astype(jnp.int32), n_used, xs_a, xs_b, w1, w3, w2, bm)
        pair_rows = dest_kt.reshape(-1)
        y8s = [_sc_gather_rows(y, pair_rows).reshape(TOP_K_EXPERTS, nt, SC_ROW_WORDS) for y in ys]
        out = _combine(y8s, g8, h2, x1, gate2, ws13, ws2b, seq, min(256, seq), tok0, out)
    return out


def _layer(x, c, positions, norm1_g, norm2_g, w_ada, b_ada, w_in, conv_w, q_norm_g, k_norm_g, kidx_norm_g, w_out,
           w_router, router_bias, w1, w3, w2, ws1, ws3, ws2):
    bsz, seq, d = x.shape
    t = bsz * seq
    tm = min(512, seq)
    tq = min(512, seq)
    x2 = x.reshape(t, d)

    ada = _ada(c, w_ada, b_ada)
    shift1, scale1, gate1, shift2, scale2, gate2 = [a.reshape(bsz, 1, d) for a in jnp.split(ada, 6, axis=-1)]

    inv_freq = ROPE_THETA ** (-jnp.arange(ROPE_HALF, dtype=F32) / ROPE_HALF)
    invf_lane = inv_freq[(jnp.arange(LANES) % HEAD_DIM) % ROPE_HALF].reshape(1, LANES)
    ct, s1, s2 = _rope_tables(positions.reshape(t, 1), invf_lane, min(2048, t))

    conv, q, k, v, qi, kilo, kihi, wit = _inproj(x2, scale1, shift1, norm1_g, w_in, conv_w, q_norm_g, k_norm_g,
                                                kidx_norm_g, ct, s1, s2, seq, tm)
    attn = _attention(q, k, v, qi, kilo, kihi, wit, bsz, seq, tq, kc=tq)
    x1, h2, h2pa, h2pb, eidx, g8 = _outproj(conv, attn, x2, gate1, scale2, shift2, norm2_g, w_out, w_router,
                                            router_bias, seq, tm)
    out = _moe(h2, h2pa, h2pb, x1, eidx, g8, gate2, w1, w3, w2, ws1, ws3, ws2, seq)
    return out.reshape(bsz, seq, d)


def kernel(x, c, positions, norm1_g, norm2_g, w_ada, b_ada, w_in, conv_w, q_norm_g, k_norm_g, kidx_norm_g, w_out,
           w_router, router_bias, w1, w3, w2, ws1, ws3, ws2):
    for l in range(w_in.shape[0]):
        x = _layer(x, c, positions, norm1_g[l], norm2_g[l], w_ada[l], b_ada[l], w_in[l], conv_w[l], q_norm_g[l],
                   k_norm_g[l], kidx_norm_g[l], w_out[l], w_router[l], router_bias[l], w1[l], w3[l], w2[l], ws1[l],
                   ws3[l], ws2[l])
    return x
```

```python
import functools

import jax
import jax.numpy as jnp
from jax import lax
from jax.experimental import pallas as pl
from jax.experimental.pallas import tpu as pltpu
from jax.experimental.pallas import tpu_sc as plsc

F32 = jnp.float32
BF16 = jnp.bfloat16

HEAD_DIM = 64
ATTN_HEADS = 8
ATTN_DIM = ATTN_HEADS * HEAD_DIM
CONV_DIM = 512
IDX_HEADS = 8
IDX_DIM = 64
IDX_SCALE = (IDX_DIM ** -0.5) * (IDX_HEADS ** -0.5)
TOPK_KEYS_MAX = 256
ROPE_THETA = 500000.0
ROPE_DIM = HEAD_DIM // 4
ROPE_HALF = ROPE_DIM // 2
N_EXPERTS = 64
TOP_K_EXPERTS = 8
EXPERT_DIM = 256
ROUTED_SCALE = 2.5
EPS = 1e-6

LANES = 128
SUBLANES = 8
V7X_VMEM_BYTES = 64 * 1024 * 1024
VMEM_LIMIT = V7X_VMEM_BYTES * 3 // 4
V7X_SC_CORES = 2
V7X_SC_SUBCORES = 16
SC_WINDOW = 128
SC_ROW_WORDS = 256

MASKED = -1e30
Q_SCALE = HEAD_DIM ** -0.5 * 1.4426950408889634
MOE_BLOCK_ROWS = 1024


def _params(*semantics):
    return pltpu.CompilerParams(dimension_semantics=semantics, vmem_limit_bytes=VMEM_LIMIT)


def _dot(a, b):
    return jnp.dot(a, b, preferred_element_type=F32)


def _dot_t(a, b):
    return lax.dot_general(a, b, (((1,), (1,)), ((), ())), preferred_element_type=F32)


def _split(a):
    hi = a.astype(BF16)
    lo = (a - hi.astype(F32)).astype(BF16)
    return hi, lo


def _dot3(a, b):
    a_hi, a_lo = _split(a)
    b_hi, b_lo = _split(b)
    return _dot(a_hi, b_hi) + _dot(a_hi, b_lo) + _dot(a_lo, b_hi)


def _silu(v):
    return v * jax.nn.sigmoid(v)


def _rms_mod(xv, g, scale, shift):
    ms = jnp.mean(xv * xv, axis=-1, keepdims=True)
    y = xv * lax.rsqrt(ms + EPS)
    return (y * g) * (1.0 + scale) + shift


def _ada_kernel(c_ref, w_ref, b_ref, o_ref):
    o_ref[...] = _dot3(_silu(c_ref[...]), w_ref[...]) + b_ref[...]


def _ada(c, w_ada, b_ada):
    bsz, d = c.shape
    n = w_ada.shape[1]
    bn = n // 4
    return pl.pallas_call(
        _ada_kernel,
        out_shape=jax.ShapeDtypeStruct((bsz, n), F32),
        grid=(n // bn,),
        in_specs=[
            pl.BlockSpec((bsz, d), lambda i: (0, 0)),
            pl.BlockSpec((d, bn), lambda i: (0, i)),
            pl.BlockSpec((1, bn), lambda i: (0, i)),
        ],
        out_specs=pl.BlockSpec((bsz, bn), lambda i: (0, i)),
        compiler_params=_params("parallel"),
        name="ada",
    )(c, w_ada, b_ada.reshape(1, n))


def _rope_kernel(pos_ref, invf_ref, c_ref, s1_ref, s2_ref):
    ang = pos_ref[...].astype(F32) * invf_ref[...]
    d = lax.broadcasted_iota(jnp.int32, ang.shape, 1) & (HEAD_DIM - 1)
    cos = jnp.cos(ang)
    sin = jnp.sin(ang)
    c_ref[...] = jnp.where(d < ROPE_DIM, cos, 1.0)
    s1_ref[...] = jnp.where(d < ROPE_HALF, -sin, 0.0)
    s2_ref[...] = jnp.where(d < ROPE_HALF, 0.0, jnp.where(d < ROPE_DIM, sin, 0.0))


def _rope_tables(pos, invf_lane, tm):
    t = pos.shape[0]
    spec = pl.BlockSpec((tm, LANES), lambda i: (i, 0))
    shp = jax.ShapeDtypeStruct((t, LANES), F32)
    return pl.pallas_call(
        _rope_kernel,
        out_shape=(shp, shp, shp),
        grid=(t // tm,),
        in_specs=[pl.BlockSpec((tm, 1), lambda i: (i, 0)), pl.BlockSpec((1, LANES), lambda i: (0, 0))],
        out_specs=(spec, spec, spec),
        compiler_params=_params("parallel"),
        name="rope_tables",
    )(pos, invf_lane)


def _rope(y, c, s1, s2):
    return y * c + pltpu.roll(y, LANES - ROPE_HALF, 1) * s1 + pltpu.roll(y, ROPE_HALF, 1) * s2


def _head_rms(xb, avg):
    hi, lo = _split(xb * xb)
    ms = _dot(hi, avg) + _dot(lo, avg)
    return xb * lax.rsqrt(ms + EPS)


def _inproj_kernel(x_ref, xh_ref, sc_ref, sh_ref, g1_ref, wmix_ref, wq_ref, wk_ref, wv_ref, wqi_ref, wl_ref, wlt_ref,
                   cw_ref, qg_ref, kg_ref, kig_ref, ct_ref, s1_ref, s2_ref, avg_ref,
                   conv_o, q_o, k_o, v_o, qi_o, kilo_o, kihi_o, wit_o, *, seq):
    tm = x_ref.shape[0]
    scale = sc_ref[0]
    shift = sh_ref[0]
    g1 = g1_ref[...]
    h = _rms_mod(x_ref[...], g1, scale, shift).astype(BF16)
    hh = _rms_mod(xh_ref[...], g1, scale, shift).astype(BF16)

    mix = _dot(h, wmix_ref[...])
    mixh = _dot(hh, wmix_ref[...])
    u = mix[:, 2 * CONV_DIM:] * mix[:, :CONV_DIM]
    uh = mixh[:, 2 * CONV_DIM:] * mixh[:, :CONV_DIM]
    seq_start = (pl.program_id(0) * tm) % seq == 0
    uh = jnp.where(seq_start, 0.0, uh)
    ext = jnp.concatenate([uh, u], axis=0)
    u1 = pltpu.roll(ext, 1, 0)[SUBLANES:]
    u2 = pltpu.roll(ext, 2, 0)[SUBLANES:]
    conv = u2 * cw_ref[0:1, :] + u1 * cw_ref[1:2, :] + u * cw_ref[2:3, :]
    conv_o[...] = (mix[:, CONV_DIM:2 * CONV_DIM] * conv).astype(BF16)

    ct = ct_ref[...]
    s1 = s1_ref[...]
    s2 = s2_ref[...]
    avg = avg_ref[...]
    qf = _dot(h, wq_ref[...])
    kf = _dot(h, wk_ref[...])
    qif = _dot(h, wqi_ref[...])
    for p in range(ATTN_DIM // LANES):
        sl = slice(p * LANES, (p + 1) * LANES)
        qn = _rope(_head_rms(qf[:, sl], avg) * qg_ref[...], ct, s1, s2)
        q_o[:, sl] = (qn * Q_SCALE).astype(BF16)
        kn = _rope(_head_rms(kf[:, sl], avg) * kg_ref[...], ct, s1, s2)
        k_o[:, sl] = kn.astype(BF16)
        qi_o[:, sl] = _rope(qif[:, sl], ct, s1, s2).astype(BF16)
    v_o[...] = _dot(h, wv_ref[...]).astype(BF16)

    wit_o[...] = _dot_t(wlt_ref[...], h)[IDX_DIM:IDX_DIM + IDX_HEADS, :]
    last = _dot(h, wl_ref[...])
    lane = lax.broadcasted_iota(jnp.int32, last.shape, 1)
    is_key = lane < IDX_DIM
    kin = _head_rms(last, avg) * kig_ref[...]
    kin = _rope(kin, jnp.where(is_key, ct, 1.0), jnp.where(is_key, s1, 0.0), jnp.where(is_key, s2, 0.0))
    klo = jnp.where(is_key, kin, 0.0)
    kilo_o[...] = klo.astype(BF16)
    kihi_o[...] = pltpu.roll(klo, IDX_DIM, 1).astype(BF16)


def _inproj(x2, scale1, shift1, norm1_g, w_in, conv_w, q_norm_g, k_norm_g, kidx_norm_g, ct, s1, s2, seq, tm):
    t, d = x2.shape
    cuts = [0, 3 * CONV_DIM, 3 * CONV_DIM + ATTN_DIM, 3 * CONV_DIM + 2 * ATTN_DIM, 3 * CONV_DIM + 3 * ATTN_DIM,
            3 * CONV_DIM + 3 * ATTN_DIM + IDX_HEADS * IDX_DIM]
    wb = w_in.astype(BF16)
    wmix, wq, wk, wv, wqi = [wb[:, a:b] for a, b in zip(cuts[:-1], cuts[1:])]
    wl = wb[:, cuts[-1]:]
    wl = jnp.pad(wl, ((0, 0), (0, LANES - wl.shape[1])))
    wlt = wl.T
    ones = jnp.ones((1, LANES - IDX_DIM), F32)
    qg = jnp.tile(q_norm_g.reshape(1, HEAD_DIM), (1, 2))
    kg = jnp.tile(k_norm_g.reshape(1, HEAD_DIM), (1, 2))
    kig = jnp.concatenate([kidx_norm_g.reshape(1, IDX_DIM), ones], axis=1)
    blk = jnp.arange(LANES) // HEAD_DIM
    avg = jnp.where(blk[:, None] == blk[None, :], 1.0 / HEAD_DIM, 0.0).astype(BF16)

    bsz = t // seq
    per_b = seq // tm
    row = lambda w: pl.BlockSpec((tm, w), lambda i: (i, 0))
    full = lambda a: pl.BlockSpec(a.shape, lambda i: (0,) * a.ndim)
    mod = pl.BlockSpec((1, 1, d), lambda i: (i // per_b, 0, 0))
    halo = pl.BlockSpec((SUBLANES, d), lambda i: (jnp.maximum(i * (tm // SUBLANES) - 1, 0), 0))
    g1 = norm1_g.reshape(1, d)
    out_shape = (
        jax.ShapeDtypeStruct((t, CONV_DIM), BF16),
        jax.ShapeDtypeStruct((t, ATTN_DIM), BF16),
        jax.ShapeDtypeStruct((t, ATTN_DIM), BF16),
        jax.ShapeDtypeStruct((t, ATTN_DIM), BF16),
        jax.ShapeDtypeStruct((t, IDX_HEADS * IDX_DIM), BF16),
        jax.ShapeDtypeStruct((t, LANES), BF16),
        jax.ShapeDtypeStruct((t, LANES), BF16),
        jax.ShapeDtypeStruct((IDX_HEADS, t), F32),
    )
    del bsz
    return pl.pallas_call(
        functools.partial(_inproj_kernel, seq=seq),
        out_shape=out_shape,
        grid=(t // tm,),
        in_specs=[row(d), halo, mod, mod, full(g1), full(wmix), full(wq), full(wk), full(wv), full(wqi), full(wl),
                  full(wlt), full(conv_w), full(qg), full(kg), full(kig), row(LANES), row(LANES), row(LANES), full(avg)],
        out_specs=(row(CONV_DIM), row(ATTN_DIM), row(ATTN_DIM), row(ATTN_DIM), row(IDX_HEADS * IDX_DIM),
                   row(LANES), row(LANES), pl.BlockSpec((IDX_HEADS, tm), lambda i: (0, i))),
        compiler_params=_params("parallel"),
        name="inproj",
    )(x2, x2, scale1, shift1, g1, wmix, wq, wk, wv, wqi, wl, wlt, conv_w, qg, kg, kig, ct, s1, s2, avg)


def _ukey_to_f32(u):
    s = u ^ jnp.int32(-2 ** 31)
    bits = s ^ ((s >> 31) & jnp.int32(0x7FFFFFFF))
    return lax.bitcast_convert_type(bits, F32)


def _attn_kernel(q_ref, k_ref, v_ref, qi_ref, kilo_ref, kihi_ref, wit_ref, o_ref,
                 sc_ref, sc16_ref, qm_ref, acc_ref, m_ref, l_ref, *, n_sel):
    tq = q_ref.shape[0]
    seq = k_ref.shape[0]
    kc = sc_ref.shape[1]
    assert kc == tq
    lb_n = kc // LANES
    j = pl.program_id(1)
    nch = (j + 1) * (tq // kc)
    lane = lax.broadcasted_iota(jnp.int32, (tq, LANES), 1)

    for h in range(ATTN_HEADS):
        qp = q_ref[:, (h // 2) * LANES:(h // 2 + 1) * LANES].astype(F32)
        keep = (lane < HEAD_DIM) if h % 2 == 0 else (lane >= HEAD_DIM)
        qm_ref[h] = jnp.where(keep, qp, 0.0).astype(BF16)

    q_pos = j * tq + lax.broadcasted_iota(jnp.int32, (kc, tq), 1)
    k_off = lax.broadcasted_iota(jnp.int32, (kc, tq), 0)
    w_rows = wit_ref[...]

    def index_chunk(c, carry):
        r0 = pl.multiple_of(c * kc, kc)
        klo = kilo_ref[pl.ds(r0, kc), :]
        khi = kihi_ref[pl.ds(r0, kc), :]
        acc = jnp.zeros((kc, tq), F32)
        for p in range(IDX_HEADS // 2):
            qip = qi_ref[:, p * LANES:(p + 1) * LANES]
            acc = acc + jnp.maximum(_dot_t(klo, qip), 0.0) * w_rows[2 * p:2 * p + 1, :]
            acc = acc + jnp.maximum(_dot_t(khi, qip), 0.0) * w_rows[2 * p + 1:2 * p + 2, :]
        score = acc * IDX_SCALE
        score = jnp.where(r0 + k_off <= q_pos, score, -jnp.inf)
        score = jnp.where(score == 0.0, 0.0, score)
        sc_ref[c] = score
        top = lax.bitcast_convert_type(score, jnp.int32) & jnp.int32(-65536)
        sc16_ref[c] = lax.bitcast_convert_type(top, F32).astype(BF16)
        return carry

    lax.fori_loop(0, nch, index_chunk, 0)

    t_q = j * tq + lax.broadcasted_iota(jnp.int32, (1, tq), 1)
    k_row = jnp.minimum(t_q + 1, n_sel).astype(F32)
    acc_rows = 4 * SUBLANES
    sub = lax.broadcasted_iota(jnp.int32, (acc_rows, tq), 0)
    idx_bits = (seq - 1).bit_length()

    def count(pred):
        def chunk(c, acc):
            blk = sc_ref[c]
            for g in range(kc // acc_rows):
                kidx = sub + (c * kc + g * acc_rows)
                acc = acc + jnp.where(pred(blk[g * acc_rows:(g + 1) * acc_rows, :], kidx), 1.0, 0.0)
            return acc

        acc = lax.fori_loop(0, nch, chunk, jnp.zeros((acc_rows, tq), F32))
        return jnp.sum(acc, axis=0, keepdims=True)

    def rows8(v):
        return jnp.broadcast_to(v, (acc_rows, tq))

    def count16(cand16):
        def chunk(c, acc):
            blk = sc16_ref[c]
            for g in range(kc // acc_rows):
                hit = blk[g * acc_rows:(g + 1) * acc_rows, :] >= cand16
                acc = acc + jnp.where(hit, jnp.ones_like(cand16), jnp.zeros_like(cand16))
            return acc

        acc = lax.fori_loop(0, nch, chunk, jnp.zeros((acc_rows, tq), BF16))
        return jnp.sum(acc.astype(F32), axis=0, keepdims=True)

    def coarse_bit(i, p):
        cand = p | jnp.left_shift(jnp.int32(1), 31 - i)
        top = lax.bitcast_convert_type(_ukey_to_f32(cand), jnp.int32) & jnp.int32(-65536)
        cand16 = rows8(lax.bitcast_convert_type(top, F32)).astype(BF16)
        return jnp.where(count16(cand16) >= k_row, cand, p)

    def value_bit(i, p):
        cand = p | jnp.left_shift(jnp.int32(1), 31 - i)
        cand_f = rows8(_ukey_to_f32(cand))
        cnt = count(lambda v, kidx: v >= cand_f)
        return jnp.where(cnt >= k_row, cand, p)

    p_hi = lax.fori_loop(0, 16, coarse_bit, jnp.zeros((1, tq), jnp.int32))
    thr = _ukey_to_f32(lax.fori_loop(16, 32, value_bit, p_hi))
    thr8 = rows8(thr)
    n_ge = count(lambda v, kidx: v >= thr8)
    tied = jnp.max(jnp.where(n_ge > k_row, 1.0, 0.0)) > 0.0

    def tie_cut():
        need = k_row - count(lambda v, kidx: v > thr8)

        def index_bit(i, p):
            cand = p | jnp.left_shift(jnp.int32(1), idx_bits - 1 - i)
            cand8 = rows8(cand)
            cnt = count(lambda v, kidx: jnp.where(v == thr8, kidx, seq) < cand8)
            return jnp.where(cnt < need, cand, p)

        return lax.fori_loop(0, idx_bits, index_bit, jnp.zeros((1, tq), jnp.int32))

    cut = lax.cond(tied, tie_cut, lambda: jnp.full((1, tq), seq, jnp.int32))

    def write_bias(c, carry):
        blk = sc_ref[c]
        tie_bias = jnp.where(c * kc + k_off <= cut, 0.0, MASKED)
        bias_t = jnp.where(blk > thr, 0.0, jnp.where(blk == thr, tie_bias, MASKED))
        sc_ref[c] = bias_t.T
        return carry

    lax.fori_loop(0, nch, write_bias, 0)

    for h in range(ATTN_HEADS):
        m_ref[h] = jnp.full((tq, LANES), MASKED, F32)
        l_ref[h] = jnp.zeros((tq, LANES), F32)
        acc_ref[h] = jnp.zeros((tq, LANES), F32)

    def attend(c, carry):
        r0 = pl.multiple_of(c * kc, kc)
        keys = pl.ds(r0, kc)
        bias = sc_ref[c]

        def qk(h):
            return _dot_t(qm_ref[h], k_ref[keys, (h // 2) * LANES:(h // 2 + 1) * LANES])

        s_next = qk(0)
        for h in range(ATTN_HEADS):
            s = s_next + bias
            if h + 1 < ATTN_HEADS:
                s_next = qk(h + 1)
            parts = [s[:, b * LANES:(b + 1) * LANES] for b in range(lb_n)]
            m_old = m_ref[h]
            row_max = jnp.max(functools.reduce(jnp.maximum, parts), axis=1, keepdims=True)
            m_new = jnp.maximum(m_old, row_max)
            alpha = jnp.exp2(m_old - m_new)
            p_parts = [jnp.exp2(part - m_new) for part in parts]
            l_ref[h] = alpha * l_ref[h] + functools.reduce(jnp.add, p_parts)
            p = jnp.concatenate(p_parts, axis=1).astype(BF16)
            acc_ref[h] = alpha * acc_ref[h] + _dot(p, v_ref[keys, (h // 2) * LANES:(h // 2 + 1) * LANES])
            m_ref[h] = m_new
        return carry

    lax.fori_loop(0, nch, attend, 0)
    for pair in range(ATTN_HEADS // 2):
        l_even = jnp.sum(l_ref[2 * pair], axis=1, keepdims=True)
        l_odd = jnp.sum(l_ref[2 * pair + 1], axis=1, keepdims=True)
        o_pair = jnp.where(lane < HEAD_DIM, acc_ref[2 * pair] / l_even, acc_ref[2 * pair + 1] / l_odd)
        o_ref[:, pair * LANES:(pair + 1) * LANES] = o_pair.astype(BF16)


def _attention(q, k, v, qi, kilo, kihi, wit, bsz, seq, tq, kc):
    n_sel = min(TOPK_KEYS_MAX, seq // 4)
    shape3 = lambda a: a.reshape(bsz, seq, a.shape[-1])
    q, k, v, qi, kilo, kihi = map(shape3, (q, k, v, qi, kilo, kihi))
    qblk = lambda w: pl.BlockSpec((None, tq, w), lambda b, j: (b, j, 0))
    kblk = lambda w: pl.BlockSpec((None, seq, w), lambda b, j: (b, 0, 0), pipeline_mode=pl.Buffered(1))
    out = pl.pallas_call(
        functools.partial(_attn_kernel, n_sel=n_sel),
        out_shape=jax.ShapeDtypeStruct((bsz, seq, ATTN_DIM), BF16),
        grid=(bsz, seq // tq),
        in_specs=[qblk(ATTN_DIM), kblk(ATTN_DIM), kblk(ATTN_DIM), qblk(IDX_HEADS * IDX_DIM), kblk(LANES), kblk(LANES),
                  pl.BlockSpec((IDX_HEADS, tq), lambda b, j: (0, b * (seq // tq) + j))],
        out_specs=qblk(ATTN_DIM),
        scratch_shapes=[
            pltpu.VMEM((seq // kc, kc, tq), F32),
            pltpu.VMEM((seq // kc, kc, tq), BF16),
            pltpu.VMEM((ATTN_HEADS, tq, LANES), BF16),
            pltpu.VMEM((ATTN_HEADS, tq, LANES), F32),
            pltpu.VMEM((ATTN_HEADS, tq, LANES), F32),
            pltpu.VMEM((ATTN_HEADS, tq, LANES), F32),
        ],
        compiler_params=_params("parallel", "arbitrary"),
        name="dsa_attention",
    )(q, k, v, qi, kilo, kihi, wit)
    return out.reshape(bsz * seq, ATTN_DIM)


def _outproj_kernel(conv_ref, attn_ref, x_ref, gate1_ref, sc_ref, sh_ref, g2_ref, wout_ref, wrt_ref, rbias_ref,
                    x1_o, h2_o, h2pa_o, h2pb_o, eidx_o, g8_o, cnt_o):
    mix = _dot(conv_ref[...], wout_ref[:CONV_DIM, :]) + _dot(attn_ref[...], wout_ref[CONV_DIM:, :])
    x1 = x_ref[...] + gate1_ref[0] * mix
    x1_o[...] = x1
    h2 = _rms_mod(x1, g2_ref[...], sc_ref[0], sh_ref[0])
    h2b = h2.astype(BF16)
    h2_o[...] = h2b
    quarter = h2.shape[1] // 4
    bits = lax.bitcast_convert_type(h2b.astype(F32), jnp.int32)
    for half_o, c0 in ((h2pa_o, 0), (h2pb_o, 2 * quarter)):
        low = lax.shift_right_logical(bits[:, c0:c0 + quarter], 16)
        half_o[...] = low | (bits[:, c0 + quarter:c0 + 2 * quarter] & jnp.int32(-65536))

    wrt = wrt_ref[...]
    wrt_hi, wrt_lo = _split(wrt)
    h2_hi, h2_lo = _split(h2)
    scores = jax.nn.sigmoid(_dot_t(wrt_hi, h2_hi) + _dot_t(wrt_hi, h2_lo) + _dot_t(wrt_lo, h2_hi))
    work = scores + rbias_ref[...]
    n_e, tm = work.shape
    row_in_tile = lax.broadcasted_iota(jnp.int32, (SUBLANES, tm), 0)
    tiles = [work[j * SUBLANES:(j + 1) * SUBLANES, :] for j in range(n_e // SUBLANES)]
    rank = [jnp.zeros((SUBLANES, tm), F32) for _ in tiles]
    for e2 in range(n_e):
        other = jnp.broadcast_to(work[e2:e2 + 1, :], (SUBLANES, tm))
        for j, tile in enumerate(tiles):
            if j > e2 // SUBLANES:
                beats = other >= tile
            elif j < e2 // SUBLANES:
                beats = other > tile
            else:
                later = row_in_tile > e2 % SUBLANES
                beats = jnp.where(later, jnp.where(other >= tile, 1.0, 0.0), jnp.where(other > tile, 1.0, 0.0)) > 0.0
            rank[j] = rank[j] + jnp.where(beats, 1.0, 0.0)
    chosen = jnp.concatenate([jnp.where(r < float(TOP_K_EXPERTS), 1.0, 0.0) for r in rank], axis=0)

    @pl.when(pl.program_id(0) == 0)
    def _():
        cnt_o[...] = jnp.zeros(cnt_o.shape, F32)

    cnt_o[...] += jnp.sum(chosen, axis=1, keepdims=True)
    eidx_rows, pick_rows = [], []
    for k in range(TOP_K_EXPERTS):
        e_sum = jnp.zeros((SUBLANES, tm), F32)
        s_sum = jnp.zeros((SUBLANES, tm), F32)
        for j in range(len(tiles)):
            is_k = rank[j] == float(k)
            e_sum = e_sum + jnp.where(is_k, (row_in_tile + j * SUBLANES).astype(F32), 0.0)
            s_sum = s_sum + jnp.where(is_k, scores[j * SUBLANES:(j + 1) * SUBLANES, :], 0.0)
        eidx_rows.append(jnp.sum(e_sum, axis=0, keepdims=True))
        pick_rows.append(jnp.sum(s_sum, axis=0, keepdims=True))
    eidx_t = jnp.concatenate(eidx_rows, axis=0)
    picked_t = jnp.concatenate(pick_rows, axis=0)
    g8_t = picked_t / jnp.sum(picked_t, axis=0, keepdims=True) * ROUTED_SCALE
    eidx_o[...] = eidx_t.T.astype(jnp.int32)
    g8_o[...] = g8_t.T


def _outproj(conv, attn, x2, gate1, scale2, shift2, norm2_g, w_out, w_router, router_bias, seq, tm):
    t, d = x2.shape
    e = w_router.shape[1]
    per_b = seq // tm
    row = lambda w: pl.BlockSpec((tm, w), lambda i: (i, 0))
    full = lambda a: pl.BlockSpec(a.shape, lambda i: (0,) * a.ndim)
    mod = pl.BlockSpec((1, 1, d), lambda i: (i // per_b, 0, 0))
    g2 = norm2_g.reshape(1, d)
    wo = w_out.astype(BF16)
    rbias = router_bias.reshape(e, 1)
    wrt = w_router.T
    return pl.pallas_call(
        _outproj_kernel,
        out_shape=(jax.ShapeDtypeStruct((t, d), F32), jax.ShapeDtypeStruct((t, d), BF16),
                   jax.ShapeDtypeStruct((t, d // 4), jnp.int32), jax.ShapeDtypeStruct((t, d // 4), jnp.int32),
                   jax.ShapeDtypeStruct((t, TOP_K_EXPERTS), jnp.int32), jax.ShapeDtypeStruct((t, TOP_K_EXPERTS), F32),
                   jax.ShapeDtypeStruct((e, LANES), F32)),
        grid=(t // tm,),
        in_specs=[row(CONV_DIM), row(ATTN_DIM), row(d), mod, mod, mod, full(g2), full(wo), full(wrt), full(rbias)],
        out_specs=(row(d), row(d), row(d // 4), row(d // 4), row(TOP_K_EXPERTS), row(TOP_K_EXPERTS),
                   pl.BlockSpec((e, LANES), lambda i: (0, 0))),
        compiler_params=_params("arbitrary"),
        name="outproj_router",
    )(conv, attn, x2, gate1, scale2, shift2, g2, wo, wrt, rbias)


def _route_kernel(eidx_ref, cnt_ref, earlier_ref, dest_o, run_ref, *, bm):
    tm = eidx_ref.shape[0]
    e8 = eidx_ref[...]
    lane = lax.broadcasted_iota(jnp.int32, (tm, N_EXPERTS), 1)
    hits = [lane == e8[:, k:k + 1] for k in range(TOP_K_EXPERTS)]
    member = functools.reduce(jnp.add, [jnp.where(hit, 1.0, 0.0) for hit in hits])

    @pl.when(pl.program_id(0) == 0)
    def _():
        blocks = jnp.ceil(cnt_ref[...] / bm)
        r = lax.broadcasted_iota(jnp.int32, (N_EXPERTS, N_EXPERTS), 0)
        c = lax.broadcasted_iota(jnp.int32, (N_EXPERTS, N_EXPERTS), 1)
        before = jnp.where(r < c, 1.0, 0.0).astype(BF16)
        b_hi, b_lo = _split(blocks)
        run_ref[...] = (_dot(b_hi, before) + _dot(b_lo, before)) * bm

    base = run_ref[0:1, :] + _dot(earlier_ref[...], member.astype(BF16))
    slot = lax.broadcasted_iota(jnp.int32, (tm, TOP_K_EXPERTS), 1)
    dest = jnp.zeros((tm, TOP_K_EXPERTS), F32)
    for k in range(TOP_K_EXPERTS):
        dest = jnp.where(slot == k, jnp.sum(jnp.where(hits[k], base, 0.0), axis=1, keepdims=True), dest)
    dest_o[...] = dest.astype(jnp.int32)
    run_ref[...] += jnp.sum(member, axis=0, keepdims=True)


def _route(eidx, counts, bm, tm):
    nt = eidx.shape[0]
    cnt8 = jnp.broadcast_to(counts.reshape(1, N_EXPERTS), (SUBLANES, N_EXPERTS))
    earlier = jnp.tri(tm, k=-1, dtype=BF16)
    return pl.pallas_call(
        functools.partial(_route_kernel, bm=bm),
        out_shape=jax.ShapeDtypeStruct((nt, TOP_K_EXPERTS), jnp.int32),
        grid=(nt // tm,),
        in_specs=[pl.BlockSpec((tm, TOP_K_EXPERTS), lambda i: (i, 0)),
                  pl.BlockSpec((SUBLANES, N_EXPERTS), lambda i: (0, 0)),
                  pl.BlockSpec((tm, tm), lambda i: (0, 0))],
        out_specs=pl.BlockSpec((tm, TOP_K_EXPERTS), lambda i: (i, 0)),
        scratch_shapes=[pltpu.VMEM((SUBLANES, N_EXPERTS), F32)],
        compiler_params=_params("arbitrary"),
        name="moe_route",
    )(eidx, cnt8, earlier)


def _sc_mesh():
    return plsc.VectorSubcoreMesh(core_axis_name="core", subcore_axis_name="subcore", num_cores=V7X_SC_CORES,
                                  num_subcores=V7X_SC_SUBCORES)


def _sc_scatter_rows(rows, dest_kt, n_out):
    t, d = rows.shape
    n_k = dest_kt.shape[0]
    window = SC_WINDOW

    @functools.partial(pl.kernel, out_type=jax.ShapeDtypeStruct((n_out, d), rows.dtype), mesh=_sc_mesh(),
                       name="moe_dispatch_scatter")
    def scatter(x_hbm, i_hbm, o_hbm):
        def body(x_vmem, i_vmem):
            for k in range(n_k):
                pltpu.sync_copy(x_vmem, o_hbm.at[i_vmem.at[k]])

        pltpu.emit_pipeline(
            body,
            grid=(t // window,),
            in_specs=[pl.BlockSpec((window, d), lambda i: (i, 0)), pl.BlockSpec((n_k, window), lambda i: (0, i))],
            out_specs=[],
            core_axis_name=("core", "subcore"),
            dimension_semantics=(pltpu.PARALLEL,),
        )(x_hbm, i_hbm)

    return scatter(rows, dest_kt)


def _sc_gather_rows(table, idx):
    n = idx.shape[0]
    d = table.shape[1]
    window = SC_WINDOW

    @functools.partial(pl.kernel, out_type=jax.ShapeDtypeStruct((n, d), table.dtype), mesh=_sc_mesh(),
                       name="moe_combine_gather")
    def gather(tab_hbm, i_hbm, o_hbm):
        def body(i_vmem, o_vmem):
            pltpu.sync_copy(tab_hbm.at[i_vmem.at[0]], o_vmem)

        pltpu.emit_pipeline(
            body,
            grid=(n // window,),
            in_specs=[pl.BlockSpec((1, window), lambda i: (0, i))],
            out_specs=[pl.BlockSpec((window, d), lambda i: (i, 0))],
            core_axis_name=("core", "subcore"),
            dimension_semantics=(pltpu.PARALLEL,),
        )(i_hbm, o_hbm)

    return gather(table, idx.reshape(1, n))


def _ffn_kernel(be_ref, nused_ref, xa_ref, xb_ref, w1_ref, w3_ref, w2_ref, *refs):
    ys_refs, (w13_s, w2_s) = refs[:-2], refs[-2:]
    b = pl.program_id(0)
    f = w2_ref.shape[1]

    @pl.when((b == 0) | (be_ref[b] != be_ref[jnp.maximum(b - 1, 0)]))
    def _():
        w13_s[:, :f] = w1_ref[0].astype(BF16)
        w13_s[:, f:] = w3_ref[0].astype(BF16)
        w2_s[...] = w2_ref[0].astype(BF16)

    @pl.when(b < nused_ref[0])
    def _():
        cols = []
        for x_ref in (xa_ref, xb_ref):
            xu = x_ref[...]
            cols.append(lax.bitcast_convert_type(lax.shift_left(xu, 16), F32).astype(BF16))
            cols.append(lax.bitcast_convert_type(xu & jnp.int32(-65536), F32).astype(BF16))
        hh = _dot(jnp.concatenate(cols, axis=1), w13_s[...])
        act = _silu(hh[:, :f]) * hh[:, f:]
        y = _dot(act.astype(BF16), w2_s[...])
        bits = lax.bitcast_convert_type(y.astype(BF16).astype(F32), jnp.int32)
        for j, y_ref in enumerate(ys_refs):
            c0 = 2 * j * SC_ROW_WORDS
            low = lax.shift_right_logical(bits[:, c0:c0 + SC_ROW_WORDS], 16)
            y_ref[...] = low | (bits[:, c0 + SC_ROW_WORDS:c0 + 2 * SC_ROW_WORDS] & jnp.int32(-65536))


def _ffn(block_e, n_used, xs_a, xs_b, w1, w3, w2, bm):
    rows, q = xs_a.shape
    _, d, f = w1.shape
    n_out = d // (2 * SC_ROW_WORDS)
    grid_spec = pltpu.PrefetchScalarGridSpec(
        num_scalar_prefetch=2,
        grid=(rows // bm,),
        in_specs=[pl.BlockSpec((bm, q), lambda b, be, nu: (b, 0)),
                  pl.BlockSpec((bm, q), lambda b, be, nu: (b, 0)),
                  pl.BlockSpec((1, d, f), lambda b, be, nu: (be[b], 0, 0)),
                  pl.BlockSpec((1, d, f), lambda b, be, nu: (be[b], 0, 0)),
                  pl.BlockSpec((1, f, d), lambda b, be, nu: (be[b], 0, 0))],
        out_specs=tuple(pl.BlockSpec((bm, SC_ROW_WORDS), lambda b, be, nu: (b, 0)) for _ in range(n_out)),
        scratch_shapes=[pltpu.VMEM((d, 2 * f), BF16), pltpu.VMEM((f, d), BF16)],
    )
    return pl.pallas_call(
        _ffn_kernel,
        out_shape=tuple(jax.ShapeDtypeStruct((rows, SC_ROW_WORDS), jnp.int32) for _ in range(n_out)),
        grid_spec=grid_spec,
        compiler_params=_params("arbitrary"),
        name="moe_expert_ffn",
    )(block_e, n_used, xs_a, xs_b, w1, w3, w2)


def _combine_kernel(*refs):
    y_refs = refs[:-7]
    g8_ref, h2_ref, x1_ref, gate2_ref, ws13_ref, ws2_ref, o_ref = refs[-7:]
    f = ws2_ref.shape[0]
    hs = _dot(h2_ref[...], ws13_ref[...])
    acc = _dot((_silu(hs[:, :f]) * hs[:, f:]).astype(BF16), ws2_ref[...])
    g8 = g8_ref[...]
    for k in range(TOP_K_EXPERTS):
        cols = []
        for y_ref in y_refs:
            yu = y_ref[k]
            cols.append(lax.bitcast_convert_type(lax.shift_left(yu, 16), F32))
            cols.append(lax.bitcast_convert_type(yu & jnp.int32(-65536), F32))
        acc = acc + g8[:, k:k + 1] * jnp.concatenate(cols, axis=1)
    o_ref[...] = x1_ref[...] + gate2_ref[0] * acc


def _combine(y8s, g8, h2, x1, gate2, ws13, ws2b, seq, tm):
    t, d = x1.shape
    per_b = seq // tm
    row = lambda w: pl.BlockSpec((tm, w), lambda i: (i, 0))
    full = lambda a: pl.BlockSpec(a.shape, lambda i: (0,) * a.ndim)
    y_spec = pl.BlockSpec((TOP_K_EXPERTS, tm, SC_ROW_WORDS), lambda i: (0, i, 0))
    return pl.pallas_call(
        _combine_kernel,
        out_shape=jax.ShapeDtypeStruct((t, d), F32),
        grid=(t // tm,),
        in_specs=[y_spec] * len(y8s) + [row(TOP_K_EXPERTS), row(d), row(d),
                                         pl.BlockSpec((1, 1, d), lambda i: (i // per_b, 0, 0)), full(ws13), full(ws2b)],
        out_specs=row(d),
        compiler_params=_params("parallel"),
        name="moe_combine",
    )(*y8s, g8, h2, x1, gate2, ws13, ws2b)


def _moe(h2, h2pa, h2pb, x1, eidx, g8, counts, gate2, w1, w3, w2, ws1, ws3, ws2, seq):
    t, d = x1.shape
    n_e = w1.shape[0]
    bm = MOE_BLOCK_ROWS
    rows = t * TOP_K_EXPERTS + n_e * bm
    ws13 = jnp.concatenate([ws1, ws3], axis=1).astype(BF16)
    ws2b = ws2.astype(BF16)

    dest = _route(eidx, counts, bm, min(512, t))
    pend = jnp.cumsum(jnp.ceil(counts / bm) * bm)
    block_row0 = jnp.arange(rows // bm, dtype=F32) * bm
    block_e = jnp.minimum(jnp.sum(pend[None, :] <= block_row0[:, None], axis=1), n_e - 1)
    n_used = (pend[-1:] / bm).astype(jnp.int32)
    dest_kt = dest.T

    xs_a = _sc_scatter_rows(h2pa, dest_kt, rows)
    xs_b = _sc_scatter_rows(h2pb, dest_kt, rows)
    ys = _ffn(block_e.astype(jnp.int32), n_used, xs_a, xs_b, w1, w3, w2, bm)
    pair_rows = dest_kt.reshape(-1)
    y8s = [_sc_gather_rows(y, pair_rows).reshape(TOP_K_EXPERTS, t, SC_ROW_WORDS) for y in ys]
    return _combine(y8s, g8, h2, x1, gate2, ws13, ws2b, seq, min(256, seq))


def _layer(x, c, positions, norm1_g, norm2_g, w_ada, b_ada, w_in, conv_w, q_norm_g, k_norm_g, kidx_norm_g, w_out,
           w_router, router_bias, w1, w3, w2, ws1, ws3, ws2):
    bsz, seq, d = x.shape
    t = bsz * seq
    tm = min(512, seq)
    tq = min(512, seq)
    x2 = x.reshape(t, d)

    ada = _ada(c, w_ada, b_ada)
    shift1, scale1, gate1, shift2, scale2, gate2 = [a.reshape(bsz, 1, d) for a in jnp.split(ada, 6, axis=-1)]

    inv_freq = ROPE_THETA ** (-jnp.arange(ROPE_HALF, dtype=F32) / ROPE_HALF)
    invf_lane = inv_freq[(jnp.arange(LANES) % HEAD_DIM) % ROPE_HALF].reshape(1, LANES)
    ct, s1, s2 = _rope_tables(positions.reshape(t, 1), invf_lane, min(2048, t))

    conv, q, k, v, qi, kilo, kihi, wit = _inproj(x2, scale1, shift1, norm1_g, w_in, conv_w, q_norm_g, k_norm_g,
                                                kidx_norm_g, ct, s1, s2, seq, tm)
    attn = _attention(q, k, v, qi, kilo, kihi, wit, bsz, seq, tq, kc=tq)
    x1, h2, h2pa, h2pb, eidx, g8, cnt = _outproj(conv, attn, x2, gate1, scale2, shift2, norm2_g, w_out, w_router,
                                                 router_bias, seq, tm)
    out = _moe(h2, h2pa, h2pb, x1, eidx, g8, cnt[:, 0], gate2, w1, w3, w2, ws1, ws3, ws2, seq)
    return out.reshape(bsz, seq, d)


def kernel(x, c, positions, norm1_g, norm2_g, w_ada, b_ada, w_in, conv_w, q_norm_g, k_norm_g, kidx_norm_g, w_out,
           w_router, router_bias, w1, w3, w2, ws1, ws3, ws2):
    for l in range(w_in.shape[0]):
        x = _layer(x, c, positions, norm1_g[l], norm2_g[l], w_ada[l], b_ada[l], w_in[l], conv_w[l], q_norm_g[l],
                   k_norm_g[l], kidx_norm_g[l], w_out[l], w_router[l], router_bias[l], w1[l], w3[l], w2[l], ws1[l],
                   ws3[l], ws2[l])
    return x
```

```python
import functools

import jax
import jax.numpy as jnp
from jax import lax
from jax.experimental import pallas as pl
from jax.experimental.pallas import tpu as pltpu
from jax.experimental.pallas import tpu_sc as plsc

F32 = jnp.float32
BF16 = jnp.bfloat16

HEAD_DIM = 64
ATTN_HEADS = 8
ATTN_DIM = ATTN_HEADS * HEAD_DIM
CONV_DIM = 512
IDX_HEADS = 8
IDX_DIM = 64
IDX_SCALE = (IDX_DIM ** -0.5) * (IDX_HEADS ** -0.5)
TOPK_KEYS_MAX = 256
ROPE_THETA = 500000.0
ROPE_DIM = HEAD_DIM // 4
ROPE_HALF = ROPE_DIM // 2
N_EXPERTS = 64
TOP_K_EXPERTS = 8
EXPERT_DIM = 256
ROUTED_SCALE = 2.5
EPS = 1e-6

LANES = 128
SUBLANES = 8
V7X_VMEM_BYTES = 64 * 1024 * 1024
VMEM_LIMIT = V7X_VMEM_BYTES * 3 // 4
V7X_SC_CORES = 2
V7X_SC_SUBCORES = 16
SC_WINDOW = 128
SC_ROW_WORDS = 256

MASKED = -1e30
Q_SCALE = HEAD_DIM ** -0.5 * 1.4426950408889634
MOE_BLOCK_ROWS = 1024


def _params(*semantics):
    return pltpu.CompilerParams(dimension_semantics=semantics, vmem_limit_bytes=VMEM_LIMIT)


def _dot(a, b):
    return jnp.dot(a, b, preferred_element_type=F32)


def _dot_t(a, b):
    return lax.dot_general(a, b, (((1,), (1,)), ((), ())), preferred_element_type=F32)


def _split(a):
    hi = a.astype(BF16)
    lo = (a - hi.astype(F32)).astype(BF16)
    return hi, lo


def _dot3(a, b):
    a_hi, a_lo = _split(a)
    b_hi, b_lo = _split(b)
    return _dot(a_hi, b_hi) + _dot(a_hi, b_lo) + _dot(a_lo, b_hi)


def _silu(v):
    return v * jax.nn.sigmoid(v)


def _rms_mod(xv, g, scale, shift):
    ms = jnp.mean(xv * xv, axis=-1, keepdims=True)
    y = xv * lax.rsqrt(ms + EPS)
    return (y * g) * (1.0 + scale) + shift


def _ada_kernel(c_ref, w_ref, b_ref, o_ref):
    o_ref[...] = _dot3(_silu(c_ref[...]), w_ref[...]) + b_ref[...]


def _ada(c, w_ada, b_ada):
    bsz, d = c.shape
    n = w_ada.shape[1]
    bn = n // 4
    return pl.pallas_call(
        _ada_kernel,
        out_shape=jax.ShapeDtypeStruct((bsz, n), F32),
        grid=(n // bn,),
        in_specs=[
            pl.BlockSpec((bsz, d), lambda i: (0, 0)),
            pl.BlockSpec((d, bn), lambda i: (0, i)),
            pl.BlockSpec((1, bn), lambda i: (0, i)),
        ],
        out_specs=pl.BlockSpec((bsz, bn), lambda i: (0, i)),
        compiler_params=_params("parallel"),
        name="ada",
    )(c, w_ada, b_ada.reshape(1, n))


def _rope_kernel(pos_ref, invf_ref, c_ref, s1_ref, s2_ref):
    ang = pos_ref[...].astype(F32) * invf_ref[...]
    d = lax.broadcasted_iota(jnp.int32, ang.shape, 1) & (HEAD_DIM - 1)
    cos = jnp.cos(ang)
    sin = jnp.sin(ang)
    c_ref[...] = jnp.where(d < ROPE_DIM, cos, 1.0)
    s1_ref[...] = jnp.where(d < ROPE_HALF, -sin, 0.0)
    s2_ref[...] = jnp.where(d < ROPE_HALF, 0.0, jnp.where(d < ROPE_DIM, sin, 0.0))


def _rope_tables(pos, invf_lane, tm):
    t = pos.shape[0]
    spec = pl.BlockSpec((tm, LANES), lambda i: (i, 0))
    shp = jax.ShapeDtypeStruct((t, LANES), F32)
    return pl.pallas_call(
        _rope_kernel,
        out_shape=(shp, shp, shp),
        grid=(t // tm,),
        in_specs=[pl.BlockSpec((tm, 1), lambda i: (i, 0)), pl.BlockSpec((1, LANES), lambda i: (0, 0))],
        out_specs=(spec, spec, spec),
        compiler_params=_params("parallel"),
        name="rope_tables",
    )(pos, invf_lane)


def _rope(y, c, s1, s2):
    return y * c + pltpu.roll(y, LANES - ROPE_HALF, 1) * s1 + pltpu.roll(y, ROPE_HALF, 1) * s2


def _head_rms(xb, avg):
    hi, lo = _split(xb * xb)
    ms = _dot(hi, avg) + _dot(lo, avg)
    return xb * lax.rsqrt(ms + EPS)


def _inproj_kernel(x_ref, xh_ref, sc_ref, sh_ref, g1_ref, wmix_ref, wq_ref, wk_ref, wv_ref, wqi_ref, wl_ref, wlt_ref,
                   cw_ref, qg_ref, kg_ref, kig_ref, ct_ref, s1_ref, s2_ref, avg_ref,
                   conv_o, q_o, k_o, v_o, qi_o, kilo_o, kihi_o, wit_o, *, seq):
    tm = x_ref.shape[0]
    scale = sc_ref[0]
    shift = sh_ref[0]
    g1 = g1_ref[...]
    h = _rms_mod(x_ref[...], g1, scale, shift).astype(BF16)
    hh = _rms_mod(xh_ref[...], g1, scale, shift).astype(BF16)

    mix = _dot(h, wmix_ref[...])
    mixh = _dot(hh, wmix_ref[...])
    u = mix[:, 2 * CONV_DIM:] * mix[:, :CONV_DIM]
    uh = mixh[:, 2 * CONV_DIM:] * mixh[:, :CONV_DIM]
    seq_start = (pl.program_id(0) * tm) % seq == 0
    uh = jnp.where(seq_start, 0.0, uh)
    ext = jnp.concatenate([uh, u], axis=0)
    u1 = pltpu.roll(ext, 1, 0)[SUBLANES:]
    u2 = pltpu.roll(ext, 2, 0)[SUBLANES:]
    conv = u2 * cw_ref[0:1, :] + u1 * cw_ref[1:2, :] + u * cw_ref[2:3, :]
    conv_o[...] = (mix[:, CONV_DIM:2 * CONV_DIM] * conv).astype(BF16)

    ct = ct_ref[...]
    s1 = s1_ref[...]
    s2 = s2_ref[...]
    avg = avg_ref[...]
    qf = _dot(h, wq_ref[...])
    kf = _dot(h, wk_ref[...])
    qif = _dot(h, wqi_ref[...])
    for p in range(ATTN_DIM // LANES):
        sl = slice(p * LANES, (p + 1) * LANES)
        qn = _rope(_head_rms(qf[:, sl], avg) * qg_ref[...], ct, s1, s2)
        q_o[:, sl] = (qn * Q_SCALE).astype(BF16)
        kn = _rope(_head_rms(kf[:, sl], avg) * kg_ref[...], ct, s1, s2)
        k_o[:, sl] = kn.astype(BF16)
        qi_o[:, sl] = _rope(qif[:, sl], ct, s1, s2).astype(BF16)
    v_o[...] = _dot(h, wv_ref[...]).astype(BF16)

    wit_o[...] = _dot_t(wlt_ref[...], h)[IDX_DIM:IDX_DIM + IDX_HEADS, :]
    last = _dot(h, wl_ref[...])
    lane = lax.broadcasted_iota(jnp.int32, last.shape, 1)
    is_key = lane < IDX_DIM
    kin = _head_rms(last, avg) * kig_ref[...]
    kin = _rope(kin, jnp.where(is_key, ct, 1.0), jnp.where(is_key, s1, 0.0), jnp.where(is_key, s2, 0.0))
    klo = jnp.where(is_key, kin, 0.0)
    kilo_o[...] = klo.astype(BF16)
    kihi_o[...] = pltpu.roll(klo, IDX_DIM, 1).astype(BF16)


def _inproj(x2, scale1, shift1, norm1_g, w_in, conv_w, q_norm_g, k_norm_g, kidx_norm_g, ct, s1, s2, seq, tm):
    t, d = x2.shape
    cuts = [0, 3 * CONV_DIM, 3 * CONV_DIM + ATTN_DIM, 3 * CONV_DIM + 2 * ATTN_DIM, 3 * CONV_DIM + 3 * ATTN_DIM,
            3 * CONV_DIM + 3 * ATTN_DIM + IDX_HEADS * IDX_DIM]
    wb = w_in.astype(BF16)
    wmix, wq, wk, wv, wqi = [wb[:, a:b] for a, b in zip(cuts[:-1], cuts[1:])]
    wl = wb[:, cuts[-1]:]
    wl = jnp.pad(wl, ((0, 0), (0, LANES - wl.shape[1])))
    wlt = wl.T
    ones = jnp.ones((1, LANES - IDX_DIM), F32)
    qg = jnp.tile(q_norm_g.reshape(1, HEAD_DIM), (1, 2))
    kg = jnp.tile(k_norm_g.reshape(1, HEAD_DIM), (1, 2))
    kig = jnp.concatenate([kidx_norm_g.reshape(1, IDX_DIM), ones], axis=1)
    blk = jnp.arange(LANES) // HEAD_DIM
    avg = jnp.where(blk[:, None] == blk[None, :], 1.0 / HEAD_DIM, 0.0).astype(BF16)

    bsz = t // seq
    per_b = seq // tm
    row = lambda w: pl.BlockSpec((tm, w), lambda i: (i, 0))
    full = lambda a: pl.BlockSpec(a.shape, lambda i: (0,) * a.ndim)
    mod = pl.BlockSpec((1, 1, d), lambda i: (i // per_b, 0, 0))
    halo = pl.BlockSpec((SUBLANES, d), lambda i: (jnp.maximum(i * (tm // SUBLANES) - 1, 0), 0))
    g1 = norm1_g.reshape(1, d)
    out_shape = (
        jax.ShapeDtypeStruct((t, CONV_DIM), BF16),
        jax.ShapeDtypeStruct((t, ATTN_DIM), BF16),
        jax.ShapeDtypeStruct((t, ATTN_DIM), BF16),
        jax.ShapeDtypeStruct((t, ATTN_DIM), BF16),
        jax.ShapeDtypeStruct((t, IDX_HEADS * IDX_DIM), BF16),
        jax.ShapeDtypeStruct((t, LANES), BF16),
        jax.ShapeDtypeStruct((t, LANES), BF16),
        jax.ShapeDtypeStruct((IDX_HEADS, t), F32),
    )
    del bsz
    return pl.pallas_call(
        functools.partial(_inproj_kernel, seq=seq),
        out_shape=out_shape,
        grid=(t // tm,),
        in_specs=[row(d), halo, mod, mod, full(g1), full(wmix), full(wq), full(wk), full(wv), full(wqi), full(wl),
                  full(wlt), full(conv_w), full(qg), full(kg), full(kig), row(LANES), row(LANES), row(LANES), full(avg)],
        out_specs=(row(CONV_DIM), row(ATTN_DIM), row(ATTN_DIM), row(ATTN_DIM), row(IDX_HEADS * IDX_DIM),
                   row(LANES), row(LANES), pl.BlockSpec((IDX_HEADS, tm), lambda i: (0, i))),
        compiler_params=_params("parallel"),
        name="inproj",
    )(x2, x2, scale1, shift1, g1, wmix, wq, wk, wv, wqi, wl, wlt, conv_w, qg, kg, kig, ct, s1, s2, avg)


def _ukey_to_f32(u):
    s = u ^ jnp.int32(-2 ** 31)
    bits = s ^ ((s >> 31) & jnp.int32(0x7FFFFFFF))
    return lax.bitcast_convert_type(bits, F32)


def _attn_kernel(q_ref, k_ref, v_ref, qi_ref, kilo_ref, kihi_ref, wit_ref, o_ref,
                 sc_ref, sc16_ref, qm_ref, acc_ref, m_ref, l_ref, *, n_sel):
    tq = q_ref.shape[0]
    seq = k_ref.shape[0]
    kc = sc_ref.shape[1]
    assert kc == tq
    lb_n = kc // LANES
    j = pl.program_id(1)
    nch = (j + 1) * (tq // kc)
    lane = lax.broadcasted_iota(jnp.int32, (tq, LANES), 1)

    for h in range(ATTN_HEADS):
        qp = q_ref[:, (h // 2) * LANES:(h // 2 + 1) * LANES].astype(F32)
        keep = (lane < HEAD_DIM) if h % 2 == 0 else (lane >= HEAD_DIM)
        qm_ref[h] = jnp.where(keep, qp, 0.0).astype(BF16)

    q_pos = j * tq + lax.broadcasted_iota(jnp.int32, (kc, tq), 1)
    k_off = lax.broadcasted_iota(jnp.int32, (kc, tq), 0)
    w_rows = wit_ref[...]

    def index_chunk(c, carry):
        r0 = pl.multiple_of(c * kc, kc)
        klo = kilo_ref[pl.ds(r0, kc), :]
        khi = kihi_ref[pl.ds(r0, kc), :]
        acc = jnp.zeros((kc, tq), F32)
        for p in range(IDX_HEADS // 2):
            qip = qi_ref[:, p * LANES:(p + 1) * LANES]
            acc = acc + jnp.maximum(_dot_t(klo, qip), 0.0) * w_rows[2 * p:2 * p + 1, :]
            acc = acc + jnp.maximum(_dot_t(khi, qip), 0.0) * w_rows[2 * p + 1:2 * p + 2, :]
        score = acc * IDX_SCALE
        score = jnp.where(r0 + k_off <= q_pos, score, -jnp.inf)
        score = jnp.where(score == 0.0, 0.0, score)
        sc_ref[c] = score
        top = lax.bitcast_convert_type(score, jnp.int32) & jnp.int32(-65536)
        sc16_ref[c] = lax.bitcast_convert_type(top, F32).astype(BF16)
        return carry

    lax.fori_loop(0, nch, index_chunk, 0)

    t_q = j * tq + lax.broadcasted_iota(jnp.int32, (1, tq), 1)
    k_row = jnp.minimum(t_q + 1, n_sel).astype(F32)
    acc_rows = 4 * SUBLANES
    sub = lax.broadcasted_iota(jnp.int32, (acc_rows, tq), 0)
    idx_bits = (seq - 1).bit_length()

    def count(pred):
        def chunk(c, acc):
            blk = sc_ref[c]
            for g in range(kc // acc_rows):
                kidx = sub + (c * kc + g * acc_rows)
                acc = acc + jnp.where(pred(blk[g * acc_rows:(g + 1) * acc_rows, :], kidx), 1.0, 0.0)
            return acc

        acc = lax.fori_loop(0, nch, chunk, jnp.zeros((acc_rows, tq), F32))
        return jnp.sum(acc, axis=0, keepdims=True)

    def rows8(v):
        return jnp.broadcast_to(v, (acc_rows, tq))

    def count16(cand16):
        def chunk(c, acc):
            blk = sc16_ref[c]
            for g in range(kc // acc_rows):
                hit = blk[g * acc_rows:(g + 1) * acc_rows, :] >= cand16
                acc = acc + jnp.where(hit, jnp.ones_like(cand16), jnp.zeros_like(cand16))
            return acc

        acc = lax.fori_loop(0, nch, chunk, jnp.zeros((acc_rows, tq), BF16))
        return jnp.sum(acc.astype(F32), axis=0, keepdims=True)

    def coarse_bit(i, carry):
        p, n_p = carry
        cand = p | jnp.left_shift(jnp.int32(1), 31 - i)
        top = lax.bitcast_convert_type(_ukey_to_f32(cand), jnp.int32) & jnp.int32(-65536)
        cnt = count16(rows8(lax.bitcast_convert_type(top, F32)).astype(BF16))
        keep = cnt >= k_row
        return jnp.where(keep, cand, p), jnp.where(keep, cnt, n_p)

    def value_bit(i, carry):
        p, n_p = carry
        cand = p | jnp.left_shift(jnp.int32(1), 31 - i)
        cand_f = rows8(_ukey_to_f32(cand))
        cnt = count(lambda v, kidx: v >= cand_f)
        keep = cnt >= k_row
        return jnp.where(keep, cand, p), jnp.where(keep, cnt, n_p)

    start = (jnp.zeros((1, tq), jnp.int32), jnp.zeros((1, tq), F32))
    p_thr, n_ge = lax.fori_loop(16, 32, value_bit, lax.fori_loop(0, 16, coarse_bit, start))
    thr = _ukey_to_f32(p_thr)
    thr8 = rows8(thr)
    tied = jnp.max(jnp.where(n_ge > k_row, 1.0, 0.0)) > 0.0

    def tie_cut():
        need = k_row - count(lambda v, kidx: v > thr8)

        def index_bit(i, p):
            cand = p | jnp.left_shift(jnp.int32(1), idx_bits - 1 - i)
            cand8 = rows8(cand)
            cnt = count(lambda v, kidx: jnp.where(v == thr8, kidx, seq) < cand8)
            return jnp.where(cnt < need, cand, p)

        return lax.fori_loop(0, idx_bits, index_bit, jnp.zeros((1, tq), jnp.int32))

    cut = lax.cond(tied, tie_cut, lambda: jnp.full((1, tq), seq, jnp.int32))

    def write_bias_tied(c, carry):
        blk = sc_ref[c]
        tie_bias = jnp.where(c * kc + k_off <= cut, 0.0, MASKED)
        sc_ref[c] = jnp.where(blk > thr, 0.0, jnp.where(blk == thr, tie_bias, MASKED)).T
        return carry

    def write_bias(c, carry):
        sc_ref[c] = jnp.where(sc_ref[c] >= thr, 0.0, MASKED).T
        return carry

    @pl.when(tied)
    def _():
        lax.fori_loop(0, nch, write_bias_tied, 0)

    @pl.when(jnp.logical_not(tied))
    def _():
        lax.fori_loop(0, nch, write_bias, 0)

    for h in range(ATTN_HEADS):
        m_ref[h] = jnp.full((tq, LANES), MASKED, F32)
        l_ref[h] = jnp.zeros((tq, LANES), F32)
        acc_ref[h] = jnp.zeros((tq, LANES), F32)

    def attend(c, carry):
        r0 = pl.multiple_of(c * kc, kc)
        keys = pl.ds(r0, kc)
        bias = sc_ref[c]

        def qk(h):
            return _dot_t(qm_ref[h], k_ref[keys, (h // 2) * LANES:(h // 2 + 1) * LANES])

        s_next = qk(0)
        for h in range(ATTN_HEADS):
            s = s_next + bias
            if h + 1 < ATTN_HEADS:
                s_next = qk(h + 1)
            parts = [s[:, b * LANES:(b + 1) * LANES] for b in range(lb_n)]
            m_old = m_ref[h]
            row_max = jnp.max(functools.reduce(jnp.maximum, parts), axis=1, keepdims=True)
            m_new = jnp.maximum(m_old, row_max)
            alpha = jnp.exp2(m_old - m_new)
            p_parts = [jnp.exp2(part - m_new) for part in parts]
            l_ref[h] = alpha * l_ref[h] + functools.reduce(jnp.add, p_parts)
            p = jnp.concatenate(p_parts, axis=1).astype(BF16)
            acc_ref[h] = alpha * acc_ref[h] + _dot(p, v_ref[keys, (h // 2) * LANES:(h // 2 + 1) * LANES])
            m_ref[h] = m_new
        return carry

    lax.fori_loop(0, nch, attend, 0)
    for pair in range(ATTN_HEADS // 2):
        l_even = jnp.sum(l_ref[2 * pair], axis=1, keepdims=True)
        l_odd = jnp.sum(l_ref[2 * pair + 1], axis=1, keepdims=True)
        o_pair = jnp.where(lane < HEAD_DIM, acc_ref[2 * pair] / l_even, acc_ref[2 * pair + 1] / l_odd)
        o_ref[:, pair * LANES:(pair + 1) * LANES] = o_pair.astype(BF16)


def _attention(q, k, v, qi, kilo, kihi, wit, bsz, seq, tq, kc):
    n_sel = min(TOPK_KEYS_MAX, seq // 4)
    shape3 = lambda a: a.reshape(bsz, seq, a.shape[-1])
    q, k, v, qi, kilo, kihi = map(shape3, (q, k, v, qi, kilo, kihi))
    qblk = lambda w: pl.BlockSpec((None, tq, w), lambda b, j: (b, j, 0))
    kblk = lambda w: pl.BlockSpec((None, seq, w), lambda b, j: (b, 0, 0), pipeline_mode=pl.Buffered(1))
    out = pl.pallas_call(
        functools.partial(_attn_kernel, n_sel=n_sel),
        out_shape=jax.ShapeDtypeStruct((bsz, seq, ATTN_DIM), BF16),
        grid=(bsz, seq // tq),
        in_specs=[qblk(ATTN_DIM), kblk(ATTN_DIM), kblk(ATTN_DIM), qblk(IDX_HEADS * IDX_DIM), kblk(LANES), kblk(LANES),
                  pl.BlockSpec((IDX_HEADS, tq), lambda b, j: (0, b * (seq // tq) + j))],
        out_specs=qblk(ATTN_DIM),
        scratch_shapes=[
            pltpu.VMEM((seq // kc, kc, tq), F32),
            pltpu.VMEM((seq // kc, kc, tq), BF16),
            pltpu.VMEM((ATTN_HEADS, tq, LANES), BF16),
            pltpu.VMEM((ATTN_HEADS, tq, LANES), F32),
            pltpu.VMEM((ATTN_HEADS, tq, LANES), F32),
            pltpu.VMEM((ATTN_HEADS, tq, LANES), F32),
        ],
        compiler_params=_params("parallel", "arbitrary"),
        name="dsa_attention",
    )(q, k, v, qi, kilo, kihi, wit)
    return out.reshape(bsz * seq, ATTN_DIM)


def _outproj_kernel(conv_ref, attn_ref, x_ref, gate1_ref, sc_ref, sh_ref, g2_ref, wout_ref, wrt_ref, rbias_ref,
                    x1_o, h2_o, h2pa_o, h2pb_o, eidx_o, g8_o, cnt_o):
    mix = _dot(conv_ref[...], wout_ref[:CONV_DIM, :]) + _dot(attn_ref[...], wout_ref[CONV_DIM:, :])
    x1 = x_ref[...] + gate1_ref[0] * mix
    x1_o[...] = x1
    h2 = _rms_mod(x1, g2_ref[...], sc_ref[0], sh_ref[0])
    h2b = h2.astype(BF16)
    h2_o[...] = h2b
    quarter = h2.shape[1] // 4
    bits = lax.bitcast_convert_type(h2b.astype(F32), jnp.int32)
    for half_o, c0 in ((h2pa_o, 0), (h2pb_o, 2 * quarter)):
        low = lax.shift_right_logical(bits[:, c0:c0 + quarter], 16)
        half_o[...] = low | (bits[:, c0 + quarter:c0 + 2 * quarter] & jnp.int32(-65536))

    wrt = wrt_ref[...]
    wrt_hi, wrt_lo = _split(wrt)
    h2_hi, h2_lo = _split(h2)
    scores = jax.nn.sigmoid(_dot_t(wrt_hi, h2_hi) + _dot_t(wrt_hi, h2_lo) + _dot_t(wrt_lo, h2_hi))
    work = scores + rbias_ref[...]
    n_e, tm = work.shape
    row_in_tile = lax.broadcasted_iota(jnp.int32, (SUBLANES, tm), 0)
    tiles = [work[j * SUBLANES:(j + 1) * SUBLANES, :] for j in range(n_e // SUBLANES)]
    rank = [jnp.zeros((SUBLANES, tm), F32) for _ in tiles]
    for e2 in range(n_e):
        other = jnp.broadcast_to(work[e2:e2 + 1, :], (SUBLANES, tm))
        for j, tile in enumerate(tiles):
            if j > e2 // SUBLANES:
                beats = other >= tile
            elif j < e2 // SUBLANES:
                beats = other > tile
            else:
                later = row_in_tile > e2 % SUBLANES
                beats = jnp.where(later, jnp.where(other >= tile, 1.0, 0.0), jnp.where(other > tile, 1.0, 0.0)) > 0.0
            rank[j] = rank[j] + jnp.where(beats, 1.0, 0.0)
    chosen = jnp.concatenate([jnp.where(r < float(TOP_K_EXPERTS), 1.0, 0.0) for r in rank], axis=0)

    @pl.when(pl.program_id(0) == 0)
    def _():
        cnt_o[...] = jnp.zeros(cnt_o.shape, F32)

    cnt_o[...] += jnp.sum(chosen, axis=1, keepdims=True)
    eidx_rows, pick_rows = [], []
    for k in range(TOP_K_EXPERTS):
        e_sum = jnp.zeros((SUBLANES, tm), F32)
        s_sum = jnp.zeros((SUBLANES, tm), F32)
        for j in range(len(tiles)):
            is_k = rank[j] == float(k)
            e_sum = e_sum + jnp.where(is_k, (row_in_tile + j * SUBLANES).astype(F32), 0.0)
            s_sum = s_sum + jnp.where(is_k, scores[j * SUBLANES:(j + 1) * SUBLANES, :], 0.0)
        eidx_rows.append(jnp.sum(e_sum, axis=0, keepdims=True))
        pick_rows.append(jnp.sum(s_sum, axis=0, keepdims=True))
    eidx_t = jnp.concatenate(eidx_rows, axis=0)
    picked_t = jnp.concatenate(pick_rows, axis=0)
    g8_t = picked_t / jnp.sum(picked_t, axis=0, keepdims=True) * ROUTED_SCALE
    eidx_o[...] = eidx_t.T.astype(jnp.int32)
    g8_o[...] = g8_t.T


def _outproj(conv, attn, x2, gate1, scale2, shift2, norm2_g, w_out, w_router, router_bias, seq, tm):
    t, d = x2.shape
    e = w_router.shape[1]
    per_b = seq // tm
    row = lambda w: pl.BlockSpec((tm, w), lambda i: (i, 0))
    full = lambda a: pl.BlockSpec(a.shape, lambda i: (0,) * a.ndim)
    mod = pl.BlockSpec((1, 1, d), lambda i: (i // per_b, 0, 0))
    g2 = norm2_g.reshape(1, d)
    wo = w_out.astype(BF16)
    rbias = router_bias.reshape(e, 1)
    wrt = w_router.T
    return pl.pallas_call(
        _outproj_kernel,
        out_shape=(jax.ShapeDtypeStruct((t, d), F32), jax.ShapeDtypeStruct((t, d), BF16),
                   jax.ShapeDtypeStruct((t, d // 4), jnp.int32), jax.ShapeDtypeStruct((t, d // 4), jnp.int32),
                   jax.ShapeDtypeStruct((t, TOP_K_EXPERTS), jnp.int32), jax.ShapeDtypeStruct((t, TOP_K_EXPERTS), F32),
                   jax.ShapeDtypeStruct((e, LANES), F32)),
        grid=(t // tm,),
        in_specs=[row(CONV_DIM), row(ATTN_DIM), row(d), mod, mod, mod, full(g2), full(wo), full(wrt), full(rbias)],
        out_specs=(row(d), row(d), row(d // 4), row(d // 4), row(TOP_K_EXPERTS), row(TOP_K_EXPERTS),
                   pl.BlockSpec((e, LANES), lambda i: (0, 0))),
        compiler_params=_params("arbitrary"),
        name="outproj_router",
    )(conv, attn, x2, gate1, scale2, shift2, g2, wo, wrt, rbias)


def _route_kernel(eidx_ref, cnt_ref, earlier_ref, dest_o, run_ref, *, bm):
    tm = eidx_ref.shape[0]
    e8 = eidx_ref[...]
    lane = lax.broadcasted_iota(jnp.int32, (tm, N_EXPERTS), 1)
    hits = [lane == e8[:, k:k + 1] for k in range(TOP_K_EXPERTS)]
    member = functools.reduce(jnp.add, [jnp.where(hit, 1.0, 0.0) for hit in hits])

    @pl.when(pl.program_id(0) == 0)
    def _():
        blocks = jnp.ceil(cnt_ref[...] / bm)
        r = lax.broadcasted_iota(jnp.int32, (N_EXPERTS, N_EXPERTS), 0)
        c = lax.broadcasted_iota(jnp.int32, (N_EXPERTS, N_EXPERTS), 1)
        before = jnp.where(r < c, 1.0, 0.0).astype(BF16)
        b_hi, b_lo = _split(blocks)
        run_ref[...] = (_dot(b_hi, before) + _dot(b_lo, before)) * bm

    base = run_ref[0:1, :] + _dot(earlier_ref[...], member.astype(BF16))
    slot = lax.broadcasted_iota(jnp.int32, (tm, TOP_K_EXPERTS), 1)
    dest = jnp.zeros((tm, TOP_K_EXPERTS), F32)
    for k in range(TOP_K_EXPERTS):
        dest = jnp.where(slot == k, jnp.sum(jnp.where(hits[k], base, 0.0), axis=1, keepdims=True), dest)
    dest_o[...] = dest.astype(jnp.int32)
    run_ref[...] += jnp.sum(member, axis=0, keepdims=True)


def _route(eidx, counts, bm, tm):
    nt = eidx.shape[0]
    cnt8 = jnp.broadcast_to(counts.reshape(1, N_EXPERTS), (SUBLANES, N_EXPERTS))
    earlier = jnp.tri(tm, k=-1, dtype=BF16)
    return pl.pallas_call(
        functools.partial(_route_kernel, bm=bm),
        out_shape=jax.ShapeDtypeStruct((nt, TOP_K_EXPERTS), jnp.int32),
        grid=(nt // tm,),
        in_specs=[pl.BlockSpec((tm, TOP_K_EXPERTS), lambda i: (i, 0)),
                  pl.BlockSpec((SUBLANES, N_EXPERTS), lambda i: (0, 0)),
                  pl.BlockSpec((tm, tm), lambda i: (0, 0))],
        out_specs=pl.BlockSpec((tm, TOP_K_EXPERTS), lambda i: (i, 0)),
        scratch_shapes=[pltpu.VMEM((SUBLANES, N_EXPERTS), F32)],
        compiler_params=_params("arbitrary"),
        name="moe_route",
    )(eidx, cnt8, earlier)


def _sc_mesh():
    return plsc.VectorSubcoreMesh(core_axis_name="core", subcore_axis_name="subcore", num_cores=V7X_SC_CORES,
                                  num_subcores=V7X_SC_SUBCORES)


def _sc_scatter_rows(rows, dest_kt, n_out):
    t, d = rows.shape
    n_k = dest_kt.shape[0]
    window = SC_WINDOW

    @functools.partial(pl.kernel, out_type=jax.ShapeDtypeStruct((n_out, d), rows.dtype), mesh=_sc_mesh(),
                       name="moe_dispatch_scatter")
    def scatter(x_hbm, i_hbm, o_hbm):
        def body(x_vmem, i_vmem):
            for k in range(n_k):
                pltpu.sync_copy(x_vmem, o_hbm.at[i_vmem.at[k]])

        pltpu.emit_pipeline(
            body,
            grid=(t // window,),
            in_specs=[pl.BlockSpec((window, d), lambda i: (i, 0)), pl.BlockSpec((n_k, window), lambda i: (0, i))],
            out_specs=[],
            core_axis_name=("core", "subcore"),
            dimension_semantics=(pltpu.PARALLEL,),
        )(x_hbm, i_hbm)

    return scatter(rows, dest_kt)


def _sc_gather_rows(table, idx):
    n = idx.shape[0]
    d = table.shape[1]
    window = SC_WINDOW

    @functools.partial(pl.kernel, out_type=jax.ShapeDtypeStruct((n, d), table.dtype), mesh=_sc_mesh(),
                       name="moe_combine_gather")
    def gather(tab_hbm, i_hbm, o_hbm):
        def body(i_vmem, o_vmem):
            pltpu.sync_copy(tab_hbm.at[i_vmem.at[0]], o_vmem)

        pltpu.emit_pipeline(
            body,
            grid=(n // window,),
            in_specs=[pl.BlockSpec((1, window), lambda i: (0, i))],
            out_specs=[pl.BlockSpec((window, d), lambda i: (i, 0))],
            core_axis_name=("core", "subcore"),
            dimension_semantics=(pltpu.PARALLEL,),
        )(i_hbm, o_hbm)

    return gather(table, idx.reshape(1, n))


def _ffn_kernel(be_ref, nused_ref, xa_ref, xb_ref, w1_ref, w3_ref, w2_ref, *refs):
    ys_refs, (w13_s, w2_s) = refs[:-2], refs[-2:]
    b = pl.program_id(0)
    f = w2_ref.shape[1]

    @pl.when((b == 0) | (be_ref[b] != be_ref[jnp.maximum(b - 1, 0)]))
    def _():
        w13_s[:, :f] = w1_ref[0].astype(BF16)
        w13_s[:, f:] = w3_ref[0].astype(BF16)
        w2_s[...] = w2_ref[0].astype(BF16)

    @pl.when(b < nused_ref[0])
    def _():
        cols = []
        for x_ref in (xa_ref, xb_ref):
            xu = x_ref[...]
            cols.append(lax.bitcast_convert_type(lax.shift_left(xu, 16), F32).astype(BF16))
            cols.append(lax.bitcast_convert_type(xu & jnp.int32(-65536), F32).astype(BF16))
        hh = _dot(jnp.concatenate(cols, axis=1), w13_s[...])
        act = _silu(hh[:, :f]) * hh[:, f:]
        y = _dot(act.astype(BF16), w2_s[...])
        bits = lax.bitcast_convert_type(y.astype(BF16).astype(F32), jnp.int32)
        for j, y_ref in enumerate(ys_refs):
            c0 = 2 * j * SC_ROW_WORDS
            low = lax.shift_right_logical(bits[:, c0:c0 + SC_ROW_WORDS], 16)
            y_ref[...] = low | (bits[:, c0 + SC_ROW_WORDS:c0 + 2 * SC_ROW_WORDS] & jnp.int32(-65536))


def _ffn(block_e, n_used, xs_a, xs_b, w1, w3, w2, bm):
    rows, q = xs_a.shape
    _, d, f = w1.shape
    n_out = d // (2 * SC_ROW_WORDS)
    grid_spec = pltpu.PrefetchScalarGridSpec(
        num_scalar_prefetch=2,
        grid=(rows // bm,),
        in_specs=[pl.BlockSpec((bm, q), lambda b, be, nu: (b, 0)),
                  pl.BlockSpec((bm, q), lambda b, be, nu: (b, 0)),
                  pl.BlockSpec((1, d, f), lambda b, be, nu: (be[b], 0, 0)),
                  pl.BlockSpec((1, d, f), lambda b, be, nu: (be[b], 0, 0)),
                  pl.BlockSpec((1, f, d), lambda b, be, nu: (be[b], 0, 0))],
        out_specs=tuple(pl.BlockSpec((bm, SC_ROW_WORDS), lambda b, be, nu: (b, 0)) for _ in range(n_out)),
        scratch_shapes=[pltpu.VMEM((d, 2 * f), BF16), pltpu.VMEM((f, d), BF16)],
    )
    return pl.pallas_call(
        _ffn_kernel,
        out_shape=tuple(jax.ShapeDtypeStruct((rows, SC_ROW_WORDS), jnp.int32) for _ in range(n_out)),
        grid_spec=grid_spec,
        compiler_params=_params("arbitrary"),
        name="moe_expert_ffn",
    )(block_e, n_used, xs_a, xs_b, w1, w3, w2)


def _combine_kernel(*refs):
    y_refs = refs[:-7]
    g8_ref, h2_ref, x1_ref, gate2_ref, ws13_ref, ws2_ref, o_ref = refs[-7:]
    f = ws2_ref.shape[0]
    hs = _dot(h2_ref[...], ws13_ref[...])
    acc = _dot((_silu(hs[:, :f]) * hs[:, f:]).astype(BF16), ws2_ref[...])
    g8 = g8_ref[...]
    for k in range(TOP_K_EXPERTS):
        cols = []
        for y_ref in y_refs:
            yu = y_ref[k]
            cols.append(lax.bitcast_convert_type(lax.shift_left(yu, 16), F32))
            cols.append(lax.bitcast_convert_type(yu & jnp.int32(-65536), F32))
        acc = acc + g8[:, k:k + 1] * jnp.concatenate(cols, axis=1)
    o_ref[...] = x1_ref[...] + gate2_ref[0] * acc


def _combine(y8s, g8, h2, x1, gate2, ws13, ws2b, seq, tm):
    t, d = x1.shape
    per_b = seq // tm
    row = lambda w: pl.BlockSpec((tm, w), lambda i: (i, 0))
    full = lambda a: pl.BlockSpec(a.shape, lambda i: (0,) * a.ndim)
    y_spec = pl.BlockSpec((TOP_K_EXPERTS, tm, SC_ROW_WORDS), lambda i: (0, i, 0))
    return pl.pallas_call(
        _combine_kernel,
        out_shape=jax.ShapeDtypeStruct((t, d), F32),
        grid=(t // tm,),
        in_specs=[y_spec] * len(y8s) + [row(TOP_K_EXPERTS), row(d), row(d),
                                         pl.BlockSpec((1, 1, d), lambda i: (i // per_b, 0, 0)), full(ws13), full(ws2b)],
        out_specs=row(d),
        compiler_params=_params("parallel"),
        name="moe_combine",
    )(*y8s, g8, h2, x1, gate2, ws13, ws2b)


def _moe(h2, h2pa, h2pb, x1, eidx, g8, counts, gate2, w1, w3, w2, ws1, ws3, ws2, seq):
    t, d = x1.shape
    n_e = w1.shape[0]
    bm = MOE_BLOCK_ROWS
    rows = t * TOP_K_EXPERTS + n_e * bm
    ws13 = jnp.concatenate([ws1, ws3], axis=1).astype(BF16)
    ws2b = ws2.astype(BF16)

    dest = _route(eidx, counts, bm, min(512, t))
    pend = jnp.cumsum(jnp.ceil(counts / bm) * bm)
    block_row0 = jnp.arange(rows // bm, dtype=F32) * bm
    block_e = jnp.minimum(jnp.sum(pend[None, :] <= block_row0[:, None], axis=1), n_e - 1)
    n_used = (pend[-1:] / bm).astype(jnp.int32)
    dest_kt = dest.T

    xs_a = _sc_scatter_rows(h2pa, dest_kt, rows)
    xs_b = _sc_scatter_rows(h2pb, dest_kt, rows)
    ys = _ffn(block_e.astype(jnp.int32), n_used, xs_a, xs_b, w1, w3, w2, bm)
    pair_rows = dest_kt.reshape(-1)
    y8s = [_sc_gather_rows(y, pair_rows).reshape(TOP_K_EXPERTS, t, SC_ROW_WORDS) for y in ys]
    return _combine(y8s, g8, h2, x1, gate2, ws13, ws2b, seq, min(256, seq))


def _layer(x, c, positions, norm1_g, norm2_g, w_ada, b_ada, w_in, conv_w, q_norm_g, k_norm_g, kidx_norm_g, w_out,
           w_router, router_bias, w1, w3, w2, ws1, ws3, ws2):
    bsz, seq, d = x.shape
    t = bsz * seq
    tm = min(512, seq)
    tq = min(512, seq)
    x2 = x.reshape(t, d)

    ada = _ada(c, w_ada, b_ada)
    shift1, scale1, gate1, shift2, scale2, gate2 = [a.reshape(bsz, 1, d) for a in jnp.split(ada, 6, axis=-1)]

    inv_freq = ROPE_THETA ** (-jnp.arange(ROPE_HALF, dtype=F32) / ROPE_HALF)
    invf_lane = inv_freq[(jnp.arange(LANES) % HEAD_DIM) % ROPE_HALF].reshape(1, LANES)
    ct, s1, s2 = _rope_tables(positions.reshape(t, 1), invf_lane, min(2048, t))

    conv, q, k, v, qi, kilo, kihi, wit = _inproj(x2, scale1, shift1, norm1_g, w_in, conv_w, q_norm_g, k_norm_g,
                                                kidx_norm_g, ct, s1, s2, seq, tm)
    attn = _attention(q, k, v, qi, kilo, kihi, wit, bsz, seq, tq, kc=tq)
    x1, h2, h2pa, h2pb, eidx, g8, cnt = _outproj(conv, attn, x2, gate1, scale2, shift2, norm2_g, w_out, w_router,
                                                 router_bias, seq, tm)
    out = _moe(h2, h2pa, h2pb, x1, eidx, g8, cnt[:, 0], gate2, w1, w3, w2, ws1, ws3, ws2, seq)
    return out.reshape(bsz, seq, d)


def kernel(x, c, positions, norm1_g, norm2_g, w_ada, b_ada, w_in, conv_w, q_norm_g, k_norm_g, kidx_norm_g, w_out,
           w_router, router_bias, w1, w3, w2, ws1, ws3, ws2):
    for l in range(w_in.shape[0]):
        x = _layer(x, c, positions, norm1_g[l], norm2_g[l], w_ada[l], b_ada[l], w_in[l], conv_w[l], q_norm_g[l],
                   k_norm_g[l], kidx_norm_g[l], w_out[l], w_router[l], router_bias[l], w1[l], w3[l], w2[l], ws1[l],
                   ws3[l], ws2[l])
    return x
```

```python
import functools

import jax
import jax.numpy as jnp
from jax import lax
from jax.experimental import pallas as pl
from jax.experimental.pallas import tpu as pltpu
from jax.experimental.pallas import tpu_sc as plsc

F32 = jnp.float32
BF16 = jnp.bfloat16

HEAD_DIM = 64
ATTN_HEADS = 8
ATTN_DIM = ATTN_HEADS * HEAD_DIM
CONV_DIM = 512
IDX_HEADS = 8
IDX_DIM = 64
IDX_SCALE = (IDX_DIM ** -0.5) * (IDX_HEADS ** -0.5)
TOPK_KEYS_MAX = 256
ROPE_THETA = 500000.0
ROPE_DIM = HEAD_DIM // 4
ROPE_HALF = ROPE_DIM // 2
N_EXPERTS = 64
TOP_K_EXPERTS = 8
EXPERT_DIM = 256
ROUTED_SCALE = 2.5
EPS = 1e-6

LANES = 128
SUBLANES = 8
V7X_VMEM_BYTES = 64 * 1024 * 1024
VMEM_LIMIT = V7X_VMEM_BYTES * 3 // 4
V7X_SC_CORES = 2
V7X_SC_SUBCORES = 16
SC_WINDOW = 128
SC_ROW_WORDS = 256

TOKEN_TILE = 512
QUERY_TILE = 512
COMBINE_TILE = 256
ROPE_TILE = 2048

HIGH_HALF = -65536
MASKED = -1e30
Q_SCALE = HEAD_DIM ** -0.5 * 1.4426950408889634
MOE_BLOCK_ROWS = 1024


def _params(*semantics):
    return pltpu.CompilerParams(dimension_semantics=semantics, vmem_limit_bytes=VMEM_LIMIT)


def _dot(a, b):
    return jnp.dot(a, b, preferred_element_type=F32)


def _dot_t(a, b):
    return lax.dot_general(a, b, (((1,), (1,)), ((), ())), preferred_element_type=F32)


def _split(a):
    hi = a.astype(BF16)
    lo = (a - hi.astype(F32)).astype(BF16)
    return hi, lo


def _dot3(a, b):
    a_hi, a_lo = _split(a)
    b_hi, b_lo = _split(b)
    return _dot(a_hi, b_hi) + _dot(a_hi, b_lo) + _dot(a_lo, b_hi)


def _silu(v):
    return v * jax.nn.sigmoid(v)


def _rms_mod(xv, g, scale, shift):
    ms = jnp.mean(xv * xv, axis=-1, keepdims=True)
    y = xv * lax.rsqrt(ms + EPS)
    return (y * g) * (1.0 + scale) + shift


def _ada_kernel(c_ref, w_ref, b_ref, o_ref):
    o_ref[...] = _dot3(_silu(c_ref[...]), w_ref[...]) + b_ref[...]


def _ada(c, w_ada, b_ada):
    bsz, d = c.shape
    n = w_ada.shape[1]
    bn = n // 4
    return pl.pallas_call(
        _ada_kernel,
        out_shape=jax.ShapeDtypeStruct((bsz, n), F32),
        grid=(n // bn,),
        in_specs=[
            pl.BlockSpec((bsz, d), lambda i: (0, 0)),
            pl.BlockSpec((d, bn), lambda i: (0, i)),
            pl.BlockSpec((1, bn), lambda i: (0, i)),
        ],
        out_specs=pl.BlockSpec((bsz, bn), lambda i: (0, i)),
        compiler_params=_params("parallel"),
        name="ada",
    )(c, w_ada, b_ada.reshape(1, n))


def _rope_kernel(pos_ref, invf_ref, c_ref, s1_ref, s2_ref):
    ang = pos_ref[...].astype(F32) * invf_ref[...]
    d = lax.broadcasted_iota(jnp.int32, ang.shape, 1) & (HEAD_DIM - 1)
    cos = jnp.cos(ang)
    sin = jnp.sin(ang)
    c_ref[...] = jnp.where(d < ROPE_DIM, cos, 1.0)
    s1_ref[...] = jnp.where(d < ROPE_HALF, -sin, 0.0)
    s2_ref[...] = jnp.where(d < ROPE_HALF, 0.0, jnp.where(d < ROPE_DIM, sin, 0.0))


def _rope_tables(pos, invf_lane, tm):
    t = pos.shape[0]
    spec = pl.BlockSpec((tm, LANES), lambda i: (i, 0))
    shp = jax.ShapeDtypeStruct((t, LANES), F32)
    return pl.pallas_call(
        _rope_kernel,
        out_shape=(shp, shp, shp),
        grid=(t // tm,),
        in_specs=[pl.BlockSpec((tm, 1), lambda i: (i, 0)), pl.BlockSpec((1, LANES), lambda i: (0, 0))],
        out_specs=(spec, spec, spec),
        compiler_params=_params("parallel"),
        name="rope_tables",
    )(pos, invf_lane)


def _rope(y, c, s1, s2):
    return y * c + pltpu.roll(y, LANES - ROPE_HALF, 1) * s1 + pltpu.roll(y, ROPE_HALF, 1) * s2


def _head_rms(xb, avg):
    hi, lo = _split(xb * xb)
    ms = _dot(hi, avg) + _dot(lo, avg)
    return xb * lax.rsqrt(ms + EPS)


def _inproj_kernel(x_ref, xh_ref, sc_ref, sh_ref, g1_ref, wmix_ref, wq_ref, wk_ref, wv_ref, wqi_ref, wl_ref, wlt_ref,
                   cw_ref, qg_ref, kg_ref, kig_ref, ct_ref, s1_ref, s2_ref, avg_ref,
                   conv_o, q_o, k_o, v_o, qi_o, kilo_o, kihi_o, wit_o, *, seq):
    tm = x_ref.shape[0]
    scale = sc_ref[0]
    shift = sh_ref[0]
    g1 = g1_ref[...]
    h = _rms_mod(x_ref[...], g1, scale, shift).astype(BF16)
    hh = _rms_mod(xh_ref[...], g1, scale, shift).astype(BF16)

    mix = _dot(h, wmix_ref[...])
    mixh = _dot(hh, wmix_ref[...])
    u = mix[:, 2 * CONV_DIM:] * mix[:, :CONV_DIM]
    uh = mixh[:, 2 * CONV_DIM:] * mixh[:, :CONV_DIM]
    seq_start = (pl.program_id(0) * tm) % seq == 0
    uh = jnp.where(seq_start, 0.0, uh)
    ext = jnp.concatenate([uh, u], axis=0)
    u1 = pltpu.roll(ext, 1, 0)[SUBLANES:]
    u2 = pltpu.roll(ext, 2, 0)[SUBLANES:]
    conv = u2 * cw_ref[0:1, :] + u1 * cw_ref[1:2, :] + u * cw_ref[2:3, :]
    conv_o[...] = (mix[:, CONV_DIM:2 * CONV_DIM] * conv).astype(BF16)

    ct = ct_ref[...]
    s1 = s1_ref[...]
    s2 = s2_ref[...]
    avg = avg_ref[...]
    qf = _dot(h, wq_ref[...])
    kf = _dot(h, wk_ref[...])
    qif = _dot(h, wqi_ref[...])
    for p in range(ATTN_DIM // LANES):
        sl = slice(p * LANES, (p + 1) * LANES)
        qn = _rope(_head_rms(qf[:, sl], avg) * qg_ref[...], ct, s1, s2)
        q_o[:, sl] = (qn * Q_SCALE).astype(BF16)
        kn = _rope(_head_rms(kf[:, sl], avg) * kg_ref[...], ct, s1, s2)
        k_o[:, sl] = kn.astype(BF16)
        qi_o[:, sl] = _rope(qif[:, sl], ct, s1, s2).astype(BF16)
    v_o[...] = _dot(h, wv_ref[...]).astype(BF16)

    wit_o[...] = _dot_t(wlt_ref[...], h)[IDX_DIM:IDX_DIM + IDX_HEADS, :]
    last = _dot(h, wl_ref[...])
    lane = lax.broadcasted_iota(jnp.int32, last.shape, 1)
    is_key = lane < IDX_DIM
    kin = _head_rms(last, avg) * kig_ref[...]
    kin = _rope(kin, jnp.where(is_key, ct, 1.0), jnp.where(is_key, s1, 0.0), jnp.where(is_key, s2, 0.0))
    klo = jnp.where(is_key, kin, 0.0)
    kilo_o[...] = klo.astype(BF16)
    kihi_o[...] = pltpu.roll(klo, IDX_DIM, 1).astype(BF16)


def _inproj(x2, scale1, shift1, norm1_g, w_in, conv_w, q_norm_g, k_norm_g, kidx_norm_g, ct, s1, s2, seq, tm):
    t, d = x2.shape
    cuts = [0, 3 * CONV_DIM, 3 * CONV_DIM + ATTN_DIM, 3 * CONV_DIM + 2 * ATTN_DIM, 3 * CONV_DIM + 3 * ATTN_DIM,
            3 * CONV_DIM + 3 * ATTN_DIM + IDX_HEADS * IDX_DIM]
    wb = w_in.astype(BF16)
    wmix, wq, wk, wv, wqi = [wb[:, a:b] for a, b in zip(cuts[:-1], cuts[1:])]
    wl = wb[:, cuts[-1]:]
    wl = jnp.pad(wl, ((0, 0), (0, LANES - wl.shape[1])))
    wlt = wl.T
    ones = jnp.ones((1, LANES - IDX_DIM), F32)
    qg = jnp.tile(q_norm_g.reshape(1, HEAD_DIM), (1, 2))
    kg = jnp.tile(k_norm_g.reshape(1, HEAD_DIM), (1, 2))
    kig = jnp.concatenate([kidx_norm_g.reshape(1, IDX_DIM), ones], axis=1)
    blk = jnp.arange(LANES) // HEAD_DIM
    avg = jnp.where(blk[:, None] == blk[None, :], 1.0 / HEAD_DIM, 0.0).astype(BF16)

    bsz = t // seq
    per_b = seq // tm
    row = lambda w: pl.BlockSpec((tm, w), lambda i: (i, 0))
    full = lambda a: pl.BlockSpec(a.shape, lambda i: (0,) * a.ndim)
    mod = pl.BlockSpec((1, 1, d), lambda i: (i // per_b, 0, 0))
    halo = pl.BlockSpec((SUBLANES, d), lambda i: (jnp.maximum(i * (tm // SUBLANES) - 1, 0), 0))
    g1 = norm1_g.reshape(1, d)
    out_shape = (
        jax.ShapeDtypeStruct((t, CONV_DIM), BF16),
        jax.ShapeDtypeStruct((t, ATTN_DIM), BF16),
        jax.ShapeDtypeStruct((t, ATTN_DIM), BF16),
        jax.ShapeDtypeStruct((t, ATTN_DIM), BF16),
        jax.ShapeDtypeStruct((t, IDX_HEADS * IDX_DIM), BF16),
        jax.ShapeDtypeStruct((t, LANES), BF16),
        jax.ShapeDtypeStruct((t, LANES), BF16),
        jax.ShapeDtypeStruct((IDX_HEADS, t), F32),
    )
    del bsz
    return pl.pallas_call(
        functools.partial(_inproj_kernel, seq=seq),
        out_shape=out_shape,
        grid=(t // tm,),
        in_specs=[row(d), halo, mod, mod, full(g1), full(wmix), full(wq), full(wk), full(wv), full(wqi), full(wl),
                  full(wlt), full(conv_w), full(qg), full(kg), full(kig), row(LANES), row(LANES), row(LANES), full(avg)],
        out_specs=(row(CONV_DIM), row(ATTN_DIM), row(ATTN_DIM), row(ATTN_DIM), row(IDX_HEADS * IDX_DIM),
                   row(LANES), row(LANES), pl.BlockSpec((IDX_HEADS, tm), lambda i: (0, i))),
        compiler_params=_params("parallel"),
        name="inproj",
    )(x2, x2, scale1, shift1, g1, wmix, wq, wk, wv, wqi, wl, wlt, conv_w, qg, kg, kig, ct, s1, s2, avg)


def _ukey_to_f32(u):
    s = u ^ jnp.int32(-2 ** 31)
    bits = s ^ ((s >> 31) & jnp.int32(0x7FFFFFFF))
    return lax.bitcast_convert_type(bits, F32)


def _attn_kernel(q_ref, k_ref, v_ref, qi_ref, kilo_ref, kihi_ref, wit_ref, o_ref,
                 sc_ref, sc16_ref, qm_ref, acc_ref, m_ref, l_ref, *, n_sel):
    tq = q_ref.shape[0]
    seq = k_ref.shape[0]
    kc = sc_ref.shape[1]
    assert kc == tq
    lb_n = kc // LANES
    j = pl.program_id(1)
    nch = (j + 1) * (tq // kc)
    lane = lax.broadcasted_iota(jnp.int32, (tq, LANES), 1)

    for h in range(ATTN_HEADS):
        qp = q_ref[:, (h // 2) * LANES:(h // 2 + 1) * LANES].astype(F32)
        keep = (lane < HEAD_DIM) if h % 2 == 0 else (lane >= HEAD_DIM)
        qm_ref[h] = jnp.where(keep, qp, 0.0).astype(BF16)

    q_pos = j * tq + lax.broadcasted_iota(jnp.int32, (kc, tq), 1)
    k_off = lax.broadcasted_iota(jnp.int32, (kc, tq), 0)
    w_rows = wit_ref[...]

    def index_chunk(c, diagonal):
        r0 = pl.multiple_of(c * kc, kc)
        klo = kilo_ref[pl.ds(r0, kc), :]
        khi = kihi_ref[pl.ds(r0, kc), :]
        acc = jnp.zeros((kc, tq), F32)
        for p in range(IDX_HEADS // 2):
            qip = qi_ref[:, p * LANES:(p + 1) * LANES]
            acc = acc + jnp.maximum(_dot_t(klo, qip), 0.0) * w_rows[2 * p:2 * p + 1, :]
            acc = acc + jnp.maximum(_dot_t(khi, qip), 0.0) * w_rows[2 * p + 1:2 * p + 2, :]
        score = acc * IDX_SCALE
        if diagonal:
            score = jnp.where(r0 + k_off <= q_pos, score, -jnp.inf)
        score = jnp.where(score == 0.0, 0.0, score)
        sc_ref[c] = score
        top = lax.bitcast_convert_type(score, jnp.int32) & HIGH_HALF
        sc16_ref[c] = lax.bitcast_convert_type(top, F32).astype(BF16)

    def index_full_chunk(c, carry):
        index_chunk(c, False)
        return carry

    lax.fori_loop(0, nch - 1, index_full_chunk, 0)
    index_chunk(nch - 1, True)

    t_q = j * tq + lax.broadcasted_iota(jnp.int32, (1, tq), 1)
    k_row = jnp.minimum(t_q + 1, n_sel).astype(F32)
    acc_rows = 4 * SUBLANES
    sub = lax.broadcasted_iota(jnp.int32, (acc_rows, tq), 0)
    idx_bits = (seq - 1).bit_length()

    def count(pred):
        def chunk(c, acc):
            blk = sc_ref[c]
            for g in range(kc // acc_rows):
                kidx = sub + (c * kc + g * acc_rows)
                acc = acc + jnp.where(pred(blk[g * acc_rows:(g + 1) * acc_rows, :], kidx), 1.0, 0.0)
            return acc

        acc = lax.fori_loop(0, nch, chunk, jnp.zeros((acc_rows, tq), F32))
        return jnp.sum(acc, axis=0, keepdims=True)

    def rows8(v):
        return jnp.broadcast_to(v, (acc_rows, tq))

    def count16(cand16):
        def chunk(c, acc):
            blk = sc16_ref[c]
            for g in range(kc // acc_rows):
                hit = blk[g * acc_rows:(g + 1) * acc_rows, :] >= cand16
                acc = acc + jnp.where(hit, jnp.ones_like(cand16), jnp.zeros_like(cand16))
            return acc

        acc = lax.fori_loop(0, nch, chunk, jnp.zeros((acc_rows, tq), BF16))
        return jnp.sum(acc.astype(F32), axis=0, keepdims=True)

    def coarse_bit(i, carry):
        p, n_p = carry
        cand = p | jnp.left_shift(jnp.int32(1), 31 - i)
        top = lax.bitcast_convert_type(_ukey_to_f32(cand), jnp.int32) & HIGH_HALF
        cnt = count16(rows8(lax.bitcast_convert_type(top, F32)).astype(BF16))
        keep = cnt >= k_row
        return jnp.where(keep, cand, p), jnp.where(keep, cnt, n_p)

    def value_bit(i, carry):
        p, n_p = carry
        cand = p | jnp.left_shift(jnp.int32(1), 31 - i)
        cand_f = rows8(_ukey_to_f32(cand))
        cnt = count(lambda v, kidx: v >= cand_f)
        keep = cnt >= k_row
        return jnp.where(keep, cand, p), jnp.where(keep, cnt, n_p)

    start = (jnp.zeros((1, tq), jnp.int32), jnp.zeros((1, tq), F32))
    p_thr, n_ge = lax.fori_loop(16, 32, value_bit, lax.fori_loop(0, 16, coarse_bit, start))
    thr = _ukey_to_f32(p_thr)
    thr8 = rows8(thr)
    tied = jnp.max(jnp.where(n_ge > k_row, 1.0, 0.0)) > 0.0

    def tie_cut():
        need = k_row - count(lambda v, kidx: v > thr8)

        def index_bit(i, p):
            cand = p | jnp.left_shift(jnp.int32(1), idx_bits - 1 - i)
            cand8 = rows8(cand)
            cnt = count(lambda v, kidx: jnp.where(v == thr8, kidx, seq) < cand8)
            return jnp.where(cnt < need, cand, p)

        return lax.fori_loop(0, idx_bits, index_bit, jnp.zeros((1, tq), jnp.int32))

    cut = lax.cond(tied, tie_cut, lambda: jnp.full((1, tq), seq, jnp.int32))

    def write_bias_tied(c, carry):
        blk = sc_ref[c]
        tie_bias = jnp.where(c * kc + k_off <= cut, 0.0, MASKED)
        sc_ref[c] = jnp.where(blk > thr, 0.0, jnp.where(blk == thr, tie_bias, MASKED)).T
        return carry

    def write_bias(c, carry):
        sc_ref[c] = jnp.where(sc_ref[c] >= thr, 0.0, MASKED).T
        return carry

    @pl.when(tied)
    def _():
        lax.fori_loop(0, nch, write_bias_tied, 0)

    @pl.when(jnp.logical_not(tied))
    def _():
        lax.fori_loop(0, nch, write_bias, 0)

    for h in range(ATTN_HEADS):
        m_ref[h] = jnp.full((tq, LANES), MASKED, F32)
        l_ref[h] = jnp.zeros((tq, LANES), F32)
        acc_ref[h] = jnp.zeros((tq, LANES), F32)

    def attend(c, carry):
        r0 = pl.multiple_of(c * kc, kc)
        keys = pl.ds(r0, kc)
        bias = sc_ref[c]

        def qk(h):
            return _dot_t(qm_ref[h], k_ref[keys, (h // 2) * LANES:(h // 2 + 1) * LANES])

        s_next = qk(0)
        for h in range(ATTN_HEADS):
            s = s_next + bias
            if h + 1 < ATTN_HEADS:
                s_next = qk(h + 1)
            parts = [s[:, b * LANES:(b + 1) * LANES] for b in range(lb_n)]
            m_old = m_ref[h]
            row_max = jnp.max(functools.reduce(jnp.maximum, parts), axis=1, keepdims=True)
            m_new = jnp.maximum(m_old, row_max)
            alpha = jnp.exp2(m_old - m_new)
            p_parts = [jnp.exp2(part - m_new) for part in parts]
            l_ref[h] = alpha * l_ref[h] + functools.reduce(jnp.add, p_parts)
            p = jnp.concatenate(p_parts, axis=1).astype(BF16)
            acc_ref[h] = alpha * acc_ref[h] + _dot(p, v_ref[keys, (h // 2) * LANES:(h // 2 + 1) * LANES])
            m_ref[h] = m_new
        return carry

    lax.fori_loop(0, nch, attend, 0)
    for pair in range(ATTN_HEADS // 2):
        l_even = jnp.sum(l_ref[2 * pair], axis=1, keepdims=True)
        l_odd = jnp.sum(l_ref[2 * pair + 1], axis=1, keepdims=True)
        o_pair = jnp.where(lane < HEAD_DIM, acc_ref[2 * pair] / l_even, acc_ref[2 * pair + 1] / l_odd)
        o_ref[:, pair * LANES:(pair + 1) * LANES] = o_pair.astype(BF16)


def _attention(q, k, v, qi, kilo, kihi, wit, bsz, seq, tq, kc):
    n_sel = min(TOPK_KEYS_MAX, seq // 4)
    shape3 = lambda a: a.reshape(bsz, seq, a.shape[-1])
    q, k, v, qi, kilo, kihi = map(shape3, (q, k, v, qi, kilo, kihi))
    qblk = lambda w: pl.BlockSpec((None, tq, w), lambda b, j: (b, j, 0))
    kblk = lambda w: pl.BlockSpec((None, seq, w), lambda b, j: (b, 0, 0), pipeline_mode=pl.Buffered(1))
    out = pl.pallas_call(
        functools.partial(_attn_kernel, n_sel=n_sel),
        out_shape=jax.ShapeDtypeStruct((bsz, seq, ATTN_DIM), BF16),
        grid=(bsz, seq // tq),
        in_specs=[qblk(ATTN_DIM), kblk(ATTN_DIM), kblk(ATTN_DIM), qblk(IDX_HEADS * IDX_DIM), kblk(LANES), kblk(LANES),
                  pl.BlockSpec((IDX_HEADS, tq), lambda b, j: (0, b * (seq // tq) + j))],
        out_specs=qblk(ATTN_DIM),
        scratch_shapes=[
            pltpu.VMEM((seq // kc, kc, tq), F32),
            pltpu.VMEM((seq // kc, kc, tq), BF16),
            pltpu.VMEM((ATTN_HEADS, tq, LANES), BF16),
            pltpu.VMEM((ATTN_HEADS, tq, LANES), F32),
            pltpu.VMEM((ATTN_HEADS, tq, LANES), F32),
            pltpu.VMEM((ATTN_HEADS, tq, LANES), F32),
        ],
        compiler_params=_params("parallel", "arbitrary"),
        name="dsa_attention",
    )(q, k, v, qi, kilo, kihi, wit)
    return out.reshape(bsz * seq, ATTN_DIM)


def _outproj_kernel(conv_ref, attn_ref, x_ref, gate1_ref, sc_ref, sh_ref, g2_ref, wout_ref, wrt_ref, rbias_ref,
                    x1_o, h2_o, h2pa_o, h2pb_o, eidx_o, g8_o, cnt_o):
    mix = _dot(conv_ref[...], wout_ref[:CONV_DIM, :]) + _dot(attn_ref[...], wout_ref[CONV_DIM:, :])
    x1 = x_ref[...] + gate1_ref[0] * mix
    x1_o[...] = x1
    h2 = _rms_mod(x1, g2_ref[...], sc_ref[0], sh_ref[0])
    h2b = h2.astype(BF16)
    h2_o[...] = h2b
    quarter = h2.shape[1] // 4
    bits = lax.bitcast_convert_type(h2b.astype(F32), jnp.int32)
    for half_o, c0 in ((h2pa_o, 0), (h2pb_o, 2 * quarter)):
        low = lax.shift_right_logical(bits[:, c0:c0 + quarter], 16)
        half_o[...] = low | (bits[:, c0 + quarter:c0 + 2 * quarter] & HIGH_HALF)

    wrt = wrt_ref[...]
    wrt_hi, wrt_lo = _split(wrt)
    h2_hi, h2_lo = _split(h2)
    scores = jax.nn.sigmoid(_dot_t(wrt_hi, h2_hi) + _dot_t(wrt_hi, h2_lo) + _dot_t(wrt_lo, h2_hi))
    work = scores + rbias_ref[...]
    n_e, tm = work.shape
    row_in_tile = lax.broadcasted_iota(jnp.int32, (SUBLANES, tm), 0)
    tiles = [work[j * SUBLANES:(j + 1) * SUBLANES, :] for j in range(n_e // SUBLANES)]
    rank = [jnp.zeros((SUBLANES, tm), F32) for _ in tiles]
    for e2 in range(n_e):
        other = jnp.broadcast_to(work[e2:e2 + 1, :], (SUBLANES, tm))
        for j, tile in enumerate(tiles):
            if j > e2 // SUBLANES:
                beats = other >= tile
            elif j < e2 // SUBLANES:
                beats = other > tile
            else:
                later = row_in_tile > e2 % SUBLANES
                beats = jnp.where(later, jnp.where(other >= tile, 1.0, 0.0), jnp.where(other > tile, 1.0, 0.0)) > 0.0
            rank[j] = rank[j] + jnp.where(beats, 1.0, 0.0)
    chosen = jnp.concatenate([jnp.where(r < float(TOP_K_EXPERTS), 1.0, 0.0) for r in rank], axis=0)

    @pl.when(pl.program_id(0) == 0)
    def _():
        cnt_o[...] = jnp.zeros(cnt_o.shape, F32)

    cnt_o[...] += jnp.sum(chosen, axis=1, keepdims=True)
    eidx_rows, pick_rows = [], []
    for k in range(TOP_K_EXPERTS):
        e_sum = jnp.zeros((SUBLANES, tm), F32)
        s_sum = jnp.zeros((SUBLANES, tm), F32)
        for j in range(len(tiles)):
            is_k = rank[j] == float(k)
            e_sum = e_sum + jnp.where(is_k, (row_in_tile + j * SUBLANES).astype(F32), 0.0)
            s_sum = s_sum + jnp.where(is_k, scores[j * SUBLANES:(j + 1) * SUBLANES, :], 0.0)
        eidx_rows.append(jnp.sum(e_sum, axis=0, keepdims=True))
        pick_rows.append(jnp.sum(s_sum, axis=0, keepdims=True))
    eidx_t = jnp.concatenate(eidx_rows, axis=0)
    picked_t = jnp.concatenate(pick_rows, axis=0)
    g8_t = picked_t / jnp.sum(picked_t, axis=0, keepdims=True) * ROUTED_SCALE
    eidx_o[...] = eidx_t.T.astype(jnp.int32)
    g8_o[...] = g8_t.T


def _outproj(conv, attn, x2, gate1, scale2, shift2, norm2_g, w_out, w_router, router_bias, seq, tm):
    t, d = x2.shape
    e = w_router.shape[1]
    per_b = seq // tm
    row = lambda w: pl.BlockSpec((tm, w), lambda i: (i, 0))
    full = lambda a: pl.BlockSpec(a.shape, lambda i: (0,) * a.ndim)
    mod = pl.BlockSpec((1, 1, d), lambda i: (i // per_b, 0, 0))
    g2 = norm2_g.reshape(1, d)
    wo = w_out.astype(BF16)
    rbias = router_bias.reshape(e, 1)
    wrt = w_router.T
    return pl.pallas_call(
        _outproj_kernel,
        out_shape=(jax.ShapeDtypeStruct((t, d), F32), jax.ShapeDtypeStruct((t, d), BF16),
                   jax.ShapeDtypeStruct((t, d // 4), jnp.int32), jax.ShapeDtypeStruct((t, d // 4), jnp.int32),
                   jax.ShapeDtypeStruct((t, TOP_K_EXPERTS), jnp.int32), jax.ShapeDtypeStruct((t, TOP_K_EXPERTS), F32),
                   jax.ShapeDtypeStruct((e, LANES), F32)),
        grid=(t // tm,),
        in_specs=[row(CONV_DIM), row(ATTN_DIM), row(d), mod, mod, mod, full(g2), full(wo), full(wrt), full(rbias)],
        out_specs=(row(d), row(d), row(d // 4), row(d // 4), row(TOP_K_EXPERTS), row(TOP_K_EXPERTS),
                   pl.BlockSpec((e, LANES), lambda i: (0, 0))),
        compiler_params=_params("arbitrary"),
        name="outproj_router",
    )(conv, attn, x2, gate1, scale2, shift2, g2, wo, wrt, rbias)


def _route_kernel(eidx_ref, cnt_ref, earlier_ref, dest_o, run_ref, *, bm):
    tm = eidx_ref.shape[0]
    e8 = eidx_ref[...]
    lane = lax.broadcasted_iota(jnp.int32, (tm, N_EXPERTS), 1)
    hits = [lane == e8[:, k:k + 1] for k in range(TOP_K_EXPERTS)]
    member = functools.reduce(jnp.add, [jnp.where(hit, 1.0, 0.0) for hit in hits])

    @pl.when(pl.program_id(0) == 0)
    def _():
        blocks = jnp.ceil(cnt_ref[...] / bm)
        r = lax.broadcasted_iota(jnp.int32, (N_EXPERTS, N_EXPERTS), 0)
        c = lax.broadcasted_iota(jnp.int32, (N_EXPERTS, N_EXPERTS), 1)
        before = jnp.where(r < c, 1.0, 0.0).astype(BF16)
        b_hi, b_lo = _split(blocks)
        run_ref[...] = (_dot(b_hi, before) + _dot(b_lo, before)) * bm

    base = run_ref[0:1, :] + _dot(earlier_ref[...], member.astype(BF16))
    slot = lax.broadcasted_iota(jnp.int32, (tm, TOP_K_EXPERTS), 1)
    dest = jnp.zeros((tm, TOP_K_EXPERTS), F32)
    for k in range(TOP_K_EXPERTS):
        dest = jnp.where(slot == k, jnp.sum(jnp.where(hits[k], base, 0.0), axis=1, keepdims=True), dest)
    dest_o[...] = dest.astype(jnp.int32)
    run_ref[...] += jnp.sum(member, axis=0, keepdims=True)


def _route(eidx, counts, bm, tm):
    nt = eidx.shape[0]
    cnt8 = jnp.broadcast_to(counts.reshape(1, N_EXPERTS), (SUBLANES, N_EXPERTS))
    earlier = jnp.tri(tm, k=-1, dtype=BF16)
    return pl.pallas_call(
        functools.partial(_route_kernel, bm=bm),
        out_shape=jax.ShapeDtypeStruct((nt, TOP_K_EXPERTS), jnp.int32),
        grid=(nt // tm,),
        in_specs=[pl.BlockSpec((tm, TOP_K_EXPERTS), lambda i: (i, 0)),
                  pl.BlockSpec((SUBLANES, N_EXPERTS), lambda i: (0, 0)),
                  pl.BlockSpec((tm, tm), lambda i: (0, 0))],
        out_specs=pl.BlockSpec((tm, TOP_K_EXPERTS), lambda i: (i, 0)),
        scratch_shapes=[pltpu.VMEM((SUBLANES, N_EXPERTS), F32)],
        compiler_params=_params("arbitrary"),
        name="moe_route",
    )(eidx, cnt8, earlier)


def _sc_mesh():
    return plsc.VectorSubcoreMesh(core_axis_name="core", subcore_axis_name="subcore", num_cores=V7X_SC_CORES,
                                  num_subcores=V7X_SC_SUBCORES)


def _sc_scatter_rows(rows, dest_kt, n_out):
    t, d = rows.shape
    n_k = dest_kt.shape[0]
    window = SC_WINDOW

    @functools.partial(pl.kernel, out_type=jax.ShapeDtypeStruct((n_out, d), rows.dtype), mesh=_sc_mesh(),
                       name="moe_dispatch_scatter")
    def scatter(x_hbm, i_hbm, o_hbm):
        def body(x_vmem, i_vmem):
            for k in range(n_k):
                pltpu.sync_copy(x_vmem, o_hbm.at[i_vmem.at[k]])

        pltpu.emit_pipeline(
            body,
            grid=(t // window,),
            in_specs=[pl.BlockSpec((window, d), lambda i: (i, 0)), pl.BlockSpec((n_k, window), lambda i: (0, i))],
            out_specs=[],
            core_axis_name=("core", "subcore"),
            dimension_semantics=(pltpu.PARALLEL,),
        )(x_hbm, i_hbm)

    return scatter(rows, dest_kt)


def _sc_gather_rows(table, idx):
    n = idx.shape[0]
    d = table.shape[1]
    window = SC_WINDOW

    @functools.partial(pl.kernel, out_type=jax.ShapeDtypeStruct((n, d), table.dtype), mesh=_sc_mesh(),
                       name="moe_combine_gather")
    def gather(tab_hbm, i_hbm, o_hbm):
        def body(i_vmem, o_vmem):
            pltpu.sync_copy(tab_hbm.at[i_vmem.at[0]], o_vmem)

        pltpu.emit_pipeline(
            body,
            grid=(n // window,),
            in_specs=[pl.BlockSpec((1, window), lambda i: (0, i))],
            out_specs=[pl.BlockSpec((window, d), lambda i: (i, 0))],
            core_axis_name=("core", "subcore"),
            dimension_semantics=(pltpu.PARALLEL,),
        )(i_hbm, o_hbm)

    return gather(table, idx.reshape(1, n))


def _ffn_kernel(be_ref, nused_ref, xa_ref, xb_ref, w1_ref, w3_ref, w2_ref, *refs):
    ys_refs, (w13_s, w2_s) = refs[:-2], refs[-2:]
    b = pl.program_id(0)
    f = w2_ref.shape[1]

    @pl.when((b == 0) | (be_ref[b] != be_ref[jnp.maximum(b - 1, 0)]))
    def _():
        w13_s[:, :f] = w1_ref[0].astype(BF16)
        w13_s[:, f:] = w3_ref[0].astype(BF16)
        w2_s[...] = w2_ref[0].astype(BF16)

    @pl.when(b < nused_ref[0])
    def _():
        cols = []
        for x_ref in (xa_ref, xb_ref):
            xu = x_ref[...]
            cols.append(lax.bitcast_convert_type(lax.shift_left(xu, 16), F32).astype(BF16))
            cols.append(lax.bitcast_convert_type(xu & HIGH_HALF, F32).astype(BF16))
        hh = _dot(jnp.concatenate(cols, axis=1), w13_s[...])
        act = _silu(hh[:, :f]) * hh[:, f:]
        y = _dot(act.astype(BF16), w2_s[...])
        bits = lax.bitcast_convert_type(y.astype(BF16).astype(F32), jnp.int32)
        for j, y_ref in enumerate(ys_refs):
            c0 = 2 * j * SC_ROW_WORDS
            low = lax.shift_right_logical(bits[:, c0:c0 + SC_ROW_WORDS], 16)
            y_ref[...] = low | (bits[:, c0 + SC_ROW_WORDS:c0 + 2 * SC_ROW_WORDS] & HIGH_HALF)


def _ffn(block_e, n_used, xs_a, xs_b, w1, w3, w2, bm):
    rows, q = xs_a.shape
    _, d, f = w1.shape
    n_out = d // (2 * SC_ROW_WORDS)
    grid_spec = pltpu.PrefetchScalarGridSpec(
        num_scalar_prefetch=2,
        grid=(rows // bm,),
        in_specs=[pl.BlockSpec((bm, q), lambda b, be, nu: (b, 0)),
                  pl.BlockSpec((bm, q), lambda b, be, nu: (b, 0)),
                  pl.BlockSpec((1, d, f), lambda b, be, nu: (be[b], 0, 0)),
                  pl.BlockSpec((1, d, f), lambda b, be, nu: (be[b], 0, 0)),
                  pl.BlockSpec((1, f, d), lambda b, be, nu: (be[b], 0, 0))],
        out_specs=tuple(pl.BlockSpec((bm, SC_ROW_WORDS), lambda b, be, nu: (b, 0)) for _ in range(n_out)),
        scratch_shapes=[pltpu.VMEM((d, 2 * f), BF16), pltpu.VMEM((f, d), BF16)],
    )
    return pl.pallas_call(
        _ffn_kernel,
        out_shape=tuple(jax.ShapeDtypeStruct((rows, SC_ROW_WORDS), jnp.int32) for _ in range(n_out)),
        grid_spec=grid_spec,
        compiler_params=_params("arbitrary"),
        name="moe_expert_ffn",
    )(block_e, n_used, xs_a, xs_b, w1, w3, w2)


def _combine_kernel(*refs):
    y_refs = refs[:-7]
    g8_ref, h2_ref, x1_ref, gate2_ref, ws13_ref, ws2_ref, o_ref = refs[-7:]
    f = ws2_ref.shape[0]
    hs = _dot(h2_ref[...], ws13_ref[...])
    acc = _dot((_silu(hs[:, :f]) * hs[:, f:]).astype(BF16), ws2_ref[...])
    g8 = g8_ref[...]
    for k in range(TOP_K_EXPERTS):
        cols = []
        for y_ref in y_refs:
            yu = y_ref[k]
            cols.append(lax.bitcast_convert_type(lax.shift_left(yu, 16), F32))
            cols.append(lax.bitcast_convert_type(yu & HIGH_HALF, F32))
        acc = acc + g8[:, k:k + 1] * jnp.concatenate(cols, axis=1)
    o_ref[...] = x1_ref[...] + gate2_ref[0] * acc


def _combine(y8s, g8, h2, x1, gate2, ws13, ws2b, seq, tm):
    t, d = x1.shape
    per_b = seq // tm
    row = lambda w: pl.BlockSpec((tm, w), lambda i: (i, 0))
    full = lambda a: pl.BlockSpec(a.shape, lambda i: (0,) * a.ndim)
    y_spec = pl.BlockSpec((TOP_K_EXPERTS, tm, SC_ROW_WORDS), lambda i: (0, i, 0))
    return pl.pallas_call(
        _combine_kernel,
        out_shape=jax.ShapeDtypeStruct((t, d), F32),
        grid=(t // tm,),
        in_specs=[y_spec] * len(y8s) + [row(TOP_K_EXPERTS), row(d), row(d),
                                         pl.BlockSpec((1, 1, d), lambda i: (i // per_b, 0, 0)), full(ws13), full(ws2b)],
        out_specs=row(d),
        compiler_params=_params("parallel"),
        name="moe_combine",
    )(*y8s, g8, h2, x1, gate2, ws13, ws2b)


def _moe(h2, h2pa, h2pb, x1, eidx, g8, counts, gate2, w1, w3, w2, ws1, ws3, ws2, seq):
    t, d = x1.shape
    n_e = w1.shape[0]
    bm = MOE_BLOCK_ROWS
    rows = t * TOP_K_EXPERTS + n_e * bm
    ws13 = jnp.concatenate([ws1, ws3], axis=1).astype(BF16)
    ws2b = ws2.astype(BF16)

    dest = _route(eidx, counts, bm, min(TOKEN_TILE, t))
    pend = jnp.cumsum(jnp.ceil(counts / bm) * bm)
    block_row0 = jnp.arange(rows // bm, dtype=F32) * bm
    block_e = jnp.minimum(jnp.sum(pend[None, :] <= block_row0[:, None], axis=1), n_e - 1)
    n_used = (pend[-1:] / bm).astype(jnp.int32)
    dest_kt = dest.T

    xs_a = _sc_scatter_rows(h2pa, dest_kt, rows)
    xs_b = _sc_scatter_rows(h2pb, dest_kt, rows)
    ys = _ffn(block_e.astype(jnp.int32), n_used, xs_a, xs_b, w1, w3, w2, bm)
    pair_rows = dest_kt.reshape(-1)
    y8s = [_sc_gather_rows(y, pair_rows).reshape(TOP_K_EXPERTS, t, SC_ROW_WORDS) for y in ys]
    return _combine(y8s, g8, h2, x1, gate2, ws13, ws2b, seq, min(COMBINE_TILE, seq))


def _layer(x, c, positions, norm1_g, norm2_g, w_ada, b_ada, w_in, conv_w, q_norm_g, k_norm_g, kidx_norm_g, w_out,
           w_router, router_bias, w1, w3, w2, ws1, ws3, ws2):
    bsz, seq, d = x.shape
    t = bsz * seq
    tm = min(TOKEN_TILE, seq)
    tq = min(QUERY_TILE, seq)
    x2 = x.reshape(t, d)

    ada = _ada(c, w_ada, b_ada)
    shift1, scale1, gate1, shift2, scale2, gate2 = [a.reshape(bsz, 1, d) for a in jnp.split(ada, 6, axis=-1)]

    inv_freq = ROPE_THETA ** (-jnp.arange(ROPE_HALF, dtype=F32) / ROPE_HALF)
    invf_lane = inv_freq[(jnp.arange(LANES) % HEAD_DIM) % ROPE_HALF].reshape(1, LANES)
    ct, s1, s2 = _rope_tables(positions.reshape(t, 1), invf_lane, min(ROPE_TILE, t))

    conv, q, k, v, qi, kilo, kihi, wit = _inproj(x2, scale1, shift1, norm1_g, w_in, conv_w, q_norm_g, k_norm_g,
                                                kidx_norm_g, ct, s1, s2, seq, tm)
    attn = _attention(q, k, v, qi, kilo, kihi, wit, bsz, seq, tq, kc=tq)
    x1, h2, h2pa, h2pb, eidx, g8, cnt = _outproj(conv, attn, x2, gate1, scale2, shift2, norm2_g, w_out, w_router,
                                                 router_bias, seq, tm)
    out = _moe(h2, h2pa, h2pb, x1, eidx, g8, cnt[:, 0], gate2, w1, w3, w2, ws1, ws3, ws2, seq)
    return out.reshape(bsz, seq, d)


def kernel(x, c, positions, norm1_g, norm2_g, w_ada, b_ada, w_in, conv_w, q_norm_g, k_norm_g, kidx_norm_g, w_out,
           w_router, router_bias, w1, w3, w2, ws1, ws3, ws2):
    for l in range(w_in.shape[0]):
        x = _layer(x, c, positions, norm1_g[l], norm2_g[l], w_ada[l], b_ada[l], w_in[l], conv_w[l], q_norm_g[l],
                   k_norm_g[l], kidx_norm_g[l], w_out[l], w_router[l], router_bias[l], w1[l], w3[l], w2[l], ws1[l],
                   ws3[l], ws2[l])
    return x
```

```python
import functools

import jax
import jax.numpy as jnp
from jax import lax
from jax.experimental import pallas as pl
from jax.experimental.pallas import tpu as pltpu
from jax.experimental.pallas import tpu_sc as plsc

F32 = jnp.float32
BF16 = jnp.bfloat16

HEAD_DIM = 64
ATTN_HEADS = 8
ATTN_DIM = ATTN_HEADS * HEAD_DIM
CONV_DIM = 512
IDX_HEADS = 8
IDX_DIM = 64
IDX_SCALE = (IDX_DIM ** -0.5) * (IDX_HEADS ** -0.5)
TOPK_KEYS_MAX = 256
ROPE_THETA = 500000.0
ROPE_DIM = HEAD_DIM // 4
ROPE_HALF = ROPE_DIM // 2
N_EXPERTS = 64
TOP_K_EXPERTS = 8
ROUTED_SCALE = 2.5
EPS = 1e-6

LANES = 128
SUBLANES = 8
V7X_VMEM_BYTES = 64 * 1024 * 1024
VMEM_LIMIT = V7X_VMEM_BYTES * 3 // 4
V7X_SC_CORES = 2
V7X_SC_SUBCORES = 16
SC_WINDOW = 128
SC_ROW_WORDS = 256

TOKEN_TILE = 512
QUERY_TILE = 512
COMBINE_TILE = 512
ROPE_TILE = 2048

HIGH_HALF = -65536
MASKED = -1e30
Q_SCALE = HEAD_DIM ** -0.5 * 1.4426950408889634
MOE_BLOCK_ROWS = 1024


def _params(*semantics):
    return pltpu.CompilerParams(dimension_semantics=semantics, vmem_limit_bytes=VMEM_LIMIT)


def _dot(a, b):
    return jnp.dot(a, b, preferred_element_type=F32)


def _dot_t(a, b):
    return lax.dot_general(a, b, (((1,), (1,)), ((), ())), preferred_element_type=F32)


def _split(a):
    hi = a.astype(BF16)
    lo = (a - hi.astype(F32)).astype(BF16)
    return hi, lo


def _dot3(a, b):
    a_hi, a_lo = _split(a)
    b_hi, b_lo = _split(b)
    return _dot(a_hi, b_hi) + _dot(a_hi, b_lo) + _dot(a_lo, b_hi)


def _silu(v):
    return v * jax.nn.sigmoid(v)


def _rms_mod(xv, g, scale, shift):
    ms = jnp.mean(xv * xv, axis=-1, keepdims=True)
    y = xv * lax.rsqrt(ms + EPS)
    return (y * g) * (1.0 + scale) + shift


def _ada_kernel(c_ref, w_ref, b_ref, o_ref):
    o_ref[...] = _dot3(_silu(c_ref[...]), w_ref[...]) + b_ref[...]


def _ada(c, w_ada, b_ada):
    bsz, d = c.shape
    n = w_ada.shape[1]
    bn = n // 4
    return pl.pallas_call(
        _ada_kernel,
        out_shape=jax.ShapeDtypeStruct((bsz, n), F32),
        grid=(n // bn,),
        in_specs=[
            pl.BlockSpec((bsz, d), lambda i: (0, 0)),
            pl.BlockSpec((d, bn), lambda i: (0, i)),
            pl.BlockSpec((1, bn), lambda i: (0, i)),
        ],
        out_specs=pl.BlockSpec((bsz, bn), lambda i: (0, i)),
        compiler_params=_params("parallel"),
        name="ada",
    )(c, w_ada, b_ada.reshape(1, n))


def _rope_kernel(pos_ref, invf_ref, c_ref, s1_ref, s2_ref):
    ang = pos_ref[...].astype(F32) * invf_ref[...]
    d = lax.broadcasted_iota(jnp.int32, ang.shape, 1) & (HEAD_DIM - 1)
    cos = jnp.cos(ang)
    sin = jnp.sin(ang)
    c_ref[...] = jnp.where(d < ROPE_DIM, cos, 1.0)
    s1_ref[...] = jnp.where(d < ROPE_HALF, -sin, 0.0)
    s2_ref[...] = jnp.where(d < ROPE_HALF, 0.0, jnp.where(d < ROPE_DIM, sin, 0.0))


def _rope_tables(pos, invf_lane, tm):
    t = pos.shape[0]
    spec = pl.BlockSpec((tm, LANES), lambda i: (i, 0))
    shp = jax.ShapeDtypeStruct((t, LANES), F32)
    return pl.pallas_call(
        _rope_kernel,
        out_shape=(shp, shp, shp),
        grid=(t // tm,),
        in_specs=[pl.BlockSpec((tm, 1), lambda i: (i, 0)), pl.BlockSpec((1, LANES), lambda i: (0, 0))],
        out_specs=(spec, spec, spec),
        compiler_params=_params("parallel"),
        name="rope_tables",
    )(pos, invf_lane)


def _rope(y, c, s1, s2):
    return y * c + pltpu.roll(y, LANES - ROPE_HALF, 1) * s1 + pltpu.roll(y, ROPE_HALF, 1) * s2


def _head_rms(xb, avg):
    hi, lo = _split(xb * xb)
    ms = _dot(hi, avg) + _dot(lo, avg)
    return xb * lax.rsqrt(ms + EPS)


def _inproj_kernel(x_ref, xh_ref, sc_ref, sh_ref, g1_ref, wmix_ref, wq_ref, wk_ref, wv_ref, wqi_ref, wl_ref, wlt_ref,
                   cw_ref, qg_ref, kg_ref, kig_ref, ct_ref, s1_ref, s2_ref, avg_ref,
                   conv_o, q_o, k_o, v_o, qi_o, kilo_o, kihi_o, wit_o, *, seq):
    tm = x_ref.shape[0]
    scale = sc_ref[0]
    shift = sh_ref[0]
    g1 = g1_ref[...]
    h = _rms_mod(x_ref[...], g1, scale, shift).astype(BF16)
    hh = _rms_mod(xh_ref[...], g1, scale, shift).astype(BF16)

    mix = _dot(h, wmix_ref[...])
    mixh = _dot(hh, wmix_ref[...])
    u = mix[:, 2 * CONV_DIM:] * mix[:, :CONV_DIM]
    uh = mixh[:, 2 * CONV_DIM:] * mixh[:, :CONV_DIM]
    seq_start = (pl.program_id(0) * tm) % seq == 0
    uh = jnp.where(seq_start, 0.0, uh)
    ext = jnp.concatenate([uh, u], axis=0)
    u1 = pltpu.roll(ext, 1, 0)[SUBLANES:]
    u2 = pltpu.roll(ext, 2, 0)[SUBLANES:]
    conv = u2 * cw_ref[0:1, :] + u1 * cw_ref[1:2, :] + u * cw_ref[2:3, :]
    conv_o[...] = (mix[:, CONV_DIM:2 * CONV_DIM] * conv).astype(BF16)

    ct = ct_ref[...]
    s1 = s1_ref[...]
    s2 = s2_ref[...]
    avg = avg_ref[...]
    qf = _dot(h, wq_ref[...])
    kf = _dot(h, wk_ref[...])
    qif = _dot(h, wqi_ref[...])
    for p in range(ATTN_DIM // LANES):
        sl = slice(p * LANES, (p + 1) * LANES)
        qn = _rope(_head_rms(qf[:, sl], avg) * qg_ref[...], ct, s1, s2)
        q_o[:, sl] = (qn * Q_SCALE).astype(BF16)
        kn = _rope(_head_rms(kf[:, sl], avg) * kg_ref[...], ct, s1, s2)
        k_o[:, sl] = kn.astype(BF16)
        qi_o[:, sl] = _rope(qif[:, sl], ct, s1, s2).astype(BF16)
    v_o[...] = _dot(h, wv_ref[...]).astype(BF16)

    wit_o[...] = _dot_t(wlt_ref[...], h)[IDX_DIM:IDX_DIM + IDX_HEADS, :]
    last = _dot(h, wl_ref[...])
    lane = lax.broadcasted_iota(jnp.int32, last.shape, 1)
    is_key = lane < IDX_DIM
    kin = _head_rms(last, avg) * kig_ref[...]
    kin = _rope(kin, jnp.where(is_key, ct, 1.0), jnp.where(is_key, s1, 0.0), jnp.where(is_key, s2, 0.0))
    klo = jnp.where(is_key, kin, 0.0)
    kilo_o[...] = klo.astype(BF16)
    kihi_o[...] = pltpu.roll(klo, IDX_DIM, 1).astype(BF16)


def _inproj(x2, scale1, shift1, norm1_g, w_in, conv_w, q_norm_g, k_norm_g, kidx_norm_g, ct, s1, s2, seq, tm):
    t, d = x2.shape
    cuts = [0, 3 * CONV_DIM, 3 * CONV_DIM + ATTN_DIM, 3 * CONV_DIM + 2 * ATTN_DIM, 3 * CONV_DIM + 3 * ATTN_DIM,
            3 * CONV_DIM + 3 * ATTN_DIM + IDX_HEADS * IDX_DIM]
    wb = w_in.astype(BF16)
    wmix, wq, wk, wv, wqi = [wb[:, a:b] for a, b in zip(cuts[:-1], cuts[1:])]
    wl = wb[:, cuts[-1]:]
    wl = jnp.pad(wl, ((0, 0), (0, LANES - wl.shape[1])))
    wlt = wl.T
    ones = jnp.ones((1, LANES - IDX_DIM), F32)
    qg = jnp.tile(q_norm_g.reshape(1, HEAD_DIM), (1, 2))
    kg = jnp.tile(k_norm_g.reshape(1, HEAD_DIM), (1, 2))
    kig = jnp.concatenate([kidx_norm_g.reshape(1, IDX_DIM), ones], axis=1)
    blk = jnp.arange(LANES) // HEAD_DIM
    avg = jnp.where(blk[:, None] == blk[None, :], 1.0 / HEAD_DIM, 0.0).astype(BF16)

    bsz = t // seq
    per_b = seq // tm
    row = lambda w: pl.BlockSpec((tm, w), lambda i: (i, 0))
    full = lambda a: pl.BlockSpec(a.shape, lambda i: (0,) * a.ndim)
    mod = pl.BlockSpec((1, 1, d), lambda i: (i // per_b, 0, 0))
    halo = pl.BlockSpec((SUBLANES, d), lambda i: (jnp.maximum(i * (tm // SUBLANES) - 1, 0), 0))
    g1 = norm1_g.reshape(1, d)
    out_shape = (
        jax.ShapeDtypeStruct((t, CONV_DIM), BF16),
        jax.ShapeDtypeStruct((t, ATTN_DIM), BF16),
        jax.ShapeDtypeStruct((t, ATTN_DIM), BF16),
        jax.ShapeDtypeStruct((t, ATTN_DIM), BF16),
        jax.ShapeDtypeStruct((t, IDX_HEADS * IDX_DIM), BF16),
        jax.ShapeDtypeStruct((t, LANES), BF16),
        jax.ShapeDtypeStruct((t, LANES), BF16),
        jax.ShapeDtypeStruct((IDX_HEADS, t), F32),
    )
    del bsz
    return pl.pallas_call(
        functools.partial(_inproj_kernel, seq=seq),
        out_shape=out_shape,
        grid=(t // tm,),
        in_specs=[row(d), halo, mod, mod, full(g1), full(wmix), full(wq), full(wk), full(wv), full(wqi), full(wl),
                  full(wlt), full(conv_w), full(qg), full(kg), full(kig), row(LANES), row(LANES), row(LANES), full(avg)],
        out_specs=(row(CONV_DIM), row(ATTN_DIM), row(ATTN_DIM), row(ATTN_DIM), row(IDX_HEADS * IDX_DIM),
                   row(LANES), row(LANES), pl.BlockSpec((IDX_HEADS, tm), lambda i: (0, i))),
        compiler_params=_params("parallel"),
        name="inproj",
    )(x2, x2, scale1, shift1, g1, wmix, wq, wk, wv, wqi, wl, wlt, conv_w, qg, kg, kig, ct, s1, s2, avg)


def _ukey_to_f32(u):
    s = u ^ jnp.int32(-2 ** 31)
    bits = s ^ ((s >> 31) & jnp.int32(0x7FFFFFFF))
    return lax.bitcast_convert_type(bits, F32)


def _attn_kernel(q_ref, k_ref, v_ref, qi_ref, kilo_ref, kihi_ref, wit_ref, o_ref,
                 sc_ref, sc16_ref, qm_ref, acc_ref, m_ref, l_ref, *, n_sel):
    tq = q_ref.shape[0]
    seq = k_ref.shape[0]
    kc = sc_ref.shape[1]
    assert kc == tq
    lb_n = kc // LANES
    j = pl.program_id(1)
    nch = (j + 1) * (tq // kc)
    lane = lax.broadcasted_iota(jnp.int32, (tq, LANES), 1)

    for h in range(ATTN_HEADS):
        qp = q_ref[:, (h // 2) * LANES:(h // 2 + 1) * LANES].astype(F32)
        keep = (lane < HEAD_DIM) if h % 2 == 0 else (lane >= HEAD_DIM)
        qm_ref[h] = jnp.where(keep, qp, 0.0).astype(BF16)

    q_pos = j * tq + lax.broadcasted_iota(jnp.int32, (kc, tq), 1)
    k_off = lax.broadcasted_iota(jnp.int32, (kc, tq), 0)
    w_rows = wit_ref[...]

    def index_chunk(c, diagonal):
        r0 = pl.multiple_of(c * kc, kc)
        klo = kilo_ref[pl.ds(r0, kc), :]
        khi = kihi_ref[pl.ds(r0, kc), :]
        acc = jnp.zeros((kc, tq), F32)
        for p in range(IDX_HEADS // 2):
            qip = qi_ref[:, p * LANES:(p + 1) * LANES]
            acc = acc + jnp.maximum(_dot_t(klo, qip), 0.0) * w_rows[2 * p:2 * p + 1, :]
            acc = acc + jnp.maximum(_dot_t(khi, qip), 0.0) * w_rows[2 * p + 1:2 * p + 2, :]
        score = acc * IDX_SCALE
        if diagonal:
            score = jnp.where(r0 + k_off <= q_pos, score, -jnp.inf)
        score = jnp.where(score == 0.0, 0.0, score)
        sc_ref[c] = score
        top = lax.bitcast_convert_type(score, jnp.int32) & HIGH_HALF
        sc16_ref[c] = lax.bitcast_convert_type(top, F32).astype(BF16)

    def index_full_chunk(c, carry):
        index_chunk(c, False)
        return carry

    lax.fori_loop(0, nch - 1, index_full_chunk, 0)
    index_chunk(nch - 1, True)

    t_q = j * tq + lax.broadcasted_iota(jnp.int32, (1, tq), 1)
    k_row = jnp.minimum(t_q + 1, n_sel).astype(F32)
    acc_rows = 4 * SUBLANES
    sub = lax.broadcasted_iota(jnp.int32, (acc_rows, tq), 0)
    idx_bits = (seq - 1).bit_length()

    def count(pred):
        def chunk(c, acc):
            blk = sc_ref[c]
            for g in range(kc // acc_rows):
                kidx = sub + (c * kc + g * acc_rows)
                acc = acc + jnp.where(pred(blk[g * acc_rows:(g + 1) * acc_rows, :], kidx), 1.0, 0.0)
            return acc

        acc = lax.fori_loop(0, nch, chunk, jnp.zeros((acc_rows, tq), F32))
        return jnp.sum(acc, axis=0, keepdims=True)

    def over_rows(v):
        return jnp.broadcast_to(v, (acc_rows, tq))

    def count16(cand16):
        def chunk(c, acc):
            blk = sc16_ref[c]
            for g in range(kc // acc_rows):
                hit = blk[g * acc_rows:(g + 1) * acc_rows, :] >= cand16
                acc = acc + jnp.where(hit, jnp.ones_like(cand16), jnp.zeros_like(cand16))
            return acc

        acc = lax.fori_loop(0, nch, chunk, jnp.zeros((acc_rows, tq), BF16))
        return jnp.sum(acc.astype(F32), axis=0, keepdims=True)

    def coarse_bit(i, carry):
        p, n_p = carry
        cand = p | jnp.left_shift(jnp.int32(1), 31 - i)
        top = lax.bitcast_convert_type(_ukey_to_f32(cand), jnp.int32) & HIGH_HALF
        cnt = count16(over_rows(lax.bitcast_convert_type(top, F32)).astype(BF16))
        keep = cnt >= k_row
        return jnp.where(keep, cand, p), jnp.where(keep, cnt, n_p)

    def value_bit(i, carry):
        p, n_p = carry
        cand = p | jnp.left_shift(jnp.int32(1), 31 - i)
        cand_f = over_rows(_ukey_to_f32(cand))
        cnt = count(lambda v, kidx: v >= cand_f)
        keep = cnt >= k_row
        return jnp.where(keep, cand, p), jnp.where(keep, cnt, n_p)

    start = (jnp.zeros((1, tq), jnp.int32), jnp.zeros((1, tq), F32))
    p_thr, n_ge = lax.fori_loop(16, 32, value_bit, lax.fori_loop(0, 16, coarse_bit, start))
    thr = _ukey_to_f32(p_thr)
    thr_rows = over_rows(thr)
    tied = jnp.max(jnp.where(n_ge > k_row, 1.0, 0.0)) > 0.0

    def tie_cut():
        need = k_row - count(lambda v, kidx: v > thr_rows)

        def index_bit(i, p):
            cand = p | jnp.left_shift(jnp.int32(1), idx_bits - 1 - i)
            cand_rows = over_rows(cand)
            cnt = count(lambda v, kidx: jnp.where(v == thr_rows, kidx, seq) < cand_rows)
            return jnp.where(cnt < need, cand, p)

        return lax.fori_loop(0, idx_bits, index_bit, jnp.zeros((1, tq), jnp.int32))

    cut = lax.cond(tied, tie_cut, lambda: jnp.full((1, tq), seq, jnp.int32))

    def write_bias_tied(c, carry):
        blk = sc_ref[c]
        tie_bias = jnp.where(c * kc + k_off <= cut, 0.0, MASKED)
        sc_ref[c] = jnp.where(blk > thr, 0.0, jnp.where(blk == thr, tie_bias, MASKED)).T
        return carry

    def write_bias(c, carry):
        sc_ref[c] = jnp.where(sc_ref[c] >= thr, 0.0, MASKED).T
        return carry

    @pl.when(tied)
    def _():
        lax.fori_loop(0, nch, write_bias_tied, 0)

    @pl.when(jnp.logical_not(tied))
    def _():
        lax.fori_loop(0, nch, write_bias, 0)

    for h in range(ATTN_HEADS):
        m_ref[h] = jnp.full((tq, LANES), MASKED, F32)
        l_ref[h] = jnp.zeros((tq, LANES), F32)
        acc_ref[h] = jnp.zeros((tq, LANES), F32)

    def attend(c, carry):
        r0 = pl.multiple_of(c * kc, kc)
        keys = pl.ds(r0, kc)
        bias = sc_ref[c]

        def qk(h):
            return _dot_t(qm_ref[h], k_ref[keys, (h // 2) * LANES:(h // 2 + 1) * LANES])

        s_next = qk(0)
        for h in range(ATTN_HEADS):
            s = s_next + bias
            if h + 1 < ATTN_HEADS:
                s_next = qk(h + 1)
            parts = [s[:, b * LANES:(b + 1) * LANES] for b in range(lb_n)]
            m_old = m_ref[h]
            row_max = jnp.max(functools.reduce(jnp.maximum, parts), axis=1, keepdims=True)
            m_new = jnp.maximum(m_old, row_max)
            alpha = jnp.exp2(m_old - m_new)
            p_parts = [jnp.exp2(part - m_new) for part in parts]
            l_ref[h] = alpha * l_ref[h] + functools.reduce(jnp.add, p_parts)
            p = jnp.concatenate(p_parts, axis=1).astype(BF16)
            acc_ref[h] = alpha * acc_ref[h] + _dot(p, v_ref[keys, (h // 2) * LANES:(h // 2 + 1) * LANES])
            m_ref[h] = m_new
        return carry

    lax.fori_loop(0, nch, attend, 0)
    for pair in range(ATTN_HEADS // 2):
        l_even = jnp.sum(l_ref[2 * pair], axis=1, keepdims=True)
        l_odd = jnp.sum(l_ref[2 * pair + 1], axis=1, keepdims=True)
        o_pair = jnp.where(lane < HEAD_DIM, acc_ref[2 * pair] / l_even, acc_ref[2 * pair + 1] / l_odd)
        o_ref[:, pair * LANES:(pair + 1) * LANES] = o_pair.astype(BF16)


def _attention(q, k, v, qi, kilo, kihi, wit, bsz, seq, tq, kc):
    n_sel = min(TOPK_KEYS_MAX, seq // 4)
    shape3 = lambda a: a.reshape(bsz, seq, a.shape[-1])
    q, k, v, qi, kilo, kihi = map(shape3, (q, k, v, qi, kilo, kihi))
    qblk = lambda w: pl.BlockSpec((None, tq, w), lambda b, j: (b, j, 0))
    kblk = lambda w: pl.BlockSpec((None, seq, w), lambda b, j: (b, 0, 0), pipeline_mode=pl.Buffered(1))
    out = pl.pallas_call(
        functools.partial(_attn_kernel, n_sel=n_sel),
        out_shape=jax.ShapeDtypeStruct((bsz, seq, ATTN_DIM), BF16),
        grid=(bsz, seq // tq),
        in_specs=[qblk(ATTN_DIM), kblk(ATTN_DIM), kblk(ATTN_DIM), qblk(IDX_HEADS * IDX_DIM), kblk(LANES), kblk(LANES),
                  pl.BlockSpec((IDX_HEADS, tq), lambda b, j: (0, b * (seq // tq) + j))],
        out_specs=qblk(ATTN_DIM),
        scratch_shapes=[
            pltpu.VMEM((seq // kc, kc, tq), F32),
            pltpu.VMEM((seq // kc, kc, tq), BF16),
            pltpu.VMEM((ATTN_HEADS, tq, LANES), BF16),
            pltpu.VMEM((ATTN_HEADS, tq, LANES), F32),
            pltpu.VMEM((ATTN_HEADS, tq, LANES), F32),
            pltpu.VMEM((ATTN_HEADS, tq, LANES), F32),
        ],
        compiler_params=_params("parallel", "arbitrary"),
        name="dsa_attention",
    )(q, k, v, qi, kilo, kihi, wit)
    return out.reshape(bsz * seq, ATTN_DIM)


def _outproj_kernel(conv_ref, attn_ref, x_ref, gate1_ref, sc_ref, sh_ref, g2_ref, wout_ref, wrt_ref, rbias_ref,
                    x1_o, h2_o, h2pa_o, h2pb_o, eidx_o, g8_o, cnt_o):
    mix = _dot(conv_ref[...], wout_ref[:CONV_DIM, :]) + _dot(attn_ref[...], wout_ref[CONV_DIM:, :])
    x1 = x_ref[...] + gate1_ref[0] * mix
    x1_o[...] = x1
    h2 = _rms_mod(x1, g2_ref[...], sc_ref[0], sh_ref[0])
    h2b = h2.astype(BF16)
    h2_o[...] = h2b
    quarter = h2.shape[1] // 4
    bits = lax.bitcast_convert_type(h2b.astype(F32), jnp.int32)
    for half_o, c0 in ((h2pa_o, 0), (h2pb_o, 2 * quarter)):
        low = lax.shift_right_logical(bits[:, c0:c0 + quarter], 16)
        half_o[...] = low | (bits[:, c0 + quarter:c0 + 2 * quarter] & HIGH_HALF)

    wrt = wrt_ref[...]
    wrt_hi, wrt_lo = _split(wrt)
    h2_hi, h2_lo = _split(h2)
    scores = jax.nn.sigmoid(_dot_t(wrt_hi, h2_hi) + _dot_t(wrt_hi, h2_lo) + _dot_t(wrt_lo, h2_hi))
    work = scores + rbias_ref[...]
    n_e, tm = work.shape
    row_in_tile = lax.broadcasted_iota(jnp.int32, (SUBLANES, tm), 0)
    tiles = [work[j * SUBLANES:(j + 1) * SUBLANES, :] for j in range(n_e // SUBLANES)]
    rank = [jnp.zeros((SUBLANES, tm), F32) for _ in tiles]
    for e2 in range(n_e):
        other = jnp.broadcast_to(work[e2:e2 + 1, :], (SUBLANES, tm))
        for j, tile in enumerate(tiles):
            if j > e2 // SUBLANES:
                beats = other >= tile
            elif j < e2 // SUBLANES:
                beats = other > tile
            else:
                later = row_in_tile > e2 % SUBLANES
                beats = jnp.where(later, jnp.where(other >= tile, 1.0, 0.0), jnp.where(other > tile, 1.0, 0.0)) > 0.0
            rank[j] = rank[j] + jnp.where(beats, 1.0, 0.0)
    chosen = jnp.concatenate([jnp.where(r < float(TOP_K_EXPERTS), 1.0, 0.0) for r in rank], axis=0)

    @pl.when(pl.program_id(0) == 0)
    def _():
        cnt_o[...] = jnp.zeros(cnt_o.shape, F32)

    cnt_o[...] += jnp.sum(chosen, axis=1, keepdims=True)
    eidx_rows, pick_rows = [], []
    for k in range(TOP_K_EXPERTS):
        e_sum = jnp.zeros((SUBLANES, tm), F32)
        s_sum = jnp.zeros((SUBLANES, tm), F32)
        for j in range(len(tiles)):
            is_k = rank[j] == float(k)
            e_sum = e_sum + jnp.where(is_k, (row_in_tile + j * SUBLANES).astype(F32), 0.0)
            s_sum = s_sum + jnp.where(is_k, scores[j * SUBLANES:(j + 1) * SUBLANES, :], 0.0)
        eidx_rows.append(jnp.sum(e_sum, axis=0, keepdims=True))
        pick_rows.append(jnp.sum(s_sum, axis=0, keepdims=True))
    eidx_t = jnp.concatenate(eidx_rows, axis=0)
    picked_t = jnp.concatenate(pick_rows, axis=0)
    g8_t = picked_t / jnp.sum(picked_t, axis=0, keepdims=True) * ROUTED_SCALE
    eidx_o[...] = eidx_t.T.astype(jnp.int32)
    g8_o[...] = g8_t.T


def _outproj(conv, attn, x2, gate1, scale2, shift2, norm2_g, w_out, w_router, router_bias, seq, tm):
    t, d = x2.shape
    e = w_router.shape[1]
    per_b = seq // tm
    row = lambda w: pl.BlockSpec((tm, w), lambda i: (i, 0))
    full = lambda a: pl.BlockSpec(a.shape, lambda i: (0,) * a.ndim)
    mod = pl.BlockSpec((1, 1, d), lambda i: (i // per_b, 0, 0))
    g2 = norm2_g.reshape(1, d)
    wo = w_out.astype(BF16)
    rbias = router_bias.reshape(e, 1)
    wrt = w_router.T
    return pl.pallas_call(
        _outproj_kernel,
        out_shape=(jax.ShapeDtypeStruct((t, d), F32), jax.ShapeDtypeStruct((t, d), BF16),
                   jax.ShapeDtypeStruct((t, d // 4), jnp.int32), jax.ShapeDtypeStruct((t, d // 4), jnp.int32),
                   jax.ShapeDtypeStruct((t, TOP_K_EXPERTS), jnp.int32), jax.ShapeDtypeStruct((t, TOP_K_EXPERTS), F32),
                   jax.ShapeDtypeStruct((e, LANES), F32)),
        grid=(t // tm,),
        in_specs=[row(CONV_DIM), row(ATTN_DIM), row(d), mod, mod, mod, full(g2), full(wo), full(wrt), full(rbias)],
        out_specs=(row(d), row(d), row(d // 4), row(d // 4), row(TOP_K_EXPERTS), row(TOP_K_EXPERTS),
                   pl.BlockSpec((e, LANES), lambda i: (0, 0))),
        compiler_params=_params("arbitrary"),
        name="outproj_router",
    )(conv, attn, x2, gate1, scale2, shift2, g2, wo, wrt, rbias)


def _route_kernel(eidx_ref, cnt_ref, earlier_ref, dest_o, run_ref, *, bm):
    tm = eidx_ref.shape[0]
    e8 = eidx_ref[...]
    lane = lax.broadcasted_iota(jnp.int32, (tm, N_EXPERTS), 1)
    hits = [lane == e8[:, k:k + 1] for k in range(TOP_K_EXPERTS)]
    member = functools.reduce(jnp.add, [jnp.where(hit, 1.0, 0.0) for hit in hits])

    @pl.when(pl.program_id(0) == 0)
    def _():
        blocks = jnp.ceil(cnt_ref[...] / bm)
        r = lax.broadcasted_iota(jnp.int32, (N_EXPERTS, N_EXPERTS), 0)
        c = lax.broadcasted_iota(jnp.int32, (N_EXPERTS, N_EXPERTS), 1)
        before = jnp.where(r < c, 1.0, 0.0).astype(BF16)
        b_hi, b_lo = _split(blocks)
        run_ref[...] = (_dot(b_hi, before) + _dot(b_lo, before)) * bm

    base = run_ref[0:1, :] + _dot(earlier_ref[...], member.astype(BF16))
    slot = lax.broadcasted_iota(jnp.int32, (tm, TOP_K_EXPERTS), 1)
    dest = jnp.zeros((tm, TOP_K_EXPERTS), F32)
    for k in range(TOP_K_EXPERTS):
        dest = jnp.where(slot == k, jnp.sum(jnp.where(hits[k], base, 0.0), axis=1, keepdims=True), dest)
    dest_o[...] = dest.astype(jnp.int32)
    run_ref[...] += jnp.sum(member, axis=0, keepdims=True)


def _route(eidx, counts, bm, tm):
    nt = eidx.shape[0]
    cnt8 = jnp.broadcast_to(counts.reshape(1, N_EXPERTS), (SUBLANES, N_EXPERTS))
    earlier = jnp.tri(tm, k=-1, dtype=BF16)
    return pl.pallas_call(
        functools.partial(_route_kernel, bm=bm),
        out_shape=jax.ShapeDtypeStruct((nt, TOP_K_EXPERTS), jnp.int32),
        grid=(nt // tm,),
        in_specs=[pl.BlockSpec((tm, TOP_K_EXPERTS), lambda i: (i, 0)),
                  pl.BlockSpec((SUBLANES, N_EXPERTS), lambda i: (0, 0)),
                  pl.BlockSpec((tm, tm), lambda i: (0, 0))],
        out_specs=pl.BlockSpec((tm, TOP_K_EXPERTS), lambda i: (i, 0)),
        scratch_shapes=[pltpu.VMEM((SUBLANES, N_EXPERTS), F32)],
        compiler_params=_params("arbitrary"),
        name="moe_route",
    )(eidx, cnt8, earlier)


def _sc_mesh():
    return plsc.VectorSubcoreMesh(core_axis_name="core", subcore_axis_name="subcore", num_cores=V7X_SC_CORES,
                                  num_subcores=V7X_SC_SUBCORES)


def _sc_scatter_rows(rows, dest_kt, n_out):
    t, d = rows.shape
    n_k = dest_kt.shape[0]
    window = SC_WINDOW

    @functools.partial(pl.kernel, out_type=jax.ShapeDtypeStruct((n_out, d), rows.dtype), mesh=_sc_mesh(),
                       name="moe_dispatch_scatter")
    def scatter(x_hbm, i_hbm, o_hbm):
        def body(x_vmem, i_vmem):
            for k in range(n_k):
                pltpu.sync_copy(x_vmem, o_hbm.at[i_vmem.at[k]])

        pltpu.emit_pipeline(
            body,
            grid=(t // window,),
            in_specs=[pl.BlockSpec((window, d), lambda i: (i, 0)), pl.BlockSpec((n_k, window), lambda i: (0, i))],
            out_specs=[],
            core_axis_name=("core", "subcore"),
            dimension_semantics=(pltpu.PARALLEL,),
        )(x_hbm, i_hbm)

    return scatter(rows, dest_kt)


def _sc_gather_rows(table, idx):
    n = idx.shape[0]
    d = table.shape[1]
    window = SC_WINDOW

    @functools.partial(pl.kernel, out_type=jax.ShapeDtypeStruct((n, d), table.dtype), mesh=_sc_mesh(),
                       name="moe_combine_gather")
    def gather(tab_hbm, i_hbm, o_hbm):
        def body(i_vmem, o_vmem):
            pltpu.sync_copy(tab_hbm.at[i_vmem.at[0]], o_vmem)

        pltpu.emit_pipeline(
            body,
            grid=(n // window,),
            in_specs=[pl.BlockSpec((1, window), lambda i: (0, i))],
            out_specs=[pl.BlockSpec((window, d), lambda i: (i, 0))],
            core_axis_name=("core", "subcore"),
            dimension_semantics=(pltpu.PARALLEL,),
        )(i_hbm, o_hbm)

    return gather(table, idx.reshape(1, n))


def _ffn_kernel(be_ref, nused_ref, xa_ref, xb_ref, w1_ref, w3_ref, w2_ref, *refs):
    ys_refs, (w13_s, w2_s) = refs[:-2], refs[-2:]
    b = pl.program_id(0)
    f = w2_ref.shape[1]

    @pl.when((b == 0) | (be_ref[b] != be_ref[jnp.maximum(b - 1, 0)]))
    def _():
        w13_s[:, :f] = w1_ref[0].astype(BF16)
        w13_s[:, f:] = w3_ref[0].astype(BF16)
        w2_s[...] = w2_ref[0].astype(BF16)

    @pl.when(b < nused_ref[0])
    def _():
        cols = []
        for x_ref in (xa_ref, xb_ref):
            xu = x_ref[...]
            cols.append(lax.bitcast_convert_type(lax.shift_left(xu, 16), F32).astype(BF16))
            cols.append(lax.bitcast_convert_type(xu & HIGH_HALF, F32).astype(BF16))
        hh = _dot(jnp.concatenate(cols, axis=1), w13_s[...])
        act = _silu(hh[:, :f]) * hh[:, f:]
        y = _dot(act.astype(BF16), w2_s[...])
        bits = lax.bitcast_convert_type(y.astype(BF16).astype(F32), jnp.int32)
        for j, y_ref in enumerate(ys_refs):
            c0 = 2 * j * SC_ROW_WORDS
            low = lax.shift_right_logical(bits[:, c0:c0 + SC_ROW_WORDS], 16)
            y_ref[...] = low | (bits[:, c0 + SC_ROW_WORDS:c0 + 2 * SC_ROW_WORDS] & HIGH_HALF)


def _ffn(block_e, n_used, xs_a, xs_b, w1, w3, w2, bm):
    rows, q = xs_a.shape
    _, d, f = w1.shape
    n_out = d // (2 * SC_ROW_WORDS)
    grid_spec = pltpu.PrefetchScalarGridSpec(
        num_scalar_prefetch=2,
        grid=(rows // bm,),
        in_specs=[pl.BlockSpec((bm, q), lambda b, be, nu: (b, 0)),
                  pl.BlockSpec((bm, q), lambda b, be, nu: (b, 0)),
                  pl.BlockSpec((1, d, f), lambda b, be, nu: (be[b], 0, 0)),
                  pl.BlockSpec((1, d, f), lambda b, be, nu: (be[b], 0, 0)),
                  pl.BlockSpec((1, f, d), lambda b, be, nu: (be[b], 0, 0))],
        out_specs=tuple(pl.BlockSpec((bm, SC_ROW_WORDS), lambda b, be, nu: (b, 0)) for _ in range(n_out)),
        scratch_shapes=[pltpu.VMEM((d, 2 * f), BF16), pltpu.VMEM((f, d), BF16)],
    )
    return pl.pallas_call(
        _ffn_kernel,
        out_shape=tuple(jax.ShapeDtypeStruct((rows, SC_ROW_WORDS), jnp.int32) for _ in range(n_out)),
        grid_spec=grid_spec,
        compiler_params=_params("arbitrary"),
        name="moe_expert_ffn",
    )(block_e, n_used, xs_a, xs_b, w1, w3, w2)


def _combine_kernel(*refs):
    y_refs = refs[:-7]
    g8_ref, h2_ref, x1_ref, gate2_ref, ws13_ref, ws2_ref, o_ref = refs[-7:]
    f = ws2_ref.shape[0]
    hs = _dot(h2_ref[...], ws13_ref[...])
    acc = _dot((_silu(hs[:, :f]) * hs[:, f:]).astype(BF16), ws2_ref[...])
    g8 = g8_ref[...]
    for k in range(TOP_K_EXPERTS):
        cols = []
        for y_ref in y_refs:
            yu = y_ref[k]
            cols.append(lax.bitcast_convert_type(lax.shift_left(yu, 16), F32))
            cols.append(lax.bitcast_convert_type(yu & HIGH_HALF, F32))
        acc = acc + g8[:, k:k + 1] * jnp.concatenate(cols, axis=1)
    o_ref[...] = x1_ref[...] + gate2_ref[0] * acc


def _combine(y8s, g8, h2, x1, gate2, ws13, ws2b, seq, tm):
    t, d = x1.shape
    per_b = seq // tm
    row = lambda w: pl.BlockSpec((tm, w), lambda i: (i, 0))
    full = lambda a: pl.BlockSpec(a.shape, lambda i: (0,) * a.ndim)
    y_spec = pl.BlockSpec((TOP_K_EXPERTS, tm, SC_ROW_WORDS), lambda i: (0, i, 0))
    return pl.pallas_call(
        _combine_kernel,
        out_shape=jax.ShapeDtypeStruct((t, d), F32),
        grid=(t // tm,),
        in_specs=[y_spec] * len(y8s) + [row(TOP_K_EXPERTS), row(d), row(d),
                                         pl.BlockSpec((1, 1, d), lambda i: (i // per_b, 0, 0)), full(ws13), full(ws2b)],
        out_specs=row(d),
        compiler_params=_params("parallel"),
        name="moe_combine",
    )(*y8s, g8, h2, x1, gate2, ws13, ws2b)


def _moe(h2, h2pa, h2pb, x1, eidx, g8, counts, gate2, w1, w3, w2, ws1, ws3, ws2, seq):
    t, d = x1.shape
    n_e = w1.shape[0]
    bm = MOE_BLOCK_ROWS
    rows = t * TOP_K_EXPERTS + n_e * bm
    ws13 = jnp.concatenate([ws1, ws3], axis=1).astype(BF16)
    ws2b = ws2.astype(BF16)

    dest = _route(eidx, counts, bm, min(TOKEN_TILE, t))
    pend = jnp.cumsum(jnp.ceil(counts / bm) * bm)
    block_row0 = jnp.arange(rows // bm, dtype=F32) * bm
    block_e = jnp.minimum(jnp.sum(pend[None, :] <= block_row0[:, None], axis=1), n_e - 1)
    n_used = (pend[-1:] / bm).astype(jnp.int32)
    dest_kt = dest.T

    xs_a = _sc_scatter_rows(h2pa, dest_kt, rows)
    xs_b = _sc_scatter_rows(h2pb, dest_kt, rows)
    ys = _ffn(block_e.astype(jnp.int32), n_used, xs_a, xs_b, w1, w3, w2, bm)
    pair_rows = dest_kt.reshape(-1)
    y8s = [_sc_gather_rows(y, pair_rows).reshape(TOP_K_EXPERTS, t, SC_ROW_WORDS) for y in ys]
    return _combine(y8s, g8, h2, x1, gate2, ws13, ws2b, seq, min(COMBINE_TILE, seq))


def _layer(x, c, positions, norm1_g, norm2_g, w_ada, b_ada, w_in, conv_w, q_norm_g, k_norm_g, kidx_norm_g, w_out,
           w_router, router_bias, w1, w3, w2, ws1, ws3, ws2):
    bsz, seq, d = x.shape
    t = bsz * seq
    tm = min(TOKEN_TILE, seq)
    tq = min(QUERY_TILE, seq)
    x2 = x.reshape(t, d)

    ada = _ada(c, w_ada, b_ada)
    shift1, scale1, gate1, shift2, scale2, gate2 = [a.reshape(bsz, 1, d) for a in jnp.split(ada, 6, axis=-1)]

    inv_freq = ROPE_THETA ** (-jnp.arange(ROPE_HALF, dtype=F32) / ROPE_HALF)
    invf_lane = inv_freq[(jnp.arange(LANES) % HEAD_DIM) % ROPE_HALF].reshape(1, LANES)
    ct, s1, s2 = _rope_tables(positions.reshape(t, 1), invf_lane, min(ROPE_TILE, t))

    conv, q, k, v, qi, kilo, kihi, wit = _inproj(x2, scale1, shift1, norm1_g, w_in, conv_w, q_norm_g, k_norm_g,
                                                kidx_norm_g, ct, s1, s2, seq, tm)
    attn = _attention(q, k, v, qi, kilo, kihi, wit, bsz, seq, tq, kc=tq)
    x1, h2, h2pa, h2pb, eidx, g8, cnt = _outproj(conv, attn, x2, gate1, scale2, shift2, norm2_g, w_out, w_router,
                                                 router_bias, seq, tm)
    out = _moe(h2, h2pa, h2pb, x1, eidx, g8, cnt[:, 0], gate2, w1, w3, w2, ws1, ws3, ws2, seq)
    return out.reshape(bsz, seq, d)


def kernel(x, c, positions, norm1_g, norm2_g, w_ada, b_ada, w_in, conv_w, q_norm_g, k_norm_g, kidx_norm_g, w_out,
           w_router, router_bias, w1, w3, w2, ws1, ws3, ws2):
    for l in range(w_in.shape[0]):
        x = _layer(x, c, positions, norm1_g[l], norm2_g[l], w_ada[l], b_ada[l], w_in[l], conv_w[l], q_norm_g[l],
                   k_norm_g[l], kidx_norm_g[l], w_out[l], w_router[l], router_bias[l], w1[l], w3[l], w2[l], ws1[l],
                   ws3[l], ws2[l])
    return x
```

```python
import functools

import jax
import jax.numpy as jnp
from jax import lax
from jax.experimental import pallas as pl
from jax.experimental.pallas import tpu as pltpu
from jax.experimental.pallas import tpu_sc as plsc

F32 = jnp.float32
BF16 = jnp.bfloat16

HEAD_DIM = 64
ATTN_HEADS = 8
ATTN_DIM = ATTN_HEADS * HEAD_DIM
CONV_DIM = 512
IDX_HEADS = 8
IDX_DIM = 64
IDX_SCALE = (IDX_DIM ** -0.5) * (IDX_HEADS ** -0.5)
TOPK_KEYS_MAX = 256
ROPE_THETA = 500000.0
ROPE_DIM = HEAD_DIM // 4
ROPE_HALF = ROPE_DIM // 2
N_EXPERTS = 64
TOP_K_EXPERTS = 8
ROUTED_SCALE = 2.5
EPS = 1e-6

LANES = 128
SUBLANES = 8
V7X_VMEM_BYTES = 64 * 1024 * 1024
VMEM_LIMIT = V7X_VMEM_BYTES * 3 // 4
V7X_SC_CORES = 2
V7X_SC_SUBCORES = 16
SC_WINDOW = 128
SC_ROW_WORDS = 256

TOKEN_TILE = 512
QUERY_TILE = 512
COMBINE_TILE = 512
ROPE_TILE = 2048

HIGH_HALF = -65536
MASKED = -1e30
Q_SCALE = HEAD_DIM ** -0.5 * 1.4426950408889634
MOE_BLOCK_ROWS = 1024


def _params(*semantics):
    return pltpu.CompilerParams(dimension_semantics=semantics, vmem_limit_bytes=VMEM_LIMIT)


def _dot(a, b):
    return jnp.dot(a, b, preferred_element_type=F32)


def _dot_t(a, b):
    return lax.dot_general(a, b, (((1,), (1,)), ((), ())), preferred_element_type=F32)


def _split(a):
    hi = a.astype(BF16)
    lo = (a - hi.astype(F32)).astype(BF16)
    return hi, lo


def _dot3(a, b):
    a_hi, a_lo = _split(a)
    b_hi, b_lo = _split(b)
    return _dot(a_hi, b_hi) + _dot(a_hi, b_lo) + _dot(a_lo, b_hi)


def _silu(v):
    return v * jax.nn.sigmoid(v)


def _rms_mod(xv, g, scale, shift):
    ms = jnp.mean(xv * xv, axis=-1, keepdims=True)
    y = xv * lax.rsqrt(ms + EPS)
    return (y * g) * (1.0 + scale) + shift


def _ada_kernel(c_ref, w_ref, b_ref, o_ref):
    o_ref[...] = _dot3(_silu(c_ref[...]), w_ref[...]) + b_ref[...]


def _ada(c, w_ada, b_ada):
    bsz, d = c.shape
    n = w_ada.shape[1]
    bn = n // 4
    return pl.pallas_call(
        _ada_kernel,
        out_shape=jax.ShapeDtypeStruct((bsz, n), F32),
        grid=(n // bn,),
        in_specs=[
            pl.BlockSpec((bsz, d), lambda i: (0, 0)),
            pl.BlockSpec((d, bn), lambda i: (0, i)),
            pl.BlockSpec((1, bn), lambda i: (0, i)),
        ],
        out_specs=pl.BlockSpec((bsz, bn), lambda i: (0, i)),
        compiler_params=_params("parallel"),
        name="ada",
    )(c, w_ada, b_ada.reshape(1, n))


def _rope_kernel(pos_ref, invf_ref, c_ref, s1_ref, s2_ref):
    ang = pos_ref[...].astype(F32) * invf_ref[...]
    d = lax.broadcasted_iota(jnp.int32, ang.shape, 1) & (HEAD_DIM - 1)
    cos = jnp.cos(ang)
    sin = jnp.sin(ang)
    c_ref[...] = jnp.where(d < ROPE_DIM, cos, 1.0)
    s1_ref[...] = jnp.where(d < ROPE_HALF, -sin, 0.0)
    s2_ref[...] = jnp.where(d < ROPE_HALF, 0.0, jnp.where(d < ROPE_DIM, sin, 0.0))


def _rope_tables(pos, invf_lane, tm):
    t = pos.shape[0]
    spec = pl.BlockSpec((tm, LANES), lambda i: (i, 0))
    shp = jax.ShapeDtypeStruct((t, LANES), F32)
    return pl.pallas_call(
        _rope_kernel,
        out_shape=(shp, shp, shp),
        grid=(t // tm,),
        in_specs=[pl.BlockSpec((tm, 1), lambda i: (i, 0)), pl.BlockSpec((1, LANES), lambda i: (0, 0))],
        out_specs=(spec, spec, spec),
        compiler_params=_params("parallel"),
        name="rope_tables",
    )(pos, invf_lane)


def _rope(y, c, s1, s2):
    return y * c + pltpu.roll(y, LANES - ROPE_HALF, 1) * s1 + pltpu.roll(y, ROPE_HALF, 1) * s2


def _head_rms(xb, avg2):
    hi, lo = _split(xb * xb)
    ms = _dot(jnp.concatenate([hi, lo], axis=1), avg2)
    return xb * lax.rsqrt(ms + EPS)


def _inproj_kernel(x_ref, xh_ref, sc_ref, sh_ref, g1_ref, wmix_ref, wq_ref, wk_ref, wv_ref, wqi_ref, wl_ref, wlt_ref,
                   cw_ref, qg_ref, kg_ref, kig_ref, ct_ref, s1_ref, s2_ref, avg_ref,
                   conv_o, q_o, k_o, v_o, qi_o, kilo_o, kihi_o, wit_o, *, seq):
    tm = x_ref.shape[0]
    scale = sc_ref[0]
    shift = sh_ref[0]
    g1 = g1_ref[...]
    h = _rms_mod(x_ref[...], g1, scale, shift).astype(BF16)
    hh = _rms_mod(xh_ref[...], g1, scale, shift).astype(BF16)

    mix = _dot(h, wmix_ref[...])
    mixh = _dot(hh, wmix_ref[...])
    u = mix[:, 2 * CONV_DIM:] * mix[:, :CONV_DIM]
    uh = mixh[:, 2 * CONV_DIM:] * mixh[:, :CONV_DIM]
    seq_start = (pl.program_id(0) * tm) % seq == 0
    uh = jnp.where(seq_start, 0.0, uh)
    ext = jnp.concatenate([uh, u], axis=0)
    u1 = pltpu.roll(ext, 1, 0)[SUBLANES:]
    u2 = pltpu.roll(ext, 2, 0)[SUBLANES:]
    conv = u2 * cw_ref[0:1, :] + u1 * cw_ref[1:2, :] + u * cw_ref[2:3, :]
    conv_o[...] = (mix[:, CONV_DIM:2 * CONV_DIM] * conv).astype(BF16)

    ct = ct_ref[...]
    s1 = s1_ref[...]
    s2 = s2_ref[...]
    avg = avg_ref[...]
    qf = _dot(h, wq_ref[...])
    kf = _dot(h, wk_ref[...])
    qif = _dot(h, wqi_ref[...])
    for p in range(ATTN_DIM // LANES):
        sl = slice(p * LANES, (p + 1) * LANES)
        qn = _rope(_head_rms(qf[:, sl], avg) * qg_ref[...], ct, s1, s2)
        q_o[:, sl] = (qn * Q_SCALE).astype(BF16)
        kn = _rope(_head_rms(kf[:, sl], avg) * kg_ref[...], ct, s1, s2)
        k_o[:, sl] = kn.astype(BF16)
        qi_o[:, sl] = _rope(qif[:, sl], ct, s1, s2).astype(BF16)
    v_o[...] = _dot(h, wv_ref[...]).astype(BF16)

    wit_o[...] = _dot_t(wlt_ref[...], h)[IDX_DIM:IDX_DIM + IDX_HEADS, :]
    last = _dot(h, wl_ref[...])
    lane = lax.broadcasted_iota(jnp.int32, last.shape, 1)
    is_key = lane < IDX_DIM
    kin = _head_rms(last, avg) * kig_ref[...]
    kin = _rope(kin, jnp.where(is_key, ct, 1.0), jnp.where(is_key, s1, 0.0), jnp.where(is_key, s2, 0.0))
    klo = jnp.where(is_key, kin, 0.0)
    kilo_o[...] = klo.astype(BF16)
    kihi_o[...] = pltpu.roll(klo, IDX_DIM, 1).astype(BF16)


def _inproj(x2, scale1, shift1, norm1_g, w_in, conv_w, q_norm_g, k_norm_g, kidx_norm_g, ct, s1, s2, seq, tm):
    t, d = x2.shape
    cuts = [0, 3 * CONV_DIM, 3 * CONV_DIM + ATTN_DIM, 3 * CONV_DIM + 2 * ATTN_DIM, 3 * CONV_DIM + 3 * ATTN_DIM,
            3 * CONV_DIM + 3 * ATTN_DIM + IDX_HEADS * IDX_DIM]
    wb = w_in.astype(BF16)
    wmix, wq, wk, wv, wqi = [wb[:, a:b] for a, b in zip(cuts[:-1], cuts[1:])]
    wl = wb[:, cuts[-1]:]
    wl = jnp.pad(wl, ((0, 0), (0, LANES - wl.shape[1])))
    wlt = wl.T
    ones = jnp.ones((1, LANES - IDX_DIM), F32)
    qg = jnp.tile(q_norm_g.reshape(1, HEAD_DIM), (1, 2))
    kg = jnp.tile(k_norm_g.reshape(1, HEAD_DIM), (1, 2))
    kig = jnp.concatenate([kidx_norm_g.reshape(1, IDX_DIM), ones], axis=1)
    blk = jnp.arange(LANES) // HEAD_DIM
    avg = jnp.where(blk[:, None] == blk[None, :], 1.0 / HEAD_DIM, 0.0).astype(BF16)
    avg = jnp.concatenate([avg, avg], axis=0)

    bsz = t // seq
    per_b = seq // tm
    row = lambda w: pl.BlockSpec((tm, w), lambda i: (i, 0))
    full = lambda a: pl.BlockSpec(a.shape, lambda i: (0,) * a.ndim)
    mod = pl.BlockSpec((1, 1, d), lambda i: (i // per_b, 0, 0))
    halo = pl.BlockSpec((SUBLANES, d), lambda i: (jnp.maximum(i * (tm // SUBLANES) - 1, 0), 0))
    g1 = norm1_g.reshape(1, d)
    out_shape = (
        jax.ShapeDtypeStruct((t, CONV_DIM), BF16),
        jax.ShapeDtypeStruct((t, ATTN_DIM), BF16),
        jax.ShapeDtypeStruct((t, ATTN_DIM), BF16),
        jax.ShapeDtypeStruct((t, ATTN_DIM), BF16),
        jax.ShapeDtypeStruct((t, IDX_HEADS * IDX_DIM), BF16),
        jax.ShapeDtypeStruct((t, LANES), BF16),
        jax.ShapeDtypeStruct((t, LANES), BF16),
        jax.ShapeDtypeStruct((IDX_HEADS, t), F32),
    )
    del bsz
    return pl.pallas_call(
        functools.partial(_inproj_kernel, seq=seq),
        out_shape=out_shape,
        grid=(t // tm,),
        in_specs=[row(d), halo, mod, mod, full(g1), full(wmix), full(wq), full(wk), full(wv), full(wqi), full(wl),
                  full(wlt), full(conv_w), full(qg), full(kg), full(kig), row(LANES), row(LANES), row(LANES), full(avg)],
        out_specs=(row(CONV_DIM), row(ATTN_DIM), row(ATTN_DIM), row(ATTN_DIM), row(IDX_HEADS * IDX_DIM),
                   row(LANES), row(LANES), pl.BlockSpec((IDX_HEADS, tm), lambda i: (0, i))),
        compiler_params=_params("parallel"),
        name="inproj",
    )(x2, x2, scale1, shift1, g1, wmix, wq, wk, wv, wqi, wl, wlt, conv_w, qg, kg, kig, ct, s1, s2, avg)


def _ukey_to_f32(u):
    s = u ^ jnp.int32(-2 ** 31)
    bits = s ^ ((s >> 31) & jnp.int32(0x7FFFFFFF))
    return lax.bitcast_convert_type(bits, F32)


def _attn_kernel(q_ref, k_ref, v_ref, qi_ref, kilo_ref, kihi_ref, wit_ref, o_ref,
                 sc_ref, sc16_ref, qm_ref, acc_ref, m_ref, l_ref, *, n_sel):
    tq = q_ref.shape[0]
    seq = k_ref.shape[0]
    kc = sc_ref.shape[1]
    assert kc == tq
    lb_n = kc // LANES
    j = pl.program_id(1)
    nch = (j + 1) * (tq // kc)
    lane = lax.broadcasted_iota(jnp.int32, (tq, LANES), 1)

    for h in range(ATTN_HEADS):
        qp = q_ref[:, (h // 2) * LANES:(h // 2 + 1) * LANES].astype(F32)
        keep = (lane < HEAD_DIM) if h % 2 == 0 else (lane >= HEAD_DIM)
        qm_ref[h] = jnp.where(keep, qp, 0.0).astype(BF16)

    q_pos = j * tq + lax.broadcasted_iota(jnp.int32, (kc, tq), 1)
    k_off = lax.broadcasted_iota(jnp.int32, (kc, tq), 0)
    w_rows = wit_ref[...]

    def index_chunk(c, diagonal):
        r0 = pl.multiple_of(c * kc, kc)
        klo = kilo_ref[pl.ds(r0, kc), :]
        khi = kihi_ref[pl.ds(r0, kc), :]
        acc = jnp.zeros((kc, tq), F32)
        for p in range(IDX_HEADS // 2):
            qip = qi_ref[:, p * LANES:(p + 1) * LANES]
            acc = acc + jnp.maximum(_dot_t(klo, qip), 0.0) * w_rows[2 * p:2 * p + 1, :]
            acc = acc + jnp.maximum(_dot_t(khi, qip), 0.0) * w_rows[2 * p + 1:2 * p + 2, :]
        score = acc * IDX_SCALE
        if diagonal:
            score = jnp.where(r0 + k_off <= q_pos, score, -jnp.inf)
        score = jnp.where(score == 0.0, 0.0, score)
        sc_ref[c] = score
        top = lax.bitcast_convert_type(score, jnp.int32) & HIGH_HALF
        sc16_ref[c] = lax.bitcast_convert_type(top, F32).astype(BF16)

    def index_full_chunk(c, carry):
        index_chunk(c, False)
        return carry

    lax.fori_loop(0, nch - 1, index_full_chunk, 0)
    index_chunk(nch - 1, True)

    t_q = j * tq + lax.broadcasted_iota(jnp.int32, (1, tq), 1)
    k_row = jnp.minimum(t_q + 1, n_sel).astype(F32)
    acc_rows = 4 * SUBLANES
    sub = lax.broadcasted_iota(jnp.int32, (acc_rows, tq), 0)
    idx_bits = (seq - 1).bit_length()

    def count(pred):
        def chunk(c, acc):
            blk = sc_ref[c]
            for g in range(kc // acc_rows):
                kidx = sub + (c * kc + g * acc_rows)
                acc = acc + jnp.where(pred(blk[g * acc_rows:(g + 1) * acc_rows, :], kidx), 1.0, 0.0)
            return acc

        acc = lax.fori_loop(0, nch, chunk, jnp.zeros((acc_rows, tq), F32))
        return jnp.sum(acc, axis=0, keepdims=True)

    def over_rows(v):
        return jnp.broadcast_to(v, (acc_rows, tq))

    def count16(cand16):
        def chunk(c, acc):
            blk = sc16_ref[c]
            for g in range(kc // acc_rows):
                hit = blk[g * acc_rows:(g + 1) * acc_rows, :] >= cand16
                acc = acc + jnp.where(hit, jnp.ones_like(cand16), jnp.zeros_like(cand16))
            return acc

        acc = lax.fori_loop(0, nch, chunk, jnp.zeros((acc_rows, tq), BF16))
        return jnp.sum(acc.astype(F32), axis=0, keepdims=True)

    def coarse_bit(i, carry):
        p, n_p = carry
        cand = p | jnp.left_shift(jnp.int32(1), 31 - i)
        top = lax.bitcast_convert_type(_ukey_to_f32(cand), jnp.int32) & HIGH_HALF
        cnt = count16(over_rows(lax.bitcast_convert_type(top, F32)).astype(BF16))
        keep = cnt >= k_row
        return jnp.where(keep, cand, p), jnp.where(keep, cnt, n_p)

    def value_bit(i, carry):
        p, n_p = carry
        cand = p | jnp.left_shift(jnp.int32(1), 31 - i)
        cand_f = over_rows(_ukey_to_f32(cand))
        cnt = count(lambda v, kidx: v >= cand_f)
        keep = cnt >= k_row
        return jnp.where(keep, cand, p), jnp.where(keep, cnt, n_p)

    start = (jnp.zeros((1, tq), jnp.int32), jnp.zeros((1, tq), F32))
    p_thr, n_ge = lax.fori_loop(16, 32, value_bit, lax.fori_loop(0, 16, coarse_bit, start))
    thr = _ukey_to_f32(p_thr)
    thr_rows = over_rows(thr)
    tied = jnp.max(jnp.where(n_ge > k_row, 1.0, 0.0)) > 0.0

    def tie_cut():
        need = k_row - count(lambda v, kidx: v > thr_rows)

        def index_bit(i, p):
            cand = p | jnp.left_shift(jnp.int32(1), idx_bits - 1 - i)
            cand_rows = over_rows(cand)
            cnt = count(lambda v, kidx: jnp.where(v == thr_rows, kidx, seq) < cand_rows)
            return jnp.where(cnt < need, cand, p)

        return lax.fori_loop(0, idx_bits, index_bit, jnp.zeros((1, tq), jnp.int32))

    cut = lax.cond(tied, tie_cut, lambda: jnp.full((1, tq), seq, jnp.int32))

    def write_bias_tied(c, carry):
        blk = sc_ref[c]
        tie_bias = jnp.where(c * kc + k_off <= cut, 0.0, MASKED)
        sc_ref[c] = jnp.where(blk > thr, 0.0, jnp.where(blk == thr, tie_bias, MASKED)).T
        return carry

    def write_bias(c, carry):
        sc_ref[c] = jnp.where(sc_ref[c] >= thr, 0.0, MASKED).T
        return carry

    @pl.when(tied)
    def _():
        lax.fori_loop(0, nch, write_bias_tied, 0)

    @pl.when(jnp.logical_not(tied))
    def _():
        lax.fori_loop(0, nch, write_bias, 0)

    for h in range(ATTN_HEADS):
        m_ref[h] = jnp.full((tq, LANES), MASKED, F32)
        l_ref[h] = jnp.zeros((tq, LANES), F32)
        acc_ref[h] = jnp.zeros((tq, LANES), F32)

    def attend(c, carry):
        r0 = pl.multiple_of(c * kc, kc)
        keys = pl.ds(r0, kc)
        bias = sc_ref[c]

        def qk(h):
            return _dot_t(qm_ref[h], k_ref[keys, (h // 2) * LANES:(h // 2 + 1) * LANES])

        s_next = qk(0)
        for h in range(ATTN_HEADS):
            s = s_next + bias
            if h + 1 < ATTN_HEADS:
                s_next = qk(h + 1)
            parts = [s[:, b * LANES:(b + 1) * LANES] for b in range(lb_n)]
            m_old = m_ref[h]
            row_max = jnp.max(functools.reduce(jnp.maximum, parts), axis=1, keepdims=True)
            m_new = jnp.maximum(m_old, row_max)
            alpha = jnp.exp2(m_old - m_new)
            p_parts = [jnp.exp2(part - m_new) for part in parts]
            l_ref[h] = alpha * l_ref[h] + functools.reduce(jnp.add, p_parts)
            p = jnp.concatenate(p_parts, axis=1).astype(BF16)
            acc_ref[h] = alpha * acc_ref[h] + _dot(p, v_ref[keys, (h // 2) * LANES:(h // 2 + 1) * LANES])
            m_ref[h] = m_new
        return carry

    lax.fori_loop(0, nch, attend, 0)
    for pair in range(ATTN_HEADS // 2):
        l_even = jnp.sum(l_ref[2 * pair], axis=1, keepdims=True)
        l_odd = jnp.sum(l_ref[2 * pair + 1], axis=1, keepdims=True)
        o_pair = jnp.where(lane < HEAD_DIM, acc_ref[2 * pair] / l_even, acc_ref[2 * pair + 1] / l_odd)
        o_ref[:, pair * LANES:(pair + 1) * LANES] = o_pair.astype(BF16)


def _attention(q, k, v, qi, kilo, kihi, wit, bsz, seq, tq, kc):
    n_sel = min(TOPK_KEYS_MAX, seq // 4)
    shape3 = lambda a: a.reshape(bsz, seq, a.shape[-1])
    q, k, v, qi, kilo, kihi = map(shape3, (q, k, v, qi, kilo, kihi))
    qblk = lambda w: pl.BlockSpec((None, tq, w), lambda b, j: (b, j, 0))
    kblk = lambda w: pl.BlockSpec((None, seq, w), lambda b, j: (b, 0, 0))
    out = pl.pallas_call(
        functools.partial(_attn_kernel, n_sel=n_sel),
        out_shape=jax.ShapeDtypeStruct((bsz, seq, ATTN_DIM), BF16),
        grid=(bsz, seq // tq),
        in_specs=[qblk(ATTN_DIM), kblk(ATTN_DIM), kblk(ATTN_DIM), qblk(IDX_HEADS * IDX_DIM), kblk(LANES), kblk(LANES),
                  pl.BlockSpec((IDX_HEADS, tq), lambda b, j: (0, b * (seq // tq) + j))],
        out_specs=qblk(ATTN_DIM),
        scratch_shapes=[
            pltpu.VMEM((seq // kc, kc, tq), F32),
            pltpu.VMEM((seq // kc, kc, tq), BF16),
            pltpu.VMEM((ATTN_HEADS, tq, LANES), BF16),
            pltpu.VMEM((ATTN_HEADS, tq, LANES), F32),
            pltpu.VMEM((ATTN_HEADS, tq, LANES), F32),
            pltpu.VMEM((ATTN_HEADS, tq, LANES), F32),
        ],
        compiler_params=_params("parallel", "arbitrary"),
        name="dsa_attention",
    )(q, k, v, qi, kilo, kihi, wit)
    return out.reshape(bsz * seq, ATTN_DIM)


def _outproj_kernel(conv_ref, attn_ref, x_ref, gate1_ref, sc_ref, sh_ref, g2_ref, wout_ref, wrt_ref, rbias_ref,
                    x1_o, h2_o, h2pa_o, h2pb_o, eidx_o, g8_o, cnt_o):
    mix = _dot(conv_ref[...], wout_ref[:CONV_DIM, :]) + _dot(attn_ref[...], wout_ref[CONV_DIM:, :])
    x1 = x_ref[...] + gate1_ref[0] * mix
    x1_o[...] = x1
    h2 = _rms_mod(x1, g2_ref[...], sc_ref[0], sh_ref[0])
    h2b = h2.astype(BF16)
    h2_o[...] = h2b
    quarter = h2.shape[1] // 4
    bits = lax.bitcast_convert_type(h2b.astype(F32), jnp.int32)
    for half_o, c0 in ((h2pa_o, 0), (h2pb_o, 2 * quarter)):
        low = lax.shift_right_logical(bits[:, c0:c0 + quarter], 16)
        half_o[...] = low | (bits[:, c0 + quarter:c0 + 2 * quarter] & HIGH_HALF)

    wrt = wrt_ref[...]
    wrt_hi, wrt_lo = _split(wrt)
    h2_hi, h2_lo = _split(h2)
    scores = jax.nn.sigmoid(_dot_t(wrt_hi, h2_hi) + _dot_t(wrt_hi, h2_lo) + _dot_t(wrt_lo, h2_hi))
    work = scores + rbias_ref[...]
    n_e, tm = work.shape
    row_in_tile = lax.broadcasted_iota(jnp.int32, (SUBLANES, tm), 0)
    tiles = [work[j * SUBLANES:(j + 1) * SUBLANES, :] for j in range(n_e // SUBLANES)]
    rank = [jnp.zeros((SUBLANES, tm), F32) for _ in tiles]
    for e2 in range(n_e):
        other = jnp.broadcast_to(work[e2:e2 + 1, :], (SUBLANES, tm))
        for j, tile in enumerate(tiles):
            if j > e2 // SUBLANES:
                beats = other >= tile
            elif j < e2 // SUBLANES:
                beats = other > tile
            else:
                later = row_in_tile > e2 % SUBLANES
                beats = jnp.where(later, jnp.where(other >= tile, 1.0, 0.0), jnp.where(other > tile, 1.0, 0.0)) > 0.0
            rank[j] = rank[j] + jnp.where(beats, 1.0, 0.0)
    chosen = jnp.concatenate([jnp.where(r < float(TOP_K_EXPERTS), 1.0, 0.0) for r in rank], axis=0)

    @pl.when(pl.program_id(0) == 0)
    def _():
        cnt_o[...] = jnp.zeros(cnt_o.shape, F32)

    cnt_o[...] += jnp.sum(chosen, axis=1, keepdims=True)
    eidx_rows, pick_rows = [], []
    for k in range(TOP_K_EXPERTS):
        e_sum = jnp.zeros((SUBLANES, tm), F32)
        s_sum = jnp.zeros((SUBLANES, tm), F32)
        for j in range(len(tiles)):
            is_k = rank[j] == float(k)
            e_sum = e_sum + jnp.where(is_k, (row_in_tile + j * SUBLANES).astype(F32), 0.0)
            s_sum = s_sum + jnp.where(is_k, scores[j * SUBLANES:(j + 1) * SUBLANES, :], 0.0)
        eidx_rows.append(jnp.sum(e_sum, axis=0, keepdims=True))
        pick_rows.append(jnp.sum(s_sum, axis=0, keepdims=True))
    eidx_t = jnp.concatenate(eidx_rows, axis=0)
    picked_t = jnp.concatenate(pick_rows, axis=0)
    g8_t = picked_t / jnp.sum(picked_t, axis=0, keepdims=True) * ROUTED_SCALE
    eidx_o[...] = eidx_t.T.astype(jnp.int32)
    g8_o[...] = g8_t.T


def _outproj(conv, attn, x2, gate1, scale2, shift2, norm2_g, w_out, w_router, router_bias, seq, tm):
    t, d = x2.shape
    e = w_router.shape[1]
    per_b = seq // tm
    row = lambda w: pl.BlockSpec((tm, w), lambda i: (i, 0))
    full = lambda a: pl.BlockSpec(a.shape, lambda i: (0,) * a.ndim)
    mod = pl.BlockSpec((1, 1, d), lambda i: (i // per_b, 0, 0))
    g2 = norm2_g.reshape(1, d)
    wo = w_out.astype(BF16)
    rbias = router_bias.reshape(e, 1)
    wrt = w_router.T
    return pl.pallas_call(
        _outproj_kernel,
        out_shape=(jax.ShapeDtypeStruct((t, d), F32), jax.ShapeDtypeStruct((t, d), BF16),
                   jax.ShapeDtypeStruct((t, d // 4), jnp.int32), jax.ShapeDtypeStruct((t, d // 4), jnp.int32),
                   jax.ShapeDtypeStruct((t, TOP_K_EXPERTS), jnp.int32), jax.ShapeDtypeStruct((t, TOP_K_EXPERTS), F32),
                   jax.ShapeDtypeStruct((e, LANES), F32)),
        grid=(t // tm,),
        in_specs=[row(CONV_DIM), row(ATTN_DIM), row(d), mod, mod, mod, full(g2), full(wo), full(wrt), full(rbias)],
        out_specs=(row(d), row(d), row(d // 4), row(d // 4), row(TOP_K_EXPERTS), row(TOP_K_EXPERTS),
                   pl.BlockSpec((e, LANES), lambda i: (0, 0))),
        compiler_params=_params("arbitrary"),
        name="outproj_router",
    )(conv, attn, x2, gate1, scale2, shift2, g2, wo, wrt, rbias)


def _route_kernel(eidx_ref, cnt_ref, earlier_ref, dest_o, run_ref, *, bm):
    tm = eidx_ref.shape[0]
    e8 = eidx_ref[...]
    lane = lax.broadcasted_iota(jnp.int32, (tm, N_EXPERTS), 1)
    hits = [lane == e8[:, k:k + 1] for k in range(TOP_K_EXPERTS)]
    member = functools.reduce(jnp.add, [jnp.where(hit, 1.0, 0.0) for hit in hits])

    @pl.when(pl.program_id(0) == 0)
    def _():
        blocks = jnp.ceil(cnt_ref[...] / bm)
        r = lax.broadcasted_iota(jnp.int32, (N_EXPERTS, N_EXPERTS), 0)
        c = lax.broadcasted_iota(jnp.int32, (N_EXPERTS, N_EXPERTS), 1)
        before = jnp.where(r < c, 1.0, 0.0).astype(BF16)
        b_hi, b_lo = _split(blocks)
        run_ref[...] = (_dot(b_hi, before) + _dot(b_lo, before)) * bm

    base = run_ref[0:1, :] + _dot(earlier_ref[...], member.astype(BF16))
    slot = lax.broadcasted_iota(jnp.int32, (tm, TOP_K_EXPERTS), 1)
    dest = jnp.zeros((tm, TOP_K_EXPERTS), F32)
    for k in range(TOP_K_EXPERTS):
        dest = jnp.where(slot == k, jnp.sum(jnp.where(hits[k], base, 0.0), axis=1, keepdims=True), dest)
    dest_o[...] = dest.astype(jnp.int32)
    run_ref[...] += jnp.sum(member, axis=0, keepdims=True)


def _route(eidx, counts, bm, tm):
    nt = eidx.shape[0]
    cnt8 = jnp.broadcast_to(counts.reshape(1, N_EXPERTS), (SUBLANES, N_EXPERTS))
    earlier = jnp.tri(tm, k=-1, dtype=BF16)
    return pl.pallas_call(
        functools.partial(_route_kernel, bm=bm),
        out_shape=jax.ShapeDtypeStruct((nt, TOP_K_EXPERTS), jnp.int32),
        grid=(nt // tm,),
        in_specs=[pl.BlockSpec((tm, TOP_K_EXPERTS), lambda i: (i, 0)),
                  pl.BlockSpec((SUBLANES, N_EXPERTS), lambda i: (0, 0)),
                  pl.BlockSpec((tm, tm), lambda i: (0, 0))],
        out_specs=pl.BlockSpec((tm, TOP_K_EXPERTS), lambda i: (i, 0)),
        scratch_shapes=[pltpu.VMEM((SUBLANES, N_EXPERTS), F32)],
        compiler_params=_params("arbitrary"),
        name="moe_route",
    )(eidx, cnt8, earlier)


def _sc_mesh():
    return plsc.VectorSubcoreMesh(core_axis_name="core", subcore_axis_name="subcore", num_cores=V7X_SC_CORES,
                                  num_subcores=V7X_SC_SUBCORES)


def _sc_scatter_rows(rows, dest_kt, n_out):
    t, d = rows.shape
    n_k = dest_kt.shape[0]
    window = SC_WINDOW

    @functools.partial(pl.kernel, out_type=jax.ShapeDtypeStruct((n_out, d), rows.dtype), mesh=_sc_mesh(),
                       name="moe_dispatch_scatter")
    def scatter(x_hbm, i_hbm, o_hbm):
        def body(x_vmem, i_vmem):
            for k in range(n_k):
                pltpu.sync_copy(x_vmem, o_hbm.at[i_vmem.at[k]])

        pltpu.emit_pipeline(
            body,
            grid=(t // window,),
            in_specs=[pl.BlockSpec((window, d), lambda i: (i, 0)), pl.BlockSpec((n_k, window), lambda i: (0, i))],
            out_specs=[],
            core_axis_name=("core", "subcore"),
            dimension_semantics=(pltpu.PARALLEL,),
        )(x_hbm, i_hbm)

    return scatter(rows, dest_kt)


def _sc_gather_rows(table, idx):
    n = idx.shape[0]
    d = table.shape[1]
    window = SC_WINDOW

    @functools.partial(pl.kernel, out_type=jax.ShapeDtypeStruct((n, d), table.dtype), mesh=_sc_mesh(),
                       name="moe_combine_gather")
    def gather(tab_hbm, i_hbm, o_hbm):
        def body(i_vmem, o_vmem):
            pltpu.sync_copy(tab_hbm.at[i_vmem.at[0]], o_vmem)

        pltpu.emit_pipeline(
            body,
            grid=(n // window,),
            in_specs=[pl.BlockSpec((1, window), lambda i: (0, i))],
            out_specs=[pl.BlockSpec((window, d), lambda i: (i, 0))],
            core_axis_name=("core", "subcore"),
            dimension_semantics=(pltpu.PARALLEL,),
        )(i_hbm, o_hbm)

    return gather(table, idx.reshape(1, n))


def _ffn_kernel(be_ref, nused_ref, xa_ref, xb_ref, w1_ref, w3_ref, w2_ref, *refs):
    ys_refs, (w13_s, w2_s) = refs[:-2], refs[-2:]
    b = pl.program_id(0)
    f = w2_ref.shape[1]

    @pl.when((b == 0) | (be_ref[b] != be_ref[jnp.maximum(b - 1, 0)]))
    def _():
        w13_s[:, :f] = w1_ref[0].astype(BF16)
        w13_s[:, f:] = w3_ref[0].astype(BF16)
        w2_s[...] = w2_ref[0].astype(BF16)

    @pl.when(b < nused_ref[0])
    def _():
        cols = []
        for x_ref in (xa_ref, xb_ref):
            xu = x_ref[...]
            cols.append(lax.bitcast_convert_type(lax.shift_left(xu, 16), F32).astype(BF16))
            cols.append(lax.bitcast_convert_type(xu & HIGH_HALF, F32).astype(BF16))
        hh = _dot(jnp.concatenate(cols, axis=1), w13_s[...])
        act = _silu(hh[:, :f]) * hh[:, f:]
        y = _dot(act.astype(BF16), w2_s[...])
        bits = lax.bitcast_convert_type(y.astype(BF16).astype(F32), jnp.int32)
        for j, y_ref in enumerate(ys_refs):
            c0 = 2 * j * SC_ROW_WORDS
            low = lax.shift_right_logical(bits[:, c0:c0 + SC_ROW_WORDS], 16)
            y_ref[...] = low | (bits[:, c0 + SC_ROW_WORDS:c0 + 2 * SC_ROW_WORDS] & HIGH_HALF)


def _ffn(block_e, n_used, xs_a, xs_b, w1, w3, w2, bm):
    rows, q = xs_a.shape
    _, d, f = w1.shape
    n_out = d // (2 * SC_ROW_WORDS)
    grid_spec = pltpu.PrefetchScalarGridSpec(
        num_scalar_prefetch=2,
        grid=(rows // bm,),
        in_specs=[pl.BlockSpec((bm, q), lambda b, be, nu: (b, 0)),
                  pl.BlockSpec((bm, q), lambda b, be, nu: (b, 0)),
                  pl.BlockSpec((1, d, f), lambda b, be, nu: (be[b], 0, 0)),
                  pl.BlockSpec((1, d, f), lambda b, be, nu: (be[b], 0, 0)),
                  pl.BlockSpec((1, f, d), lambda b, be, nu: (be[b], 0, 0))],
        out_specs=tuple(pl.BlockSpec((bm, SC_ROW_WORDS), lambda b, be, nu: (b, 0)) for _ in range(n_out)),
        scratch_shapes=[pltpu.VMEM((d, 2 * f), BF16), pltpu.VMEM((f, d), BF16)],
    )
    return pl.pallas_call(
        _ffn_kernel,
        out_shape=tuple(jax.ShapeDtypeStruct((rows, SC_ROW_WORDS), jnp.int32) for _ in range(n_out)),
        grid_spec=grid_spec,
        compiler_params=_params("arbitrary"),
        name="moe_expert_ffn",
    )(block_e, n_used, xs_a, xs_b, w1, w3, w2)


def _combine_kernel(*refs):
    y_refs = refs[:-7]
    g8_ref, h2_ref, x1_ref, gate2_ref, ws13_ref, ws2_ref, o_ref = refs[-7:]
    f = ws2_ref.shape[0]
    hs = _dot(h2_ref[...], ws13_ref[...])
    acc = _dot((_silu(hs[:, :f]) * hs[:, f:]).astype(BF16), ws2_ref[...])
    g8 = g8_ref[...]
    for k in range(TOP_K_EXPERTS):
        cols = []
        for y_ref in y_refs:
            yu = y_ref[k]
            cols.append(lax.bitcast_convert_type(lax.shift_left(yu, 16), F32))
            cols.append(lax.bitcast_convert_type(yu & HIGH_HALF, F32))
        acc = acc + g8[:, k:k + 1] * jnp.concatenate(cols, axis=1)
    o_ref[...] = x1_ref[...] + gate2_ref[0] * acc


def _combine(y8s, g8, h2, x1, gate2, ws13, ws2b, seq, tm):
    t, d = x1.shape
    per_b = seq // tm
    row = lambda w: pl.BlockSpec((tm, w), lambda i: (i, 0))
    full = lambda a: pl.BlockSpec(a.shape, lambda i: (0,) * a.ndim)
    y_spec = pl.BlockSpec((TOP_K_EXPERTS, tm, SC_ROW_WORDS), lambda i: (0, i, 0))
    return pl.pallas_call(
        _combine_kernel,
        out_shape=jax.ShapeDtypeStruct((t, d), F32),
        grid=(t // tm,),
        in_specs=[y_spec] * len(y8s) + [row(TOP_K_EXPERTS), row(d), row(d),
                                         pl.BlockSpec((1, 1, d), lambda i: (i // per_b, 0, 0)), full(ws13), full(ws2b)],
        out_specs=row(d),
        compiler_params=_params("parallel"),
        name="moe_combine",
    )(*y8s, g8, h2, x1, gate2, ws13, ws2b)


def _moe(h2, h2pa, h2pb, x1, eidx, g8, counts, gate2, w1, w3, w2, ws1, ws3, ws2, seq):
    t, d = x1.shape
    n_e = w1.shape[0]
    bm = MOE_BLOCK_ROWS
    rows = t * TOP_K_EXPERTS + n_e * bm
    ws13 = jnp.concatenate([ws1, ws3], axis=1).astype(BF16)
    ws2b = ws2.astype(BF16)

    dest = _route(eidx, counts, bm, min(TOKEN_TILE, t))
    pend = jnp.cumsum(jnp.ceil(counts / bm) * bm)
    block_row0 = jnp.arange(rows // bm, dtype=F32) * bm
    block_e = jnp.minimum(jnp.sum(pend[None, :] <= block_row0[:, None], axis=1), n_e - 1)
    n_used = (pend[-1:] / bm).astype(jnp.int32)
    dest_kt = dest.T

    xs_a = _sc_scatter_rows(h2pa, dest_kt, rows)
    xs_b = _sc_scatter_rows(h2pb, dest_kt, rows)
    ys = _ffn(block_e.astype(jnp.int32), n_used, xs_a, xs_b, w1, w3, w2, bm)
    pair_rows = dest_kt.reshape(-1)
    y8s = [_sc_gather_rows(y, pair_rows).reshape(TOP_K_EXPERTS, t, SC_ROW_WORDS) for y in ys]
    return _combine(y8s, g8, h2, x1, gate2, ws13, ws2b, seq, min(COMBINE_TILE, seq))


def _layer(x, c, positions, norm1_g, norm2_g, w_ada, b_ada, w_in, conv_w, q_norm_g, k_norm_g, kidx_norm_g, w_out,
           w_router, router_bias, w1, w3, w2, ws1, ws3, ws2):
    bsz, seq, d = x.shape
    t = bsz * seq
    tm = min(TOKEN_TILE, seq)
    tq = min(QUERY_TILE, seq)
    x2 = x.reshape(t, d)

    ada = _ada(c, w_ada, b_ada)
    shift1, scale1, gate1, shift2, scale2, gate2 = [a.reshape(bsz, 1, d) for a in jnp.split(ada, 6, axis=-1)]

    inv_freq = ROPE_THETA ** (-jnp.arange(ROPE_HALF, dtype=F32) / ROPE_HALF)
    invf_lane = inv_freq[(jnp.arange(LANES) % HEAD_DIM) % ROPE_HALF].reshape(1, LANES)
    ct, s1, s2 = _rope_tables(positions.reshape(t, 1), invf_lane, min(ROPE_TILE, t))

    conv, q, k, v, qi, kilo, kihi, wit = _inproj(x2, scale1, shift1, norm1_g, w_in, conv_w, q_norm_g, k_norm_g,
                                                kidx_norm_g, ct, s1, s2, seq, tm)
    attn = _attention(q, k, v, qi, kilo, kihi, wit, bsz, seq, tq, kc=tq)
    x1, h2, h2pa, h2pb, eidx, g8, cnt = _outproj(conv, attn, x2, gate1, scale2, shift2, norm2_g, w_out, w_router,
                                                 router_bias, seq, tm)
    out = _moe(h2, h2pa, h2pb, x1, eidx, g8, cnt[:, 0], gate2, w1, w3, w2, ws1, ws3, ws2, seq)
    return out.reshape(bsz, seq, d)


def kernel(x, c, positions, norm1_g, norm2_g, w_ada, b_ada, w_in, conv_w, q_norm_g, k_norm_g, kidx_norm_g, w_out,
           w_router, router_bias, w1, w3, w2, ws1, ws3, ws2):
    for l in range(w_in.shape[0]):
        x = _layer(x, c, positions, norm1_g[l], norm2_g[l], w_ada[l], b_ada[l], w_in[l], conv_w[l], q_norm_g[l],
                   k_norm_g[l], kidx_norm_g[l], w_out[l], w_router[l], router_bias[l], w1[l], w3[l], w2[l], ws1[l],
                   ws3[l], ws2[l])
    return x
```
